```python
import jax, jax.numpy as jnp
from jax import lax
import numpy as np

D_MODEL = 1024
BATCH = 4
SEQ = 8192
DEPTH = 1

CHUNK = 64
MIX_WIDTH = D_MODEL
HGRN_WIDTH = MIX_WIDTH // 2
HGRN_HEAD_DIM = 128
HGRN_HEADS = HGRN_WIDTH // HGRN_HEAD_DIM
POOL_WIDTH = MIX_WIDTH - HGRN_WIDTH
POOL_WINDOWS = (2, 4, 8, 16)
N_POOL_GROUPS = len(POOL_WINDOWS)
POOL_GW = POOL_WIDTH // N_POOL_GROUPS
IN_COLS = 4 * HGRN_WIDTH + POOL_WIDTH
N_EXPERTS = 32
TOP_K = 4
D_FF = D_MODEL
SWIGLU_LIMIT = 7.0
SWIGLU_ALPHA = 1.702
EXPERT_BLOCK = 256
NORM_EPS = 1e-6

kernel_name = 'hybrid_hgrn2_pool_moe'


def rmsnorm(x, g):
    xf = x.astype(jnp.float32)
    y = xf * lax.rsqrt(jnp.mean(xf * xf, axis=-1, keepdims=True) + NORM_EPS)
    return (y * g.astype(jnp.float32)).astype(x.dtype)


def hgrn2(q_pre, f_pre, v, g_pre, lb, norm_g):
    B_, S_, _ = q_pre.shape
    n_chunks = S_ // CHUNK
    q = jax.nn.silu(q_pre.astype(jnp.float32))
    f = lb + (1.0 - lb) * jax.nn.sigmoid(f_pre.astype(jnp.float32))
    log_f = jnp.log(f)
    k = 1.0 - f

    def to_chunks(a):
        a = a.astype(jnp.float32).reshape(B_, n_chunks, CHUNK, HGRN_HEADS, HGRN_HEAD_DIM)
        return a.transpose(1, 0, 3, 2, 4)

    causal = jnp.tril(jnp.ones((CHUNK, CHUNK), dtype=bool))

    def step(state, inp):
        qc, kc, vc, lfc = inp
        b = jnp.cumsum(lfc, axis=2)
        b_last = b[:, :, -1, :]
        o_inter = jnp.einsum('bhtk,bhkv->bhtv', qc * jnp.exp(b), state)
        diff = b[:, :, :, None, :] - b[:, :, None, :, :]
        decay = jnp.exp(jnp.where(causal[None, None, :, :, None], diff, -jnp.inf))
        scores = jnp.einsum('bhtk,bhtsk,bhsk->bhts', qc, decay, kc)
        o = o_inter + jnp.einsum('bhts,bhsv->bhtv', scores, vc)
        new_state = jnp.exp(b_last)[..., None] * state + jnp.einsum(
            'bhsk,bhsv->bhkv', kc * jnp.exp(b_last[:, :, None, :] - b), vc)
        return new_state, o

    state0 = jnp.zeros((B_, HGRN_HEADS, HGRN_HEAD_DIM, HGRN_HEAD_DIM), jnp.float32)
    _, o = lax.scan(step, state0, (to_chunks(q), to_chunks(k), to_chunks(v), to_chunks(log_f)))
    o = o.transpose(1, 0, 3, 2, 4).reshape(B_, S_, HGRN_HEADS, HGRN_HEAD_DIM)
    o = o * lax.rsqrt(jnp.mean(o * o, axis=-1, keepdims=True) + NORM_EPS) * norm_g.astype(jnp.float32)
    gate = jax.nn.silu(g_pre.astype(jnp.float32)).reshape(B_, S_, HGRN_HEADS, HGRN_HEAD_DIM)
    return (o * gate).reshape(B_, S_, HGRN_WIDTH).astype(q_pre.dtype)


def pool_mixer(u, w, scale):
    B_, S_, _ = u.shape
    ug = u.astype(jnp.float32).reshape(B_, S_, N_POOL_GROUPS, POOL_GW)
    cs = jnp.cumsum(ug, axis=1)
    pos = jnp.arange(1, S_ + 1, dtype=jnp.float32)[None, :, None]
    outs = []
    for gi, win in enumerate(POOL_WINDOWS):
        c = cs[:, :, gi, :]
        lower = jnp.pad(c, ((0, 0), (win, 0), (0, 0)))[:, :S_]
        mean = (c - lower) / jnp.minimum(pos, float(win))
        outs.append(mean - ug[:, :, gi, :])
    d = jnp.stack(outs, axis=2)
    y = jnp.einsum('bsgc,gcd->bsgd', d, w.astype(jnp.float32))
    return (y.reshape(B_, S_, POOL_WIDTH) * scale.astype(jnp.float32)).astype(u.dtype)


def moe(h, w_r, b_r, w_g, b_g, w_u, b_u, w_d, b_d):
    B_, S_, D = h.shape
    N = B_ * S_
    t = h.reshape(N, D)
    logits = (t @ w_r + b_r).astype(jnp.float32)
    top_logit, top_e = lax.top_k(logits, TOP_K)
    gate = jax.nn.softmax(top_logit, axis=-1)
    M = N * TOP_K
    e_flat = top_e.reshape(M)
    order = jnp.argsort(e_flat)
    e_sorted = e_flat[order]
    tok_sorted = (order // TOP_K).astype(jnp.int32)
    gate_sorted = gate.reshape(M)[order]
    counts = jnp.bincount(e_flat, length=N_EXPERTS)
    start = jnp.cumsum(counts) - counts
    padded = (counts + EXPERT_BLOCK - 1) // EXPERT_BLOCK * EXPERT_BLOCK
    pad_end = jnp.cumsum(padded)
    pad_start = pad_end - padded
    dest = pad_start[e_sorted] + (jnp.arange(M) - start[e_sorted])
    P = M + N_EXPERTS * EXPERT_BLOCK
    n_blocks = P // EXPERT_BLOCK
    row_tok = jnp.full((P,), N, jnp.int32).at[dest].set(tok_sorted)
    row_gate = jnp.zeros((P,), jnp.float32).at[dest].set(gate_sorted)
    blk_e = jnp.minimum(jnp.searchsorted(pad_end, jnp.arange(n_blocks) * EXPERT_BLOCK, side='right'),
                        N_EXPERTS - 1)
    t_pad = jnp.concatenate([t, jnp.zeros((1, D), t.dtype)], axis=0)
    xs = t_pad[row_tok].reshape(n_blocks, EXPERT_BLOCK, D)

    def expert_block(args):
        xb, e = args
        gt = jnp.minimum(xb @ w_g[e] + b_g[e], SWIGLU_LIMIT)
        up = jnp.clip(xb @ w_u[e] + b_u[e], -SWIGLU_LIMIT, SWIGLU_LIMIT)
        act = (up + 1.0) * (gt * jax.nn.sigmoid(SWIGLU_ALPHA * gt))
        return act @ w_d[e] + b_d[e]

    ys = lax.map(expert_block, (xs, blk_e)).reshape(P, D)
    ys = ys * row_gate.astype(ys.dtype)[:, None]
    out = jax.ops.segment_sum(ys, row_tok, num_segments=N + 1)[:N]
    return out.reshape(B_, S_, D)


def setup_inputs(seed: int = 0) -> dict:
    key = jax.random.key(seed)
    ks = jax.random.split(key, 20)

    def nrm(k, shape, scale):
        return jax.random.normal(k, shape, jnp.float32) * scale

    return {
        'x': nrm(ks[0], (BATCH, SEQ, D_MODEL), 1.0),
        'norm1_g': 1.0 + nrm(ks[1], (DEPTH, D_MODEL), 0.02),
        'w_in': nrm(ks[2], (DEPTH, D_MODEL, IN_COLS), D_MODEL ** -0.5),
        'hgrn_lb': nrm(ks[3], (DEPTH + 1, HGRN_WIDTH), 0.5),
        'hgrn_norm_g': 1.0 + nrm(ks[4], (DEPTH, HGRN_HEAD_DIM), 0.02),
        'pool_w': nrm(ks[5], (DEPTH, N_POOL_GROUPS, POOL_GW, POOL_GW), POOL_GW ** -0.5),
        'pool_scale': 1.0 + nrm(ks[6], (DEPTH, POOL_WIDTH), 0.02),
        'w_out': nrm(ks[7], (DEPTH, MIX_WIDTH, D_MODEL), MIX_WIDTH ** -0.5),
        'norm2_g': 1.0 + nrm(ks[8], (DEPTH, D_MODEL), 0.02),
        'router_w': nrm(ks[9], (DEPTH, D_MODEL, N_EXPERTS), D_MODEL ** -0.5),
        'router_b': nrm(ks[10], (DEPTH, N_EXPERTS), 0.01),
        'w_gate': nrm(ks[11], (DEPTH, N_EXPERTS, D_MODEL, D_FF), D_MODEL ** -0.5),
        'b_gate': nrm(ks[12], (DEPTH, N_EXPERTS, D_FF), 0.01),
        'w_up': nrm(ks[13], (DEPTH, N_EXPERTS, D_MODEL, D_FF), D_MODEL ** -0.5),
        'b_up': nrm(ks[14], (DEPTH, N_EXPERTS, D_FF), 0.01),
        'w_down': nrm(ks[15], (DEPTH, N_EXPERTS, D_FF, D_MODEL), D_FF ** -0.5),
        'b_down': nrm(ks[16], (DEPTH, N_EXPERTS, D_MODEL), 0.01),
        'norm_f_g': 1.0 + nrm(ks[17], (D_MODEL,), 0.02),
    }


def reference(x, norm1_g, w_in, hgrn_lb, hgrn_norm_g, pool_w, pool_scale, w_out, norm2_g,
              router_w, router_b, w_gate, b_gate, w_up, b_up, w_down, b_down, norm_f_g):
    lb_all = jnp.cumsum(jax.nn.softmax(hgrn_lb.astype(jnp.float32), axis=0), axis=0)
    split_at = [HGRN_WIDTH, 2 * HGRN_WIDTH, 3 * HGRN_WIDTH, 4 * HGRN_WIDTH]
    for l in range(DEPTH):
        h = rmsnorm(x, norm1_g[l])
        proj = h @ w_in[l]
        q_pre, f_pre, v_in, g_pre, u_pool = jnp.split(proj, split_at, axis=-1)
        o_rec = hgrn2(q_pre, f_pre, v_in, g_pre, lb_all[l], hgrn_norm_g[l])
        o_pool = pool_mixer(u_pool, pool_w[l], pool_scale[l])
        x = x + jnp.concatenate([o_rec, o_pool], axis=-1) @ w_out[l]
        h = rmsnorm(x, norm2_g[l])
        x = x + moe(h, router_w[l], router_b[l], w_gate[l], b_gate[l], w_up[l], b_up[l],
                    w_down[l], b_down[l])
    return rmsnorm(x, norm_f_g)
```

```python
import functools
import math

import numpy as np
import jax
import jax.numpy as jnp
from jax import lax
from jax.experimental import pallas as pl
from jax.experimental.pallas import tpu as pltpu

F32 = jnp.float32
BF16 = jnp.bfloat16

D_MODEL = 1024
HGRN_WIDTH = 512
HEAD_DIM = 128
N_HEADS = HGRN_WIDTH // HEAD_DIM
POOL_WIDTH = 512
POOL_WINDOWS = (2, 4, 8, 16)
POOL_GW = POOL_WIDTH // len(POOL_WINDOWS)
IN_COLS = 4 * HGRN_WIDTH + POOL_WIDTH
N_EXPERTS = 32
TOP_K = 4
SWIGLU_LIMIT = 7.0
SWIGLU_ALPHA = 1.702
NORM_EPS = 1e-6

LANES = 128
VMEM_LIMIT_BYTES = 56 * 1024 * 1024

MIX_TOKENS = 512
CHUNK = 64
N_LEVELS = int(math.log2(CHUNK))
HALO = max(POOL_WINDOWS)
ROW_BLOCK = 256
DISPATCH_TOKENS = 256
COMBINE_TOKENS = 256
NEG_BIG = -1e30


def _dot(a, b):
    return jnp.dot(a, b, preferred_element_type=F32)


def _dot_nt(a, b):
    return lax.dot_general(a, b, (((1,), (1,)), ((), ())), preferred_element_type=F32)


def _dot_tn(a, b):
    return lax.dot_general(a, b, (((0,), (0,)), ((), ())), preferred_element_type=F32)


def _sigmoid(x):
    return 1.0 / (1.0 + jnp.exp(-x))


def _rms(x, g):
    return x * lax.rsqrt(jnp.mean(x * x, axis=-1, keepdims=True) + NORM_EPS) * g


def _split_bf16(x):
    hi = x.astype(BF16)
    lo = (x - hi.astype(F32)).astype(BF16)
    return hi, lo


def _decay_tables(ch):
    t = np.arange(ch)[:, None]
    u = np.arange(ch)[None, :]
    mats = [u <= t, u > t]
    masks = []
    m = 1
    while m < ch:
        right = (t // m) % 2 == 1
        mid = (t // (2 * m)) * (2 * m) + m
        mats.append(np.where(right, (u >= mid) & (u <= t), (u > t) & (u <= mid - 1)))
        masks.append(right & ((u // m) % 2 == 0) & ((u // (2 * m)) == (t // (2 * m))))
        m *= 2
    return (np.concatenate(mats, 0).astype(np.float32),
            np.stack(masks).astype(np.float32))


def _hgrn_chunk(qp, fp, v, gp, lb, hg, tabs, masks_ref, st_t):
    q = qp * _sigmoid(qp)
    f = lb + (1.0 - lb) * _sigmoid(fp)
    lf = jnp.log(f)
    k = 1.0 - f
    lf_hi, lf_lo = _split_bf16(lf)
    x = jnp.exp(_dot(tabs, lf_hi) + _dot(tabs, lf_lo))
    eb = x[0:CHUNK]
    esuf = x[CHUNK:2 * CHUNK]
    vb = v.astype(BF16)
    o = _dot_nt((q * eb).astype(BF16), st_t.astype(BF16))
    o = o + jnp.sum(q * k, axis=-1, keepdims=True) * v
    s = jnp.zeros((CHUNK, CHUNK), F32)
    for lvl in range(N_LEVELS):
        xl = x[(2 + lvl) * CHUNK:(3 + lvl) * CHUNK]
        s = s + masks_ref[lvl] * _dot_nt((q * xl).astype(BF16), (k * xl).astype(BF16))
    o = o + _dot(s.astype(BF16), vb)
    new_st = eb[CHUNK - 1:CHUNK, :] * st_t + _dot_tn(vb, (k * esuf).astype(BF16))
    o = o * lax.rsqrt(jnp.mean(o * o, axis=-1, keepdims=True) + NORM_EPS) * hg
    out = o * (gp * _sigmoid(gp))
    return out, new_st


def _mixer_kernel(x_ref, g1_ref, win_ref, lb_ref, hg_ref, pw_ref, ps_ref, wout_ref, g2_ref,
                  wrh_ref, wrl_ref, br_ref, tabs_ref, masks_ref, tri_ref,
                  x1_ref, h2_ref, route_ref, cnt_ref,
                  proj_scr, mix_scr, st_scr, ubuf_scr, cnt_scr):
    b_i = pl.program_id(0)
    s_i = pl.program_id(1)

    @pl.when(s_i == 0)
    def _():
        st_scr[...] = jnp.zeros_like(st_scr)
        ubuf_scr[0:HALO, :] = jnp.zeros((HALO, POOL_WIDTH), F32)

    @pl.when((b_i == 0) & (s_i == 0))
    def _():
        cnt_scr[...] = jnp.zeros_like(cnt_scr)

    x = x_ref[...]
    h = _rms(x, g1_ref[...]).astype(BF16)
    proj_scr[...] = _dot(h, win_ref[...])

    tabs = tabs_ref[...]

    def chunk_body(c, carry):
        rows = pl.ds(pl.multiple_of(c * CHUNK, CHUNK), CHUNK)
        for hd in range(N_HEADS):
            lo = hd * HEAD_DIM
            cs = slice(lo, lo + HEAD_DIM)
            out, new_st = _hgrn_chunk(
                proj_scr[rows, lo:lo + HEAD_DIM],
                proj_scr[rows, HGRN_WIDTH + lo:HGRN_WIDTH + lo + HEAD_DIM],
                proj_scr[rows, 2 * HGRN_WIDTH + lo:2 * HGRN_WIDTH + lo + HEAD_DIM],
                proj_scr[rows, 3 * HGRN_WIDTH + lo:3 * HGRN_WIDTH + lo + HEAD_DIM],
                lb_ref[:, cs], hg_ref[...], tabs, masks_ref, st_scr[hd])
            st_scr[hd] = new_st
            mix_scr[rows, cs] = out.astype(BF16)
        return carry

    lax.fori_loop(0, MIX_TOKENS // CHUNK, chunk_body, 0)

    ubuf_scr[HALO:HALO + MIX_TOKENS, :] = proj_scr[:, 4 * HGRN_WIDTH:IN_COLS]
    pos = (s_i * MIX_TOKENS + 1
           + lax.broadcasted_iota(jnp.int32, (MIX_TOKENS, 1), 0)).astype(F32)
    for g, win in enumerate(POOL_WINDOWS):
        lo = g * POOL_GW
        u = ubuf_scr[HALO:HALO + MIX_TOKENS, lo:lo + POOL_GW]
        acc = u
        for j in range(1, win):
            acc = acc + ubuf_scr[HALO - j:HALO - j + MIX_TOKENS, lo:lo + POOL_GW]
        d = acc / jnp.minimum(pos, float(win)) - u
        y = _dot(d.astype(BF16), pw_ref[g]) * ps_ref[:, lo:lo + POOL_GW]
        mix_scr[:, HGRN_WIDTH + lo:HGRN_WIDTH + lo + POOL_GW] = y.astype(BF16)
    ubuf_scr[0:HALO, :] = ubuf_scr[MIX_TOKENS:MIX_TOKENS + HALO, :]

    x1 = x + _dot(mix_scr[...], wout_ref[...])
    x1_ref[...] = x1
    h2 = _rms(x1, g2_ref[...])
    h2_ref[...] = h2

    h_hi, h_lo = _split_bf16(h2)
    wrh = wrh_ref[...]
    logits = _dot(h_hi, wrh) + _dot(h_lo, wrh) + _dot(h_hi, wrl_ref[...]) + br_ref[...]
    lane = lax.broadcasted_iota(jnp.int32, (MIX_TOKENS, LANES), 1)
    work = logits
    tops, idxs, sels = [], [], []
    for _ in range(TOP_K):
        m = jnp.max(work, axis=-1, keepdims=True)
        idx = jnp.min(jnp.where(work == m, lane, LANES), axis=-1, keepdims=True)
        sel = lane == idx
        work = jnp.where(sel, 2.0 * NEG_BIG, work)
        tops.append(m)
        idxs.append(idx)
        sels.append(sel)
    exps = [jnp.exp(m - tops[0]) for m in tops]
    denom = exps[0] + exps[1] + exps[2] + exps[3]
    gates = [e / denom for e in exps]

    sel_any = jnp.zeros((MIX_TOKENS, LANES), F32)
    for sel in sels:
        sel_any = jnp.where(sel, 1.0, sel_any)
    cnt = cnt_scr[...]
    before = _dot(tri_ref[...], sel_any.astype(BF16)) + cnt
    cnt_new = cnt + jnp.sum(sel_any, axis=0, keepdims=True)
    cnt_scr[...] = cnt_new
    cnt_ref[...] = jnp.broadcast_to(cnt_new, cnt_ref.shape)

    route = jnp.zeros((MIX_TOKENS, LANES), F32)
    for k in range(TOP_K):
        slot = jnp.sum(jnp.where(sels[k], before, 0.0), axis=-1, keepdims=True)
        route = jnp.where(lane == k, idxs[k].astype(F32), route)
        route = jnp.where(lane == TOP_K + k, gates[k], route)
        route = jnp.where(lane == 2 * TOP_K + k, slot, route)
    route_ref[...] = route


def _mixer(x2d, g1, w_in, lb, hg, pool_w, pool_scale, w_out, g2, wr_hi, wr_lo, br, seq):
    n = x2d.shape[0]
    steps_per_seq = seq // MIX_TOKENS
    tabs_np, masks_np = _decay_tables(CHUNK)
    tabs = jnp.asarray(tabs_np, BF16)
    masks = jnp.asarray(masks_np, F32)
    tri = jnp.asarray(np.tril(np.ones((MIX_TOKENS, MIX_TOKENS), np.float32), -1), BF16)

    def tok(b, s):
        return (b * steps_per_seq + s, 0)

    def const2(b, s):
        return (0, 0)

    def const3(b, s):
        return (0, 0, 0)

    in_specs = [
        pl.BlockSpec((MIX_TOKENS, D_MODEL), tok),
        pl.BlockSpec((1, D_MODEL), const2),
        pl.BlockSpec((D_MODEL, IN_COLS), const2),
        pl.BlockSpec((1, HGRN_WIDTH), const2),
        pl.BlockSpec((1, HEAD_DIM), const2),
        pl.BlockSpec((len(POOL_WINDOWS), POOL_GW, POOL_GW), const3),
        pl.BlockSpec((1, POOL_WIDTH), const2),
        pl.BlockSpec((D_MODEL, D_MODEL), const2),
        pl.BlockSpec((1, D_MODEL), const2),
        pl.BlockSpec((D_MODEL, LANES), const2),
        pl.BlockSpec((D_MODEL, LANES), const2),
        pl.BlockSpec((1, LANES), const2),
        pl.BlockSpec(tabs.shape, const2),
        pl.BlockSpec(masks.shape, const3),
        pl.BlockSpec(tri.shape, const2),
    ]
    out_specs = [
        pl.BlockSpec((MIX_TOKENS, D_MODEL), tok),
        pl.BlockSpec((MIX_TOKENS, D_MODEL), tok),
        pl.BlockSpec((MIX_TOKENS, LANES), tok),
        pl.BlockSpec((8, LANES), const2),
    ]
    out_shape = [
        jax.ShapeDtypeStruct((n, D_MODEL), F32),
        jax.ShapeDtypeStruct((n, D_MODEL), F32),
        jax.ShapeDtypeStruct((n, LANES), F32),
        jax.ShapeDtypeStruct((8, LANES), F32),
    ]
    scratch = [
        pltpu.VMEM((MIX_TOKENS, IN_COLS), F32),
        pltpu.VMEM((MIX_TOKENS, D_MODEL), BF16),
        pltpu.VMEM((N_HEADS, HEAD_DIM, HEAD_DIM), F32),
        pltpu.VMEM((HALO + MIX_TOKENS, POOL_WIDTH), F32),
        pltpu.VMEM((1, LANES), F32),
    ]
    return pl.pallas_call(
        _mixer_kernel,
        grid=(n // seq, steps_per_seq),
        in_specs=in_specs,
        out_specs=out_specs,
        out_shape=out_shape,
        scratch_shapes=scratch,
        compiler_params=pltpu.CompilerParams(
            dimension_semantics=("arbitrary", "arbitrary"),
            vmem_limit_bytes=VMEM_LIMIT_BYTES),
        name="mixer",
    )(x2d, g1, w_in, lb, hg, pool_w, pool_scale, w_out, g2, wr_hi, wr_lo, br, tabs, masks, tri)


def _row_copy(src, src_row, dst, dst_row, sem):
    return pltpu.make_async_copy(src.at[pl.ds(src_row, 1), :], dst.at[pl.ds(dst_row, 1), :], sem)


def _dispatch_kernel(cnt_ref, pstart_ref, pend_ref, dest_ref, h2_ref, xs_ref, zrow, sem, zsem):
    i = pl.program_id(0)

    @pl.when(i == 0)
    def _():
        zrow[...] = jnp.zeros_like(zrow)
        n_blocks = xs_ref.shape[0] // ROW_BLOCK

        def block_copy(b):
            return pltpu.make_async_copy(
                zrow, xs_ref.at[pl.ds(pl.multiple_of(b * ROW_BLOCK, ROW_BLOCK), ROW_BLOCK), :],
                zsem)

        def block_start(b, carry):
            block_copy(b).start()
            return carry

        def block_wait(b, carry):
            block_copy(b).wait()
            return carry

        first_unused = pend_ref[N_EXPERTS - 1] // ROW_BLOCK
        lax.fori_loop(first_unused, n_blocks, block_start, 0)
        lax.fori_loop(first_unused, n_blocks, block_wait, 0)

        def expert_body(e, carry):
            first = pstart_ref[e] + cnt_ref[e]
            last = pend_ref[e]

            def start_body(r, c):
                _row_copy(zrow, 0, xs_ref, r, zsem).start()
                return c

            def wait_body(r, c):
                _row_copy(zrow, 0, xs_ref, r, zsem).wait()
                return c

            lax.fori_loop(first, last, start_body, 0)
            lax.fori_loop(first, last, wait_body, 0)
            return carry

        lax.fori_loop(0, N_EXPERTS, expert_body, 0)

    def row_body(r, carry):
        for k in range(TOP_K):
            _row_copy(h2_ref, r, xs_ref, dest_ref[r * TOP_K + k], sem).start()
        return carry

    lax.fori_loop(0, DISPATCH_TOKENS, row_body, 0)
    for k in range(TOP_K):
        pltpu.make_async_copy(h2_ref, xs_ref.at[pl.ds(0, DISPATCH_TOKENS), :], sem).wait()


def _dispatch(cnt, pstart, pend, dest, h2, n_rows):
    n = h2.shape[0]
    grid_spec = pltpu.PrefetchScalarGridSpec(
        num_scalar_prefetch=3,
        grid=(n // DISPATCH_TOKENS,),
        in_specs=[
            pl.BlockSpec((DISPATCH_TOKENS * TOP_K,), lambda i, *_: (i,),
                         memory_space=pltpu.SMEM),
            pl.BlockSpec((DISPATCH_TOKENS, D_MODEL), lambda i, *_: (i, 0)),
        ],
        out_specs=pl.BlockSpec(memory_space=pl.ANY),
        scratch_shapes=[
            pltpu.VMEM((ROW_BLOCK, D_MODEL), F32),
            pltpu.SemaphoreType.DMA,
            pltpu.SemaphoreType.DMA,
        ],
    )
    return pl.pallas_call(
        _dispatch_kernel,
        grid_spec=grid_spec,
        out_shape=jax.ShapeDtypeStruct((n_rows, D_MODEL), F32),
        compiler_params=pltpu.CompilerParams(
            dimension_semantics=("arbitrary",),
            vmem_limit_bytes=VMEM_LIMIT_BYTES),
        name="dispatch",
    )(cnt, pstart, pend, dest, h2)


def _expert_kernel(blk_e_ref, nused_ref, xs_ref, wg_ref, bg_ref, wu_ref, bu_ref, wd_ref, bd_ref,
                   ys_ref, wg_s, wu_s, wd_s):
    i = pl.program_id(0)
    active = i < nused_ref[0]
    changed = (i == 0) | (blk_e_ref[i] != blk_e_ref[jnp.maximum(i - 1, 0)])

    @pl.when(active & changed)
    def _():
        wg_s[...] = wg_ref[0].astype(BF16)
        wu_s[...] = wu_ref[0].astype(BF16)
        wd_s[...] = wd_ref[0].astype(BF16)

    @pl.when(active)
    def _():
        xb = xs_ref[...].astype(BF16)
        gt = jnp.minimum(_dot(xb, wg_s[...]) + bg_ref[0], SWIGLU_LIMIT)
        up = jnp.clip(_dot(xb, wu_s[...]) + bu_ref[0], -SWIGLU_LIMIT, SWIGLU_LIMIT)
        act = (up + 1.0) * (gt * _sigmoid(SWIGLU_ALPHA * gt))
        ys_ref[...] = _dot(act.astype(BF16), wd_s[...]) + bd_ref[0]

    @pl.when(jnp.logical_not(active))
    def _():
        ys_ref[...] = jnp.zeros_like(ys_ref)


def _experts(blk_e, n_used, xs, w_gate, b_gate, w_up, b_up, w_down, b_down):
    n_rows = xs.shape[0]
    n_blocks = n_rows // ROW_BLOCK

    def blk(i, be, nu):
        return jnp.minimum(i, nu[0] - 1)

    def row_map(i, be, nu):
        return (blk(i, be, nu), 0)

    def w_map(i, be, nu):
        return (be[blk(i, be, nu)], 0, 0)

    w_spec = pl.BlockSpec((1, D_MODEL, D_MODEL), w_map)
    b_spec = pl.BlockSpec((1, 1, D_MODEL), w_map)
    grid_spec = pltpu.PrefetchScalarGridSpec(
        num_scalar_prefetch=2,
        grid=(n_blocks,),
        in_specs=[pl.BlockSpec((ROW_BLOCK, D_MODEL), row_map),
                  w_spec, b_spec, w_spec, b_spec, w_spec, b_spec],
        out_specs=pl.BlockSpec((ROW_BLOCK, D_MODEL), lambda i, be, nu: (i, 0)),
        scratch_shapes=[pltpu.VMEM((D_MODEL, D_MODEL), BF16)] * 3,
    )
    return pl.pallas_call(
        _expert_kernel,
        grid_spec=grid_spec,
        out_shape=jax.ShapeDtypeStruct((n_rows, D_MODEL), F32),
        compiler_params=pltpu.CompilerParams(
            dimension_semantics=("arbitrary",),
            vmem_limit_bytes=VMEM_LIMIT_BYTES),
        name="experts",
    )(blk_e, n_used, xs, w_gate, b_gate[:, None, :], w_up, b_up[:, None, :],
      w_down, b_down[:, None, :])


def _combine_kernel(final, dest_ref, ys_ref, x1_ref, route_ref, gf_ref, out_ref, buf, sem):
    def row_body(r, carry):
        for k in range(TOP_K):
            pltpu.make_async_copy(ys_ref.at[pl.ds(dest_ref[r * TOP_K + k], 1), :],
                                  buf.at[k, pl.ds(r, 1), :], sem).start()
        return carry

    lax.fori_loop(0, COMBINE_TOKENS, row_body, 0)
    for k in range(TOP_K):
        pltpu.make_async_copy(ys_ref.at[pl.ds(0, COMBINE_TOKENS), :], buf.at[k], sem).wait()

    route = route_ref[...]
    acc = x1_ref[...]
    for k in range(TOP_K):
        acc = acc + route[:, TOP_K + k:TOP_K + k + 1] * buf[k]
    out_ref[...] = _rms(acc, gf_ref[...]) if final else acc


def _combine(dest, ys, x1, route, gf, final):
    n = x1.shape[0]
    return pl.pallas_call(
        functools.partial(_combine_kernel, final),
        grid=(n // COMBINE_TOKENS,),
        in_specs=[
            pl.BlockSpec((COMBINE_TOKENS * TOP_K,), lambda i: (i,), memory_space=pltpu.SMEM),
            pl.BlockSpec(memory_space=pl.ANY),
            pl.BlockSpec((COMBINE_TOKENS, D_MODEL), lambda i: (i, 0)),
            pl.BlockSpec((COMBINE_TOKENS, LANES), lambda i: (i, 0)),
            pl.BlockSpec((1, D_MODEL), lambda i: (0, 0)),
        ],
        out_specs=pl.BlockSpec((COMBINE_TOKENS, D_MODEL), lambda i: (i, 0)),
        out_shape=jax.ShapeDtypeStruct((n, D_MODEL), F32),
        scratch_shapes=[
            pltpu.VMEM((TOP_K, COMBINE_TOKENS, D_MODEL), F32),
            pltpu.SemaphoreType.DMA,
        ],
        compiler_params=pltpu.CompilerParams(
            dimension_semantics=("arbitrary",),
            vmem_limit_bytes=VMEM_LIMIT_BYTES),
        name="combine",
    )(dest, ys, x1, route, gf)


def kernel(x, norm1_g, w_in, hgrn_lb, hgrn_norm_g, pool_w, pool_scale, w_out, norm2_g,
           router_w, router_b, w_gate, b_gate, w_up, b_up, w_down, b_down, norm_f_g):
    depth = w_in.shape[0]
    batch, seq, _ = x.shape
    n = batch * seq
    n_rows = n * TOP_K + N_EXPERTS * ROW_BLOCK
    lb_all = jnp.cumsum(jax.nn.softmax(hgrn_lb.astype(F32), axis=0), axis=0)

    xt = x.reshape(n, D_MODEL)
    for l in range(depth):
        wr = jnp.pad(router_w[l], ((0, 0), (0, LANES - N_EXPERTS)))
        wr_hi = wr.astype(BF16)
        wr_lo = (wr - wr_hi.astype(F32)).astype(BF16)
        br = jnp.pad(router_b[l], (0, LANES - N_EXPERTS), constant_values=NEG_BIG)[None, :]
        x1, h2, route, cnt = _mixer(
            xt, norm1_g[l][None, :], w_in[l].astype(BF16), lb_all[l][None, :],
            hgrn_norm_g[l][None, :], pool_w[l].astype(BF16), pool_scale[l][None, :],
            w_out[l].astype(BF16), norm2_g[l][None, :], wr_hi, wr_lo, br, seq)

        counts = cnt[0, :N_EXPERTS].astype(jnp.int32)
        padded = (counts + ROW_BLOCK - 1) // ROW_BLOCK * ROW_BLOCK
        pend = jnp.cumsum(padded).astype(jnp.int32)
        pstart = pend - padded
        n_used = (pend[-1:] // ROW_BLOCK).astype(jnp.int32)
        blk_e = jnp.minimum(
            jnp.searchsorted(pend, jnp.arange(n_rows // ROW_BLOCK, dtype=jnp.int32) * ROW_BLOCK,
                             side='right'),
            N_EXPERTS - 1).astype(jnp.int32)
        top_e = route[:, 0:TOP_K].astype(jnp.int32)
        slot = route[:, 2 * TOP_K:3 * TOP_K].astype(jnp.int32)
        dest = (slot + pstart[top_e]).reshape(n * TOP_K)

        xs = _dispatch(counts, pstart, pend, dest, h2, n_rows)
        ys = _experts(blk_e, n_used, xs, w_gate[l], b_gate[l], w_up[l], b_up[l],
                      w_down[l], b_down[l])
        xt = _combine(dest, ys, x1, route, norm_f_g[None, :], l == depth - 1)
    return xt.reshape(batch, seq, D_MODEL)
```

```python
import functools
import math

import numpy as np
import jax
import jax.numpy as jnp
from jax import lax
from jax.experimental import pallas as pl
from jax.experimental.pallas import tpu as pltpu

F32 = jnp.float32
BF16 = jnp.bfloat16

D_MODEL = 1024
HGRN_WIDTH = 512
HEAD_DIM = 128
N_HEADS = HGRN_WIDTH // HEAD_DIM
POOL_WIDTH = 512
POOL_WINDOWS = (2, 4, 8, 16)
POOL_GW = POOL_WIDTH // len(POOL_WINDOWS)
IN_COLS = 4 * HGRN_WIDTH + POOL_WIDTH
N_EXPERTS = 32
TOP_K = 4
SWIGLU_LIMIT = 7.0
SWIGLU_ALPHA = 1.702
NORM_EPS = 1e-6

LANES = 128
VMEM_LIMIT_BYTES = 56 * 1024 * 1024

MIX_TOKENS = 512
CHUNK = 64
N_LEVELS = int(math.log2(CHUNK))
HALO = max(POOL_WINDOWS)
ROW_BLOCK = 512
DISPATCH_TOKENS = 256
COMBINE_TOKENS = 256
NEG_BIG = -1e30


def _dot(a, b):
    return jnp.dot(a, b, preferred_element_type=F32)


def _dot_nt(a, b):
    return lax.dot_general(a, b, (((1,), (1,)), ((), ())), preferred_element_type=F32)


def _dot_tn(a, b):
    return lax.dot_general(a, b, (((0,), (0,)), ((), ())), preferred_element_type=F32)


def _sigmoid(x):
    return 1.0 / (1.0 + jnp.exp(-x))


def _rms(x, g):
    return x * lax.rsqrt(jnp.mean(x * x, axis=-1, keepdims=True) + NORM_EPS) * g


def _split_bf16(x):
    hi = x.astype(BF16)
    lo = (x - hi.astype(F32)).astype(BF16)
    return hi, lo


def _decay_tables(ch):
    t = np.arange(ch)[:, None]
    u = np.arange(ch)[None, :]
    mats = [u <= t, u > t]
    masks = []
    m = 1
    while m < ch:
        right = (t // m) % 2 == 1
        mid = (t // (2 * m)) * (2 * m) + m
        mats.append(np.where(right, (u >= mid) & (u <= t), (u > t) & (u <= mid - 1)))
        masks.append(right & ((u // m) % 2 == 0) & ((u // (2 * m)) == (t // (2 * m))))
        m *= 2
    return (np.concatenate(mats, 0).astype(np.float32),
            np.stack(masks).astype(np.float32))


def _hgrn_chunk(qp, fp, v, gp, lb, hg, tabs, masks_ref, st_t):
    q = qp * _sigmoid(qp)
    f = lb + (1.0 - lb) * _sigmoid(fp)
    lf = jnp.log(f)
    k = 1.0 - f
    lf_hi, lf_lo = _split_bf16(lf)
    x = jnp.exp(_dot(tabs, lf_hi) + _dot(tabs, lf_lo))
    eb = x[0:CHUNK]
    esuf = x[CHUNK:2 * CHUNK]
    vb = v.astype(BF16)
    o = _dot_nt((q * eb).astype(BF16), st_t.astype(BF16))
    o = o + jnp.sum(q * k, axis=-1, keepdims=True) * v
    s = jnp.zeros((CHUNK, CHUNK), F32)
    for lvl in range(N_LEVELS):
        xl = x[(2 + lvl) * CHUNK:(3 + lvl) * CHUNK]
        s = s + masks_ref[lvl] * _dot_nt((q * xl).astype(BF16), (k * xl).astype(BF16))
    o = o + _dot(s.astype(BF16), vb)
    new_st = eb[CHUNK - 1:CHUNK, :] * st_t + _dot_tn(vb, (k * esuf).astype(BF16))
    o = o * lax.rsqrt(jnp.mean(o * o, axis=-1, keepdims=True) + NORM_EPS) * hg
    out = o * (gp * _sigmoid(gp))
    return out, new_st


def _mixer_kernel(x_ref, g1_ref, win_ref, lb_ref, hg_ref, pw_ref, ps_ref, wout_ref, g2_ref,
                  wrh_ref, wrl_ref, br_ref, tabs_ref, masks_ref, tri_ref,
                  x1_ref, h2_ref, route_ref, cnt_ref,
                  proj_scr, mix_scr, st_scr, ubuf_scr, cnt_scr):
    b_i = pl.program_id(0)
    s_i = pl.program_id(1)

    @pl.when(s_i == 0)
    def _():
        st_scr[...] = jnp.zeros_like(st_scr)
        ubuf_scr[0:HALO, :] = jnp.zeros((HALO, POOL_WIDTH), F32)

    @pl.when((b_i == 0) & (s_i == 0))
    def _():
        cnt_scr[...] = jnp.zeros_like(cnt_scr)

    x = x_ref[...]
    h = _rms(x, g1_ref[...]).astype(BF16)
    proj_scr[...] = _dot(h, win_ref[...])

    tabs = tabs_ref[...]

    def chunk_body(c, carry):
        rows = pl.ds(pl.multiple_of(c * CHUNK, CHUNK), CHUNK)
        for hd in range(N_HEADS):
            lo = hd * HEAD_DIM
            cs = slice(lo, lo + HEAD_DIM)
            out, new_st = _hgrn_chunk(
                proj_scr[rows, lo:lo + HEAD_DIM],
                proj_scr[rows, HGRN_WIDTH + lo:HGRN_WIDTH + lo + HEAD_DIM],
                proj_scr[rows, 2 * HGRN_WIDTH + lo:2 * HGRN_WIDTH + lo + HEAD_DIM],
                proj_scr[rows, 3 * HGRN_WIDTH + lo:3 * HGRN_WIDTH + lo + HEAD_DIM],
                lb_ref[:, cs], hg_ref[...], tabs, masks_ref, st_scr[hd])
            st_scr[hd] = new_st
            mix_scr[rows, cs] = out.astype(BF16)
        return carry

    lax.fori_loop(0, MIX_TOKENS // CHUNK, chunk_body, 0)

    ubuf_scr[HALO:HALO + MIX_TOKENS, :] = proj_scr[:, 4 * HGRN_WIDTH:IN_COLS]
    pos = (s_i * MIX_TOKENS + 1
           + lax.broadcasted_iota(jnp.int32, (MIX_TOKENS, 1), 0)).astype(F32)
    for g, win in enumerate(POOL_WINDOWS):
        lo = g * POOL_GW
        u = ubuf_scr[HALO:HALO + MIX_TOKENS, lo:lo + POOL_GW]
        acc = u
        for j in range(1, win):
            acc = acc + ubuf_scr[HALO - j:HALO - j + MIX_TOKENS, lo:lo + POOL_GW]
        d = acc / jnp.minimum(pos, float(win)) - u
        y = _dot(d.astype(BF16), pw_ref[g]) * ps_ref[:, lo:lo + POOL_GW]
        mix_scr[:, HGRN_WIDTH + lo:HGRN_WIDTH + lo + POOL_GW] = y.astype(BF16)
    ubuf_scr[0:HALO, :] = ubuf_scr[MIX_TOKENS:MIX_TOKENS + HALO, :]

    x1 = x + _dot(mix_scr[...], wout_ref[...])
    x1_ref[...] = x1
    h2 = _rms(x1, g2_ref[...])
    h2_ref[...] = h2

    h_hi, h_lo = _split_bf16(h2)
    wrh = wrh_ref[...]
    logits = _dot(h_hi, wrh) + _dot(h_lo, wrh) + _dot(h_hi, wrl_ref[...]) + br_ref[...]
    lane = lax.broadcasted_iota(jnp.int32, (MIX_TOKENS, LANES), 1)
    work = logits
    tops, idxs, sels = [], [], []
    for _ in range(TOP_K):
        m = jnp.max(work, axis=-1, keepdims=True)
        idx = jnp.min(jnp.where(work == m, lane, LANES), axis=-1, keepdims=True)
        sel = lane == idx
        work = jnp.where(sel, 2.0 * NEG_BIG, work)
        tops.append(m)
        idxs.append(idx)
        sels.append(sel)
    exps = [jnp.exp(m - tops[0]) for m in tops]
    denom = exps[0] + exps[1] + exps[2] + exps[3]
    gates = [e / denom for e in exps]

    sel_any = jnp.zeros((MIX_TOKENS, LANES), F32)
    for sel in sels:
        sel_any = jnp.where(sel, 1.0, sel_any)
    cnt = cnt_scr[...]
    before = _dot(tri_ref[...], sel_any.astype(BF16)) + cnt
    cnt_new = cnt + jnp.sum(sel_any, axis=0, keepdims=True)
    cnt_scr[...] = cnt_new
    cnt_ref[...] = jnp.broadcast_to(cnt_new, cnt_ref.shape)

    route = jnp.zeros((MIX_TOKENS, LANES), F32)
    for k in range(TOP_K):
        slot = jnp.sum(jnp.where(sels[k], before, 0.0), axis=-1, keepdims=True)
        route = jnp.where(lane == k, idxs[k].astype(F32), route)
        route = jnp.where(lane == TOP_K + k, gates[k], route)
        route = jnp.where(lane == 2 * TOP_K + k, slot, route)
    route_ref[...] = route


def _mixer(x2d, g1, w_in, lb, hg, pool_w, pool_scale, w_out, g2, wr_hi, wr_lo, br, seq):
    n = x2d.shape[0]
    steps_per_seq = seq // MIX_TOKENS
    tabs_np, masks_np = _decay_tables(CHUNK)
    tabs = jnp.asarray(tabs_np, BF16)
    masks = jnp.asarray(masks_np, F32)
    tri = jnp.asarray(np.tril(np.ones((MIX_TOKENS, MIX_TOKENS), np.float32), -1), BF16)

    def tok(b, s):
        return (b * steps_per_seq + s, 0)

    def const2(b, s):
        return (0, 0)

    def const3(b, s):
        return (0, 0, 0)

    in_specs = [
        pl.BlockSpec((MIX_TOKENS, D_MODEL), tok),
        pl.BlockSpec((1, D_MODEL), const2),
        pl.BlockSpec((D_MODEL, IN_COLS), const2),
        pl.BlockSpec((1, HGRN_WIDTH), const2),
        pl.BlockSpec((1, HEAD_DIM), const2),
        pl.BlockSpec((len(POOL_WINDOWS), POOL_GW, POOL_GW), const3),
        pl.BlockSpec((1, POOL_WIDTH), const2),
        pl.BlockSpec((D_MODEL, D_MODEL), const2),
        pl.BlockSpec((1, D_MODEL), const2),
        pl.BlockSpec((D_MODEL, LANES), const2),
        pl.BlockSpec((D_MODEL, LANES), const2),
        pl.BlockSpec((1, LANES), const2),
        pl.BlockSpec(tabs.shape, const2),
        pl.BlockSpec(masks.shape, const3),
        pl.BlockSpec(tri.shape, const2),
    ]
    out_specs = [
        pl.BlockSpec((MIX_TOKENS, D_MODEL), tok),
        pl.BlockSpec((MIX_TOKENS, D_MODEL), tok),
        pl.BlockSpec((MIX_TOKENS, LANES), tok),
        pl.BlockSpec((8, LANES), const2),
    ]
    out_shape = [
        jax.ShapeDtypeStruct((n, D_MODEL), F32),
        jax.ShapeDtypeStruct((n, D_MODEL), F32),
        jax.ShapeDtypeStruct((n, LANES), F32),
        jax.ShapeDtypeStruct((8, LANES), F32),
    ]
    scratch = [
        pltpu.VMEM((MIX_TOKENS, IN_COLS), F32),
        pltpu.VMEM((MIX_TOKENS, D_MODEL), BF16),
        pltpu.VMEM((N_HEADS, HEAD_DIM, HEAD_DIM), F32),
        pltpu.VMEM((HALO + MIX_TOKENS, POOL_WIDTH), F32),
        pltpu.VMEM((1, LANES), F32),
    ]
    return pl.pallas_call(
        _mixer_kernel,
        grid=(n // seq, steps_per_seq),
        in_specs=in_specs,
        out_specs=out_specs,
        out_shape=out_shape,
        scratch_shapes=scratch,
        compiler_params=pltpu.CompilerParams(
            dimension_semantics=("arbitrary", "arbitrary"),
            vmem_limit_bytes=VMEM_LIMIT_BYTES),
        name="mixer",
    )(x2d, g1, w_in, lb, hg, pool_w, pool_scale, w_out, g2, wr_hi, wr_lo, br, tabs, masks, tri)


def _row_copy(src, src_row, dst, dst_row, sem):
    return pltpu.make_async_copy(src.at[pl.ds(src_row, 1), :], dst.at[pl.ds(dst_row, 1), :], sem)


def _dispatch_kernel(cnt_ref, pstart_ref, pend_ref, dest_ref, h2_ref, xs_ref, zrow, sem, zsem):
    i = pl.program_id(0)

    @pl.when(i == 0)
    def _():
        zrow[...] = jnp.zeros_like(zrow)
        n_blocks = xs_ref.shape[0] // ROW_BLOCK

        def block_copy(b):
            return pltpu.make_async_copy(
                zrow, xs_ref.at[pl.ds(pl.multiple_of(b * ROW_BLOCK, ROW_BLOCK), ROW_BLOCK), :],
                zsem)

        def block_start(b, carry):
            block_copy(b).start()
            return carry

        def block_wait(b, carry):
            block_copy(b).wait()
            return carry

        first_unused = pend_ref[N_EXPERTS - 1] // ROW_BLOCK
        lax.fori_loop(first_unused, n_blocks, block_start, 0)
        lax.fori_loop(first_unused, n_blocks, block_wait, 0)

        def expert_body(e, carry):
            first = pstart_ref[e] + cnt_ref[e]
            last = pend_ref[e]

            def start_body(r, c):
                _row_copy(zrow, 0, xs_ref, r, zsem).start()
                return c

            def wait_body(r, c):
                _row_copy(zrow, 0, xs_ref, r, zsem).wait()
                return c

            lax.fori_loop(first, last, start_body, 0)
            lax.fori_loop(first, last, wait_body, 0)
            return carry

        lax.fori_loop(0, N_EXPERTS, expert_body, 0)

    def row_body(r, carry):
        for k in range(TOP_K):
            _row_copy(h2_ref, r, xs_ref, dest_ref[r * TOP_K + k], sem).start()
        return carry

    lax.fori_loop(0, DISPATCH_TOKENS, row_body, 0)
    for k in range(TOP_K):
        pltpu.make_async_copy(h2_ref, xs_ref.at[pl.ds(0, DISPATCH_TOKENS), :], sem).wait()


def _dispatch(cnt, pstart, pend, dest, h2, n_rows):
    n = h2.shape[0]
    grid_spec = pltpu.PrefetchScalarGridSpec(
        num_scalar_prefetch=3,
        grid=(n // DISPATCH_TOKENS,),
        in_specs=[
            pl.BlockSpec((DISPATCH_TOKENS * TOP_K,), lambda i, *_: (i,),
                         memory_space=pltpu.SMEM),
            pl.BlockSpec((DISPATCH_TOKENS, D_MODEL), lambda i, *_: (i, 0)),
        ],
        out_specs=pl.BlockSpec(memory_space=pl.ANY),
        scratch_shapes=[
            pltpu.VMEM((ROW_BLOCK, D_MODEL), F32),
            pltpu.SemaphoreType.DMA,
            pltpu.SemaphoreType.DMA,
        ],
    )
    return pl.pallas_call(
        _dispatch_kernel,
        grid_spec=grid_spec,
        out_shape=jax.ShapeDtypeStruct((n_rows, D_MODEL), F32),
        compiler_params=pltpu.CompilerParams(
            dimension_semantics=("arbitrary",),
            vmem_limit_bytes=VMEM_LIMIT_BYTES),
        name="dispatch",
    )(cnt, pstart, pend, dest, h2)


def _expert_kernel(blk_e_ref, nused_ref, xs_ref, wg_ref, bg_ref, wu_ref, bu_ref, wd_ref, bd_ref,
                   ys_ref, wg_s, wu_s, wd_s):
    i = pl.program_id(0)
    active = i < nused_ref[0]
    changed = (i == 0) | (blk_e_ref[i] != blk_e_ref[jnp.maximum(i - 1, 0)])

    @pl.when(active & changed)
    def _():
        wg_s[...] = wg_ref[0].astype(BF16)
        wu_s[...] = wu_ref[0].astype(BF16)
        wd_s[...] = wd_ref[0].astype(BF16)

    @pl.when(active)
    def _():
        xb = xs_ref[...].astype(BF16)
        gt = jnp.minimum(_dot(xb, wg_s[...]) + bg_ref[0], SWIGLU_LIMIT)
        up = jnp.clip(_dot(xb, wu_s[...]) + bu_ref[0], -SWIGLU_LIMIT, SWIGLU_LIMIT)
        act = (up + 1.0) * (gt * _sigmoid(SWIGLU_ALPHA * gt))
        ys_ref[...] = _dot(act.astype(BF16), wd_s[...]) + bd_ref[0]

    @pl.when(jnp.logical_not(active))
    def _():
        ys_ref[...] = jnp.zeros_like(ys_ref)


def _experts(blk_e, n_used, xs, w_gate, b_gate, w_up, b_up, w_down, b_down):
    n_rows = xs.shape[0]
    n_blocks = n_rows // ROW_BLOCK

    def blk(i, be, nu):
        return jnp.minimum(i, nu[0] - 1)

    def row_map(i, be, nu):
        return (blk(i, be, nu), 0)

    def w_map(i, be, nu):
        return (be[blk(i, be, nu)], 0, 0)

    w_spec = pl.BlockSpec((1, D_MODEL, D_MODEL), w_map)
    b_spec = pl.BlockSpec((1, 1, D_MODEL), w_map)
    grid_spec = pltpu.PrefetchScalarGridSpec(
        num_scalar_prefetch=2,
        grid=(n_blocks,),
        in_specs=[pl.BlockSpec((ROW_BLOCK, D_MODEL), row_map),
                  w_spec, b_spec, w_spec, b_spec, w_spec, b_spec],
        out_specs=pl.BlockSpec((ROW_BLOCK, D_MODEL), lambda i, be, nu: (i, 0)),
        scratch_shapes=[pltpu.VMEM((D_MODEL, D_MODEL), BF16)] * 3,
    )
    return pl.pallas_call(
        _expert_kernel,
        grid_spec=grid_spec,
        out_shape=jax.ShapeDtypeStruct((n_rows, D_MODEL), F32),
        compiler_params=pltpu.CompilerParams(
            dimension_semantics=("arbitrary",),
            vmem_limit_bytes=VMEM_LIMIT_BYTES),
        name="experts",
    )(blk_e, n_used, xs, w_gate, b_gate[:, None, :], w_up, b_up[:, None, :],
      w_down, b_down[:, None, :])


def _combine_kernel(final, dest_ref, ys_ref, x1_ref, route_ref, gf_ref, out_ref, buf, sem):
    def row_body(r, carry):
        for k in range(TOP_K):
            pltpu.make_async_copy(ys_ref.at[pl.ds(dest_ref[r * TOP_K + k], 1), :],
                                  buf.at[k, pl.ds(r, 1), :], sem).start()
        return carry

    lax.fori_loop(0, COMBINE_TOKENS, row_body, 0)
    for k in range(TOP_K):
        pltpu.make_async_copy(ys_ref.at[pl.ds(0, COMBINE_TOKENS), :], buf.at[k], sem).wait()

    route = route_ref[...]
    acc = x1_ref[...]
    for k in range(TOP_K):
        acc = acc + route[:, TOP_K + k:TOP_K + k + 1] * buf[k]
    out_ref[...] = _rms(acc, gf_ref[...]) if final else acc


def _combine(dest, ys, x1, route, gf, final):
    n = x1.shape[0]
    return pl.pallas_call(
        functools.partial(_combine_kernel, final),
        grid=(n // COMBINE_TOKENS,),
        in_specs=[
            pl.BlockSpec((COMBINE_TOKENS * TOP_K,), lambda i: (i,), memory_space=pltpu.SMEM),
            pl.BlockSpec(memory_space=pl.ANY),
            pl.BlockSpec((COMBINE_TOKENS, D_MODEL), lambda i: (i, 0)),
            pl.BlockSpec((COMBINE_TOKENS, LANES), lambda i: (i, 0)),
            pl.BlockSpec((1, D_MODEL), lambda i: (0, 0)),
        ],
        out_specs=pl.BlockSpec((COMBINE_TOKENS, D_MODEL), lambda i: (i, 0)),
        out_shape=jax.ShapeDtypeStruct((n, D_MODEL), F32),
        scratch_shapes=[
            pltpu.VMEM((TOP_K, COMBINE_TOKENS, D_MODEL), F32),
            pltpu.SemaphoreType.DMA,
        ],
        compiler_params=pltpu.CompilerParams(
            dimension_semantics=("arbitrary",),
            vmem_limit_bytes=VMEM_LIMIT_BYTES),
        name="combine",
    )(dest, ys, x1, route, gf)


def kernel(x, norm1_g, w_in, hgrn_lb, hgrn_norm_g, pool_w, pool_scale, w_out, norm2_g,
           router_w, router_b, w_gate, b_gate, w_up, b_up, w_down, b_down, norm_f_g):
    depth = w_in.shape[0]
    batch, seq, _ = x.shape
    n = batch * seq
    n_rows = n * TOP_K + N_EXPERTS * ROW_BLOCK
    lb_all = jnp.cumsum(jax.nn.softmax(hgrn_lb.astype(F32), axis=0), axis=0)

    xt = x.reshape(n, D_MODEL)
    for l in range(depth):
        wr = jnp.pad(router_w[l], ((0, 0), (0, LANES - N_EXPERTS)))
        wr_hi = wr.astype(BF16)
        wr_lo = (wr - wr_hi.astype(F32)).astype(BF16)
        br = jnp.pad(router_b[l], (0, LANES - N_EXPERTS), constant_values=NEG_BIG)[None, :]
        x1, h2, route, cnt = _mixer(
            xt, norm1_g[l][None, :], w_in[l].astype(BF16), lb_all[l][None, :],
            hgrn_norm_g[l][None, :], pool_w[l].astype(BF16), pool_scale[l][None, :],
            w_out[l].astype(BF16), norm2_g[l][None, :], wr_hi, wr_lo, br, seq)

        counts = cnt[0, :N_EXPERTS].astype(jnp.int32)
        padded = (counts + ROW_BLOCK - 1) // ROW_BLOCK * ROW_BLOCK
        pend = jnp.cumsum(padded).astype(jnp.int32)
        pstart = pend - padded
        n_used = (pend[-1:] // ROW_BLOCK).astype(jnp.int32)
        blk_row0 = jnp.arange(n_rows // ROW_BLOCK, dtype=jnp.int32) * ROW_BLOCK
        blk_e = jnp.minimum(
            jnp.sum((pend[None, :] <= blk_row0[:, None]).astype(jnp.int32), axis=1),
            N_EXPERTS - 1)
        top_e = route[:, 0:TOP_K].astype(jnp.int32)
        slot = route[:, 2 * TOP_K:3 * TOP_K].astype(jnp.int32)
        expert_ids = jnp.arange(N_EXPERTS, dtype=jnp.int32)
        dest = slot + jnp.sum(jnp.where(top_e[..., None] == expert_ids, pstart, 0), axis=-1)
        dest = dest.reshape(n * TOP_K)

        xs = _dispatch(counts, pstart, pend, dest, h2, n_rows)
        ys = _experts(blk_e, n_used, xs, w_gate[l], b_gate[l], w_up[l], b_up[l],
                      w_down[l], b_down[l])
        xt = _combine(dest, ys, x1, route, norm_f_g[None, :], l == depth - 1)
    return xt.reshape(batch, seq, D_MODEL)
```

```python
import functools
import math

import numpy as np
import jax
import jax.numpy as jnp
from jax import lax
from jax.experimental import pallas as pl
from jax.experimental.pallas import tpu as pltpu

F32 = jnp.float32
BF16 = jnp.bfloat16

D_MODEL = 1024
HGRN_WIDTH = 512
HEAD_DIM = 128
N_HEADS = HGRN_WIDTH // HEAD_DIM
POOL_WIDTH = 512
POOL_WINDOWS = (2, 4, 8, 16)
POOL_GW = POOL_WIDTH // len(POOL_WINDOWS)
IN_COLS = 4 * HGRN_WIDTH + POOL_WIDTH
N_EXPERTS = 32
TOP_K = 4
SWIGLU_LIMIT = 7.0
SWIGLU_ALPHA = 1.702
NORM_EPS = 1e-6

LANES = 128
VMEM_LIMIT_BYTES = 56 * 1024 * 1024

MIX_TOKENS = 512
CHUNK = 64
N_LEVELS = int(math.log2(CHUNK))
HALO = max(POOL_WINDOWS)
ROW_BLOCK = 512
DISPATCH_TOKENS = 256
COMBINE_TOKENS = 256
ROWS_PER_ITER = 2
NEG_BIG = -1e30


def _dot(a, b):
    return jnp.dot(a, b, preferred_element_type=F32)


def _dot_nt(a, b):
    return lax.dot_general(a, b, (((1,), (1,)), ((), ())), preferred_element_type=F32)


def _dot_tn(a, b):
    return lax.dot_general(a, b, (((0,), (0,)), ((), ())), preferred_element_type=F32)


def _sigmoid(x):
    return 1.0 / (1.0 + jnp.exp(-x))


def _rms(x, g):
    return x * lax.rsqrt(jnp.mean(x * x, axis=-1, keepdims=True) + NORM_EPS) * g


def _split_bf16(x):
    hi = x.astype(BF16)
    lo = (x - hi.astype(F32)).astype(BF16)
    return hi, lo


def _decay_tables(ch):
    t = np.arange(ch)[:, None]
    u = np.arange(ch)[None, :]
    mats = [u <= t, u > t]
    masks = []
    m = 1
    while m < ch:
        right = (t // m) % 2 == 1
        mid = (t // (2 * m)) * (2 * m) + m
        mats.append(np.where(right, (u >= mid) & (u <= t), (u > t) & (u <= mid - 1)))
        masks.append(right & ((u // m) % 2 == 0) & ((u // (2 * m)) == (t // (2 * m))))
        m *= 2
    return (np.concatenate(mats, 0).astype(np.float32),
            np.stack(masks).astype(np.float32))


def _hgrn_chunk(qp, fp, v, gp, lb, hg, tabs, masks_ref, st_t):
    q = qp * _sigmoid(qp)
    f = lb + (1.0 - lb) * _sigmoid(fp)
    lf = jnp.log(f)
    k = 1.0 - f
    lf_hi, lf_lo = _split_bf16(lf)
    x = jnp.exp(_dot(tabs, lf_hi) + _dot(tabs, lf_lo))
    eb = x[0:CHUNK]
    esuf = x[CHUNK:2 * CHUNK]
    vb = v.astype(BF16)
    o = _dot_nt((q * eb).astype(BF16), st_t.astype(BF16))
    o = o + jnp.sum(q * k, axis=-1, keepdims=True) * v
    s = jnp.zeros((CHUNK, CHUNK), F32)
    for lvl in range(N_LEVELS):
        xl = x[(2 + lvl) * CHUNK:(3 + lvl) * CHUNK]
        s = s + masks_ref[lvl] * _dot_nt((q * xl).astype(BF16), (k * xl).astype(BF16))
    o = o + _dot(s.astype(BF16), vb)
    new_st = eb[CHUNK - 1:CHUNK, :] * st_t + _dot_tn(vb, (k * esuf).astype(BF16))
    o = o * lax.rsqrt(jnp.mean(o * o, axis=-1, keepdims=True) + NORM_EPS) * hg
    out = o * (gp * _sigmoid(gp))
    return out, new_st


def _mixer_kernel(x_ref, g1_ref, win_ref, lb_ref, hg_ref, pw_ref, ps_ref, wout_ref, g2_ref,
                  wrh_ref, wrl_ref, br_ref, tabs_ref, masks_ref, tri_ref,
                  x1_ref, h2_ref, route_ref, cnt_ref,
                  proj_scr, mix_scr, st_scr, ubuf_scr, cnt_scr):
    b_i = pl.program_id(0)
    s_i = pl.program_id(1)

    @pl.when(s_i == 0)
    def _():
        st_scr[...] = jnp.zeros_like(st_scr)
        ubuf_scr[0:HALO, :] = jnp.zeros((HALO, POOL_WIDTH), F32)

    @pl.when((b_i == 0) & (s_i == 0))
    def _():
        cnt_scr[...] = jnp.zeros_like(cnt_scr)

    x = x_ref[...]
    h = _rms(x, g1_ref[...]).astype(BF16)
    proj_scr[...] = _dot(h, win_ref[...])

    tabs = tabs_ref[...]

    def chunk_body(c, carry):
        rows = pl.ds(pl.multiple_of(c * CHUNK, CHUNK), CHUNK)
        for hd in range(N_HEADS):
            lo = hd * HEAD_DIM
            cs = slice(lo, lo + HEAD_DIM)
            out, new_st = _hgrn_chunk(
                proj_scr[rows, lo:lo + HEAD_DIM],
                proj_scr[rows, HGRN_WIDTH + lo:HGRN_WIDTH + lo + HEAD_DIM],
                proj_scr[rows, 2 * HGRN_WIDTH + lo:2 * HGRN_WIDTH + lo + HEAD_DIM],
                proj_scr[rows, 3 * HGRN_WIDTH + lo:3 * HGRN_WIDTH + lo + HEAD_DIM],
                lb_ref[:, cs], hg_ref[...], tabs, masks_ref, st_scr[hd])
            st_scr[hd] = new_st
            mix_scr[rows, cs] = out.astype(BF16)
        return carry

    lax.fori_loop(0, MIX_TOKENS // CHUNK, chunk_body, 0)

    ubuf_scr[HALO:HALO + MIX_TOKENS, :] = proj_scr[:, 4 * HGRN_WIDTH:IN_COLS]
    pos = (s_i * MIX_TOKENS + 1
           + lax.broadcasted_iota(jnp.int32, (MIX_TOKENS, 1), 0)).astype(F32)
    for g, win in enumerate(POOL_WINDOWS):
        lo = g * POOL_GW
        u = ubuf_scr[HALO:HALO + MIX_TOKENS, lo:lo + POOL_GW]
        acc = u
        for j in range(1, win):
            acc = acc + ubuf_scr[HALO - j:HALO - j + MIX_TOKENS, lo:lo + POOL_GW]
        d = acc / jnp.minimum(pos, float(win)) - u
        y = _dot(d.astype(BF16), pw_ref[g]) * ps_ref[:, lo:lo + POOL_GW]
        mix_scr[:, HGRN_WIDTH + lo:HGRN_WIDTH + lo + POOL_GW] = y.astype(BF16)
    ubuf_scr[0:HALO, :] = ubuf_scr[MIX_TOKENS:MIX_TOKENS + HALO, :]

    x1 = x + _dot(mix_scr[...], wout_ref[...])
    x1_ref[...] = x1
    h2 = _rms(x1, g2_ref[...])
    h2_ref[...] = h2

    h_hi, h_lo = _split_bf16(h2)
    wrh = wrh_ref[...]
    logits = _dot(h_hi, wrh) + _dot(h_lo, wrh) + _dot(h_hi, wrl_ref[...]) + br_ref[...]
    lane = lax.broadcasted_iota(jnp.int32, (MIX_TOKENS, LANES), 1)
    work = logits
    tops, idxs, sels = [], [], []
    for _ in range(TOP_K):
        m = jnp.max(work, axis=-1, keepdims=True)
        idx = jnp.min(jnp.where(work == m, lane, LANES), axis=-1, keepdims=True)
        sel = lane == idx
        work = jnp.where(sel, 2.0 * NEG_BIG, work)
        tops.append(m)
        idxs.append(idx)
        sels.append(sel)
    exps = [jnp.exp(m - tops[0]) for m in tops]
    denom = exps[0] + exps[1] + exps[2] + exps[3]
    gates = [e / denom for e in exps]

    sel_any = jnp.zeros((MIX_TOKENS, LANES), F32)
    for sel in sels:
        sel_any = jnp.where(sel, 1.0, sel_any)
    cnt = cnt_scr[...]
    before = _dot(tri_ref[...], sel_any.astype(BF16)) + cnt
    cnt_new = cnt + jnp.sum(sel_any, axis=0, keepdims=True)
    cnt_scr[...] = cnt_new
    cnt_ref[...] = jnp.broadcast_to(cnt_new, cnt_ref.shape)

    route = jnp.zeros((MIX_TOKENS, LANES), F32)
    for k in range(TOP_K):
        slot = jnp.sum(jnp.where(sels[k], before, 0.0), axis=-1, keepdims=True)
        route = jnp.where(lane == k, idxs[k].astype(F32), route)
        route = jnp.where(lane == TOP_K + k, gates[k], route)
        route = jnp.where(lane == 2 * TOP_K + k, slot, route)
    route_ref[...] = route


def _mixer(x2d, g1, w_in, lb, hg, pool_w, pool_scale, w_out, g2, wr_hi, wr_lo, br, seq):
    n = x2d.shape[0]
    steps_per_seq = seq // MIX_TOKENS
    tabs_np, masks_np = _decay_tables(CHUNK)
    tabs = jnp.asarray(tabs_np, BF16)
    masks = jnp.asarray(masks_np, F32)
    tri = jnp.asarray(np.tril(np.ones((MIX_TOKENS, MIX_TOKENS), np.float32), -1), BF16)

    def tok(b, s):
        return (b * steps_per_seq + s, 0)

    def const2(b, s):
        return (0, 0)

    def const3(b, s):
        return (0, 0, 0)

    in_specs = [
        pl.BlockSpec((MIX_TOKENS, D_MODEL), tok),
        pl.BlockSpec((1, D_MODEL), const2),
        pl.BlockSpec((D_MODEL, IN_COLS), const2),
        pl.BlockSpec((1, HGRN_WIDTH), const2),
        pl.BlockSpec((1, HEAD_DIM), const2),
        pl.BlockSpec((len(POOL_WINDOWS), POOL_GW, POOL_GW), const3),
        pl.BlockSpec((1, POOL_WIDTH), const2),
        pl.BlockSpec((D_MODEL, D_MODEL), const2),
        pl.BlockSpec((1, D_MODEL), const2),
        pl.BlockSpec((D_MODEL, LANES), const2),
        pl.BlockSpec((D_MODEL, LANES), const2),
        pl.BlockSpec((1, LANES), const2),
        pl.BlockSpec(tabs.shape, const2),
        pl.BlockSpec(masks.shape, const3),
        pl.BlockSpec(tri.shape, const2),
    ]
    out_specs = [
        pl.BlockSpec((MIX_TOKENS, D_MODEL), tok),
        pl.BlockSpec((MIX_TOKENS, D_MODEL), tok),
        pl.BlockSpec((MIX_TOKENS, LANES), tok),
        pl.BlockSpec((8, LANES), const2),
    ]
    out_shape = [
        jax.ShapeDtypeStruct((n, D_MODEL), F32),
        jax.ShapeDtypeStruct((n, D_MODEL), F32),
        jax.ShapeDtypeStruct((n, LANES), F32),
        jax.ShapeDtypeStruct((8, LANES), F32),
    ]
    scratch = [
        pltpu.VMEM((MIX_TOKENS, IN_COLS), F32),
        pltpu.VMEM((MIX_TOKENS, D_MODEL), BF16),
        pltpu.VMEM((N_HEADS, HEAD_DIM, HEAD_DIM), F32),
        pltpu.VMEM((HALO + MIX_TOKENS, POOL_WIDTH), F32),
        pltpu.VMEM((1, LANES), F32),
    ]
    return pl.pallas_call(
        _mixer_kernel,
        grid=(n // seq, steps_per_seq),
        in_specs=in_specs,
        out_specs=out_specs,
        out_shape=out_shape,
        scratch_shapes=scratch,
        compiler_params=pltpu.CompilerParams(
            dimension_semantics=("arbitrary", "arbitrary"),
            vmem_limit_bytes=VMEM_LIMIT_BYTES),
        name="mixer",
    )(x2d, g1, w_in, lb, hg, pool_w, pool_scale, w_out, g2, wr_hi, wr_lo, br, tabs, masks, tri)


def _row_copy(src, src_row, dst, dst_row, sem):
    return pltpu.make_async_copy(src.at[pl.ds(src_row, 1), :], dst.at[pl.ds(dst_row, 1), :], sem)


def _dispatch_kernel(pstart_ref, pend_ref, dest_ref, h2_ref, xs_ref, zrow, sem, zsem):
    i = pl.program_id(0)

    @pl.when(i == 0)
    def _():
        zrow[...] = jnp.zeros_like(zrow)
        n_blocks = xs_ref.shape[0] // ROW_BLOCK

        def block_copy(b):
            return pltpu.make_async_copy(
                zrow, xs_ref.at[pl.ds(pl.multiple_of(b * ROW_BLOCK, ROW_BLOCK), ROW_BLOCK), :],
                zsem)

        def block_start(b, carry):
            block_copy(b).start()
            return carry

        def block_wait(b, carry):
            block_copy(b).wait()
            return carry

        first_unused = pend_ref[N_EXPERTS - 1] // ROW_BLOCK
        lax.fori_loop(first_unused, n_blocks, block_start, 0)
        lax.fori_loop(first_unused, n_blocks, block_wait, 0)

        def tail_copy(e):
            return block_copy(pend_ref[e] // ROW_BLOCK - 1)

        def tail_start(e, carry):
            @pl.when(pend_ref[e] > pstart_ref[e])
            def _():
                tail_copy(e).start()
            return carry

        def tail_wait(e, carry):
            @pl.when(pend_ref[e] > pstart_ref[e])
            def _():
                tail_copy(e).wait()
            return carry

        lax.fori_loop(0, N_EXPERTS, tail_start, 0)
        lax.fori_loop(0, N_EXPERTS, tail_wait, 0)

    def row_body(i2, carry):
        for j in range(ROWS_PER_ITER):
            r = i2 * ROWS_PER_ITER + j
            for k in range(TOP_K):
                _row_copy(h2_ref, r, xs_ref, dest_ref[r * TOP_K + k], sem).start(priority=k % 2)
        return carry

    lax.fori_loop(0, DISPATCH_TOKENS // ROWS_PER_ITER, row_body, 0)
    for k in range(TOP_K):
        pltpu.make_async_copy(h2_ref, xs_ref.at[pl.ds(0, DISPATCH_TOKENS), :], sem).wait()


def _dispatch(pstart, pend, dest, h2, n_rows):
    n = h2.shape[0]
    grid_spec = pltpu.PrefetchScalarGridSpec(
        num_scalar_prefetch=2,
        grid=(n // DISPATCH_TOKENS,),
        in_specs=[
            pl.BlockSpec((DISPATCH_TOKENS * TOP_K,), lambda i, *_: (i,),
                         memory_space=pltpu.SMEM),
            pl.BlockSpec((DISPATCH_TOKENS, D_MODEL), lambda i, *_: (i, 0)),
        ],
        out_specs=pl.BlockSpec(memory_space=pl.ANY),
        scratch_shapes=[
            pltpu.VMEM((ROW_BLOCK, D_MODEL), F32),
            pltpu.SemaphoreType.DMA,
            pltpu.SemaphoreType.DMA,
        ],
    )
    return pl.pallas_call(
        _dispatch_kernel,
        grid_spec=grid_spec,
        out_shape=jax.ShapeDtypeStruct((n_rows, D_MODEL), F32),
        compiler_params=pltpu.CompilerParams(
            dimension_semantics=("arbitrary",),
            vmem_limit_bytes=VMEM_LIMIT_BYTES),
        name="dispatch",
    )(pstart, pend, dest, h2)


def _expert_kernel(blk_e_ref, nused_ref, xs_ref, wg_ref, bg_ref, wu_ref, bu_ref, wd_ref, bd_ref,
                   ys_ref, wg_s, wu_s, wd_s):
    i = pl.program_id(0)
    active = i < nused_ref[0]
    changed = (i == 0) | (blk_e_ref[i] != blk_e_ref[jnp.maximum(i - 1, 0)])

    @pl.when(active & changed)
    def _():
        wg_s[...] = wg_ref[0].astype(BF16)
        wu_s[...] = wu_ref[0].astype(BF16)
        wd_s[...] = wd_ref[0].astype(BF16)

    @pl.when(active)
    def _():
        xb = xs_ref[...].astype(BF16)
        gt = jnp.minimum(_dot(xb, wg_s[...]) + bg_ref[0], SWIGLU_LIMIT)
        up = jnp.clip(_dot(xb, wu_s[...]) + bu_ref[0], -SWIGLU_LIMIT, SWIGLU_LIMIT)
        act = (up + 1.0) * (gt * _sigmoid(SWIGLU_ALPHA * gt))
        ys_ref[...] = _dot(act.astype(BF16), wd_s[...]) + bd_ref[0]

    @pl.when(jnp.logical_not(active))
    def _():
        ys_ref[...] = jnp.zeros_like(ys_ref)


def _experts(blk_e, n_used, xs, w_gate, b_gate, w_up, b_up, w_down, b_down):
    n_rows = xs.shape[0]
    n_blocks = n_rows // ROW_BLOCK

    def blk(i, be, nu):
        return jnp.minimum(i, nu[0] - 1)

    def row_map(i, be, nu):
        return (blk(i, be, nu), 0)

    def w_map(i, be, nu):
        return (be[blk(i, be, nu)], 0, 0)

    w_spec = pl.BlockSpec((1, D_MODEL, D_MODEL), w_map)
    b_spec = pl.BlockSpec((1, 1, D_MODEL), w_map)
    grid_spec = pltpu.PrefetchScalarGridSpec(
        num_scalar_prefetch=2,
        grid=(n_blocks,),
        in_specs=[pl.BlockSpec((ROW_BLOCK, D_MODEL), row_map),
                  w_spec, b_spec, w_spec, b_spec, w_spec, b_spec],
        out_specs=pl.BlockSpec((ROW_BLOCK, D_MODEL), lambda i, be, nu: (i, 0)),
        scratch_shapes=[pltpu.VMEM((D_MODEL, D_MODEL), BF16)] * 3,
    )
    return pl.pallas_call(
        _expert_kernel,
        grid_spec=grid_spec,
        out_shape=jax.ShapeDtypeStruct((n_rows, D_MODEL), F32),
        compiler_params=pltpu.CompilerParams(
            dimension_semantics=("arbitrary",),
            vmem_limit_bytes=VMEM_LIMIT_BYTES),
        name="experts",
    )(blk_e, n_used, xs, w_gate, b_gate[:, None, :], w_up, b_up[:, None, :],
      w_down, b_down[:, None, :])


def _combine_kernel(final, dest_ref, ys_ref, x1_ref, route_ref, gf_ref, out_ref, buf, sem):
    def row_body(i2, carry):
        for j in range(ROWS_PER_ITER):
            r = i2 * ROWS_PER_ITER + j
            for k in range(TOP_K):
                pltpu.make_async_copy(ys_ref.at[pl.ds(dest_ref[r * TOP_K + k], 1), :],
                                      buf.at[k, pl.ds(r, 1), :], sem).start(priority=k % 2)
        return carry

    lax.fori_loop(0, COMBINE_TOKENS // ROWS_PER_ITER, row_body, 0)
    for k in range(TOP_K):
        pltpu.make_async_copy(ys_ref.at[pl.ds(0, COMBINE_TOKENS), :], buf.at[k], sem).wait()

    route = route_ref[...]
    acc = x1_ref[...]
    for k in range(TOP_K):
        acc = acc + route[:, TOP_K + k:TOP_K + k + 1] * buf[k]
    out_ref[...] = _rms(acc, gf_ref[...]) if final else acc


def _combine(dest, ys, x1, route, gf, final):
    n = x1.shape[0]
    return pl.pallas_call(
        functools.partial(_combine_kernel, final),
        grid=(n // COMBINE_TOKENS,),
        in_specs=[
            pl.BlockSpec((COMBINE_TOKENS * TOP_K,), lambda i: (i,), memory_space=pltpu.SMEM),
            pl.BlockSpec(memory_space=pl.ANY),
            pl.BlockSpec((COMBINE_TOKENS, D_MODEL), lambda i: (i, 0)),
            pl.BlockSpec((COMBINE_TOKENS, LANES), lambda i: (i, 0)),
            pl.BlockSpec((1, D_MODEL), lambda i: (0, 0)),
        ],
        out_specs=pl.BlockSpec((COMBINE_TOKENS, D_MODEL), lambda i: (i, 0)),
        out_shape=jax.ShapeDtypeStruct((n, D_MODEL), F32),
        scratch_shapes=[
            pltpu.VMEM((TOP_K, COMBINE_TOKENS, D_MODEL), F32),
            pltpu.SemaphoreType.DMA,
        ],
        compiler_params=pltpu.CompilerParams(
            dimension_semantics=("arbitrary",),
            vmem_limit_bytes=VMEM_LIMIT_BYTES),
        name="combine",
    )(dest, ys, x1, route, gf)


def kernel(x, norm1_g, w_in, hgrn_lb, hgrn_norm_g, pool_w, pool_scale, w_out, norm2_g,
           router_w, router_b, w_gate, b_gate, w_up, b_up, w_down, b_down, norm_f_g):
    depth = w_in.shape[0]
    batch, seq, _ = x.shape
    n = batch * seq
    n_rows = n * TOP_K + N_EXPERTS * ROW_BLOCK
    lb_all = jnp.cumsum(jax.nn.softmax(hgrn_lb.astype(F32), axis=0), axis=0)

    xt = x.reshape(n, D_MODEL)
    for l in range(depth):
        wr = jnp.pad(router_w[l], ((0, 0), (0, LANES - N_EXPERTS)))
        wr_hi = wr.astype(BF16)
        wr_lo = (wr - wr_hi.astype(F32)).astype(BF16)
        br = jnp.pad(router_b[l], (0, LANES - N_EXPERTS), constant_values=NEG_BIG)[None, :]
        x1, h2, route, cnt = _mixer(
            xt, norm1_g[l][None, :], w_in[l].astype(BF16), lb_all[l][None, :],
            hgrn_norm_g[l][None, :], pool_w[l].astype(BF16), pool_scale[l][None, :],
            w_out[l].astype(BF16), norm2_g[l][None, :], wr_hi, wr_lo, br, seq)

        counts = cnt[0, :N_EXPERTS].astype(jnp.int32)
        padded = (counts + ROW_BLOCK - 1) // ROW_BLOCK * ROW_BLOCK
        pend = jnp.cumsum(padded).astype(jnp.int32)
        pstart = pend - padded
        n_used = (pend[-1:] // ROW_BLOCK).astype(jnp.int32)
        blk_row0 = jnp.arange(n_rows // ROW_BLOCK, dtype=jnp.int32) * ROW_BLOCK
        blk_e = jnp.minimum(
            jnp.sum((pend[None, :] <= blk_row0[:, None]).astype(jnp.int32), axis=1),
            N_EXPERTS - 1)
        top_e = route[:, 0:TOP_K].astype(jnp.int32)
        slot = route[:, 2 * TOP_K:3 * TOP_K].astype(jnp.int32)
        expert_ids = jnp.arange(N_EXPERTS, dtype=jnp.int32)
        dest = slot + jnp.sum(jnp.where(top_e[..., None] == expert_ids, pstart, 0), axis=-1)
        dest = dest.reshape(n * TOP_K)

        xs = _dispatch(pstart, pend, dest, h2, n_rows)
        ys = _experts(blk_e, n_used, xs, w_gate[l], b_gate[l], w_up[l], b_up[l],
                      w_down[l], b_down[l])
        xt = _combine(dest, ys, x1, route, norm_f_g[None, :], l == depth - 1)
    return xt.reshape(batch, seq, D_MODEL)
```

```python
import functools
import math

import numpy as np
import jax
import jax.numpy as jnp
from jax import lax
from jax.experimental import pallas as pl
from jax.experimental.pallas import tpu as pltpu

F32 = jnp.float32
BF16 = jnp.bfloat16

D_MODEL = 1024
HGRN_WIDTH = 512
HEAD_DIM = 128
N_HEADS = HGRN_WIDTH // HEAD_DIM
POOL_WIDTH = 512
POOL_WINDOWS = (2, 4, 8, 16)
POOL_GW = POOL_WIDTH // len(POOL_WINDOWS)
IN_COLS = 4 * HGRN_WIDTH + POOL_WIDTH
N_EXPERTS = 32
TOP_K = 4
SWIGLU_LIMIT = 7.0
SWIGLU_ALPHA = 1.702
NORM_EPS = 1e-6

LANES = 128
ROW_TILE = D_MODEL // LANES
VMEM_LIMIT_BYTES = 56 * 1024 * 1024

MIX_TOKENS = 512
CHUNK = 64
N_LEVELS = int(math.log2(CHUNK))
HALO = max(POOL_WINDOWS)
ROW_BLOCK = 512
DISPATCH_TOKENS = 1024
COMBINE_TOKENS = 256
ROWS_PER_ITER = 8
NEG_BIG = -1e30


def _dot(a, b):
    return jnp.dot(a, b, preferred_element_type=F32)


def _dot_nt(a, b):
    return lax.dot_general(a, b, (((1,), (1,)), ((), ())), preferred_element_type=F32)


def _dot_tn(a, b):
    return lax.dot_general(a, b, (((0,), (0,)), ((), ())), preferred_element_type=F32)


def _sigmoid(x):
    return 1.0 / (1.0 + jnp.exp(-x))


def _rms(x, g):
    return x * lax.rsqrt(jnp.mean(x * x, axis=-1, keepdims=True) + NORM_EPS) * g


def _split_bf16(x):
    hi = x.astype(BF16)
    lo = (x - hi.astype(F32)).astype(BF16)
    return hi, lo


def _to_row_tiles(ref, value):
    t = value.shape[0]
    for j in range(ROW_TILE):
        ref[pl.ds(j, t, stride=ROW_TILE), :] = value[:, j * LANES:(j + 1) * LANES]


def _from_row_tiles(ref, t):
    return jnp.concatenate(
        [ref[pl.ds(j, t, stride=ROW_TILE), :] for j in range(ROW_TILE)], axis=1)


def _tile_rows(row):
    return pl.ds(pl.multiple_of(row * ROW_TILE, ROW_TILE), ROW_TILE)


def _decay_tables(ch):
    t = np.arange(ch)[:, None]
    u = np.arange(ch)[None, :]
    mats = [u <= t, u > t]
    masks = []
    m = 1
    while m < ch:
        right = (t // m) % 2 == 1
        mid = (t // (2 * m)) * (2 * m) + m
        mats.append(np.where(right, (u >= mid) & (u <= t), (u > t) & (u <= mid - 1)))
        masks.append(right & ((u // m) % 2 == 0) & ((u // (2 * m)) == (t // (2 * m))))
        m *= 2
    return (np.concatenate(mats, 0).astype(np.float32),
            np.stack(masks).astype(np.float32))


def _hgrn_chunk(qp, fp, v, gp, lb, hg, tabs, masks_ref, st_t):
    q = qp * _sigmoid(qp)
    f = lb + (1.0 - lb) * _sigmoid(fp)
    lf = jnp.log(f)
    k = 1.0 - f
    lf_hi, lf_lo = _split_bf16(lf)
    x = jnp.exp(_dot(tabs, lf_hi) + _dot(tabs, lf_lo))
    eb = x[0:CHUNK]
    esuf = x[CHUNK:2 * CHUNK]
    vb = v.astype(BF16)
    o = _dot_nt((q * eb).astype(BF16), st_t.astype(BF16))
    o = o + jnp.sum(q * k, axis=-1, keepdims=True) * v
    s = jnp.zeros((CHUNK, CHUNK), F32)
    for lvl in range(N_LEVELS):
        xl = x[(2 + lvl) * CHUNK:(3 + lvl) * CHUNK]
        s = s + masks_ref[lvl] * _dot_nt((q * xl).astype(BF16), (k * xl).astype(BF16))
    o = o + _dot(s.astype(BF16), vb)
    new_st = eb[CHUNK - 1:CHUNK, :] * st_t + _dot_tn(vb, (k * esuf).astype(BF16))
    o = o * lax.rsqrt(jnp.mean(o * o, axis=-1, keepdims=True) + NORM_EPS) * hg
    out = o * (gp * _sigmoid(gp))
    return out, new_st


def _mixer_kernel(x_ref, g1_ref, win_ref, lb_ref, hg_ref, pw_ref, ps_ref, wout_ref, g2_ref,
                  wrh_ref, wrl_ref, br_ref, tabs_ref, masks_ref, tri_ref,
                  x1_ref, h2_ref, route_ref, cnt_ref,
                  proj_scr, mix_scr, st_scr, ubuf_scr, cnt_scr):
    b_i = pl.program_id(0)
    s_i = pl.program_id(1)

    @pl.when(s_i == 0)
    def _():
        st_scr[...] = jnp.zeros_like(st_scr)
        ubuf_scr[0:HALO, :] = jnp.zeros((HALO, POOL_WIDTH), F32)

    @pl.when((b_i == 0) & (s_i == 0))
    def _():
        cnt_scr[...] = jnp.zeros_like(cnt_scr)

    x = x_ref[...]
    h = _rms(x, g1_ref[...]).astype(BF16)
    proj_scr[...] = _dot(h, win_ref[...])

    tabs = tabs_ref[...]

    def chunk_body(c, carry):
        rows = pl.ds(pl.multiple_of(c * CHUNK, CHUNK), CHUNK)
        for hd in range(N_HEADS):
            lo = hd * HEAD_DIM
            cs = slice(lo, lo + HEAD_DIM)
            out, new_st = _hgrn_chunk(
                proj_scr[rows, lo:lo + HEAD_DIM],
                proj_scr[rows, HGRN_WIDTH + lo:HGRN_WIDTH + lo + HEAD_DIM],
                proj_scr[rows, 2 * HGRN_WIDTH + lo:2 * HGRN_WIDTH + lo + HEAD_DIM],
                proj_scr[rows, 3 * HGRN_WIDTH + lo:3 * HGRN_WIDTH + lo + HEAD_DIM],
                lb_ref[:, cs], hg_ref[...], tabs, masks_ref, st_scr[hd])
            st_scr[hd] = new_st
            mix_scr[rows, cs] = out.astype(BF16)
        return carry

    lax.fori_loop(0, MIX_TOKENS // CHUNK, chunk_body, 0)

    ubuf_scr[HALO:HALO + MIX_TOKENS, :] = proj_scr[:, 4 * HGRN_WIDTH:IN_COLS]
    pos = (s_i * MIX_TOKENS + 1
           + lax.broadcasted_iota(jnp.int32, (MIX_TOKENS, 1), 0)).astype(F32)
    for g, win in enumerate(POOL_WINDOWS):
        lo = g * POOL_GW
        u = ubuf_scr[HALO:HALO + MIX_TOKENS, lo:lo + POOL_GW]
        acc = u
        for j in range(1, win):
            acc = acc + ubuf_scr[HALO - j:HALO - j + MIX_TOKENS, lo:lo + POOL_GW]
        d = acc / jnp.minimum(pos, float(win)) - u
        y = _dot(d.astype(BF16), pw_ref[g]) * ps_ref[:, lo:lo + POOL_GW]
        mix_scr[:, HGRN_WIDTH + lo:HGRN_WIDTH + lo + POOL_GW] = y.astype(BF16)
    ubuf_scr[0:HALO, :] = ubuf_scr[MIX_TOKENS:MIX_TOKENS + HALO, :]

    x1 = x + _dot(mix_scr[...], wout_ref[...])
    x1_ref[...] = x1
    h2 = _rms(x1, g2_ref[...])
    _to_row_tiles(h2_ref, h2)

    h_hi, h_lo = _split_bf16(h2)
    wrh = wrh_ref[...]
    logits = _dot(h_hi, wrh) + _dot(h_lo, wrh) + _dot(h_hi, wrl_ref[...]) + br_ref[...]
    lane = lax.broadcasted_iota(jnp.int32, (MIX_TOKENS, LANES), 1)
    work = logits
    tops, idxs, sels = [], [], []
    for _ in range(TOP_K):
        m = jnp.max(work, axis=-1, keepdims=True)
        idx = jnp.min(jnp.where(work == m, lane, LANES), axis=-1, keepdims=True)
        sel = lane == idx
        work = jnp.where(sel, 2.0 * NEG_BIG, work)
        tops.append(m)
        idxs.append(idx)
        sels.append(sel)
    exps = [jnp.exp(m - tops[0]) for m in tops]
    denom = exps[0] + exps[1] + exps[2] + exps[3]
    gates = [e / denom for e in exps]

    sel_any = jnp.zeros((MIX_TOKENS, LANES), F32)
    for sel in sels:
        sel_any = jnp.where(sel, 1.0, sel_any)
    cnt = cnt_scr[...]
    before = _dot(tri_ref[...], sel_any.astype(BF16)) + cnt
    cnt_new = cnt + jnp.sum(sel_any, axis=0, keepdims=True)
    cnt_scr[...] = cnt_new
    cnt_ref[...] = jnp.broadcast_to(cnt_new, cnt_ref.shape)

    route = jnp.zeros((MIX_TOKENS, LANES), F32)
    for k in range(TOP_K):
        slot = jnp.sum(jnp.where(sels[k], before, 0.0), axis=-1, keepdims=True)
        route = jnp.where(lane == k, idxs[k].astype(F32), route)
        route = jnp.where(lane == TOP_K + k, gates[k], route)
        route = jnp.where(lane == 2 * TOP_K + k, slot, route)
    route_ref[...] = route


def _mixer(x2d, g1, w_in, lb, hg, pool_w, pool_scale, w_out, g2, wr_hi, wr_lo, br, seq):
    n = x2d.shape[0]
    steps_per_seq = seq // MIX_TOKENS
    tabs_np, masks_np = _decay_tables(CHUNK)
    tabs = jnp.asarray(tabs_np, BF16)
    masks = jnp.asarray(masks_np, F32)
    tri = jnp.asarray(np.tril(np.ones((MIX_TOKENS, MIX_TOKENS), np.float32), -1), BF16)

    def tok(b, s):
        return (b * steps_per_seq + s, 0)

    def const2(b, s):
        return (0, 0)

    def const3(b, s):
        return (0, 0, 0)

    in_specs = [
        pl.BlockSpec((MIX_TOKENS, D_MODEL), tok),
        pl.BlockSpec((1, D_MODEL), const2),
        pl.BlockSpec((D_MODEL, IN_COLS), const2),
        pl.BlockSpec((1, HGRN_WIDTH), const2),
        pl.BlockSpec((1, HEAD_DIM), const2),
        pl.BlockSpec((len(POOL_WINDOWS), POOL_GW, POOL_GW), const3),
        pl.BlockSpec((1, POOL_WIDTH), const2),
        pl.BlockSpec((D_MODEL, D_MODEL), const2),
        pl.BlockSpec((1, D_MODEL), const2),
        pl.BlockSpec((D_MODEL, LANES), const2),
        pl.BlockSpec((D_MODEL, LANES), const2),
        pl.BlockSpec((1, LANES), const2),
        pl.BlockSpec(tabs.shape, const2),
        pl.BlockSpec(masks.shape, const3),
        pl.BlockSpec(tri.shape, const2),
    ]
    out_specs = [
        pl.BlockSpec((MIX_TOKENS, D_MODEL), tok),
        pl.BlockSpec((MIX_TOKENS * ROW_TILE, LANES), tok),
        pl.BlockSpec((MIX_TOKENS, LANES), tok),
        pl.BlockSpec((8, LANES), const2),
    ]
    out_shape = [
        jax.ShapeDtypeStruct((n, D_MODEL), F32),
        jax.ShapeDtypeStruct((n * ROW_TILE, LANES), F32),
        jax.ShapeDtypeStruct((n, LANES), F32),
        jax.ShapeDtypeStruct((8, LANES), F32),
    ]
    scratch = [
        pltpu.VMEM((MIX_TOKENS, IN_COLS), F32),
        pltpu.VMEM((MIX_TOKENS, D_MODEL), BF16),
        pltpu.VMEM((N_HEADS, HEAD_DIM, HEAD_DIM), F32),
        pltpu.VMEM((HALO + MIX_TOKENS, POOL_WIDTH), F32),
        pltpu.VMEM((1, LANES), F32),
    ]
    return pl.pallas_call(
        _mixer_kernel,
        grid=(n // seq, steps_per_seq),
        in_specs=in_specs,
        out_specs=out_specs,
        out_shape=out_shape,
        scratch_shapes=scratch,
        compiler_params=pltpu.CompilerParams(
            dimension_semantics=("arbitrary", "arbitrary"),
            vmem_limit_bytes=VMEM_LIMIT_BYTES),
        name="mixer",
    )(x2d, g1, w_in, lb, hg, pool_w, pool_scale, w_out, g2, wr_hi, wr_lo, br, tabs, masks, tri)


def _row_copy(src, src_row, dst, dst_row, sem):
    return pltpu.make_async_copy(src.at[_tile_rows(src_row), :], dst.at[_tile_rows(dst_row), :], sem)


def _dispatch_kernel(pstart_ref, pend_ref, dest_ref, h2_ref, xs_ref, zrow, sem, zsem):
    i = pl.program_id(0)

    @pl.when(i == 0)
    def _():
        zrow[...] = jnp.zeros_like(zrow)
        block_rows = ROW_BLOCK * ROW_TILE
        n_blocks = xs_ref.shape[0] // block_rows

        def block_copy(b):
            return pltpu.make_async_copy(
                zrow, xs_ref.at[pl.ds(pl.multiple_of(b * block_rows, block_rows), block_rows), :],
                zsem)

        def block_start(b, carry):
            block_copy(b).start()
            return carry

        def block_wait(b, carry):
            block_copy(b).wait()
            return carry

        first_unused = pend_ref[N_EXPERTS - 1] // ROW_BLOCK
        lax.fori_loop(first_unused, n_blocks, block_start, 0)
        lax.fori_loop(first_unused, n_blocks, block_wait, 0)

        def tail_copy(e):
            return block_copy(pend_ref[e] // ROW_BLOCK - 1)

        def tail_start(e, carry):
            @pl.when(pend_ref[e] > pstart_ref[e])
            def _():
                tail_copy(e).start()
            return carry

        def tail_wait(e, carry):
            @pl.when(pend_ref[e] > pstart_ref[e])
            def _():
                tail_copy(e).wait()
            return carry

        lax.fori_loop(0, N_EXPERTS, tail_start, 0)
        lax.fori_loop(0, N_EXPERTS, tail_wait, 0)

    def row_body(i2, carry):
        for j in range(ROWS_PER_ITER):
            r = i2 * ROWS_PER_ITER + j
            for k in range(TOP_K):
                _row_copy(h2_ref, r, xs_ref, dest_ref[r * TOP_K + k], sem).start(priority=k % 2)
        return carry

    lax.fori_loop(0, DISPATCH_TOKENS // ROWS_PER_ITER, row_body, 0)
    for k in range(TOP_K):
        pltpu.make_async_copy(h2_ref, xs_ref.at[pl.ds(0, DISPATCH_TOKENS * ROW_TILE), :],
                              sem).wait()


def _dispatch(pstart, pend, dest, h2, n_rows):
    n = h2.shape[0] // ROW_TILE
    grid_spec = pltpu.PrefetchScalarGridSpec(
        num_scalar_prefetch=2,
        grid=(n // DISPATCH_TOKENS,),
        in_specs=[
            pl.BlockSpec((DISPATCH_TOKENS * TOP_K,), lambda i, *_: (i,),
                         memory_space=pltpu.SMEM),
            pl.BlockSpec((DISPATCH_TOKENS * ROW_TILE, LANES), lambda i, *_: (i, 0)),
        ],
        out_specs=pl.BlockSpec(memory_space=pl.ANY),
        scratch_shapes=[
            pltpu.VMEM((ROW_BLOCK * ROW_TILE, LANES), F32),
            pltpu.SemaphoreType.DMA,
            pltpu.SemaphoreType.DMA,
        ],
    )
    return pl.pallas_call(
        _dispatch_kernel,
        grid_spec=grid_spec,
        out_shape=jax.ShapeDtypeStruct((n_rows * ROW_TILE, LANES), F32),
        compiler_params=pltpu.CompilerParams(
            dimension_semantics=("arbitrary",),
            vmem_limit_bytes=VMEM_LIMIT_BYTES),
        name="dispatch",
    )(pstart, pend, dest, h2)


def _expert_kernel(blk_e_ref, nused_ref, xs_ref, wg_ref, bg_ref, wu_ref, bu_ref, wd_ref, bd_ref,
                   ys_ref, wg_s, wu_s, wd_s):
    i = pl.program_id(0)
    active = i < nused_ref[0]
    changed = (i == 0) | (blk_e_ref[i] != blk_e_ref[jnp.maximum(i - 1, 0)])

    @pl.when(active & changed)
    def _():
        wg_s[...] = wg_ref[0].astype(BF16)
        wu_s[...] = wu_ref[0].astype(BF16)
        wd_s[...] = wd_ref[0].astype(BF16)

    @pl.when(active)
    def _():
        xb = _from_row_tiles(xs_ref, ROW_BLOCK).astype(BF16)
        gt = jnp.minimum(_dot(xb, wg_s[...]) + bg_ref[0], SWIGLU_LIMIT)
        up = jnp.clip(_dot(xb, wu_s[...]) + bu_ref[0], -SWIGLU_LIMIT, SWIGLU_LIMIT)
        act = (up + 1.0) * (gt * _sigmoid(SWIGLU_ALPHA * gt))
        _to_row_tiles(ys_ref, _dot(act.astype(BF16), wd_s[...]) + bd_ref[0])

    @pl.when(jnp.logical_not(active))
    def _():
        ys_ref[...] = jnp.zeros_like(ys_ref)


def _experts(blk_e, n_used, xs, w_gate, b_gate, w_up, b_up, w_down, b_down):
    n_rows = xs.shape[0] // ROW_TILE
    n_blocks = n_rows // ROW_BLOCK
    rows_spec_shape = (ROW_BLOCK * ROW_TILE, LANES)

    def blk(i, be, nu):
        return jnp.minimum(i, nu[0] - 1)

    def row_map(i, be, nu):
        return (blk(i, be, nu), 0)

    def w_map(i, be, nu):
        return (be[blk(i, be, nu)], 0, 0)

    w_spec = pl.BlockSpec((1, D_MODEL, D_MODEL), w_map)
    b_spec = pl.BlockSpec((1, 1, D_MODEL), w_map)
    grid_spec = pltpu.PrefetchScalarGridSpec(
        num_scalar_prefetch=2,
        grid=(n_blocks,),
        in_specs=[pl.BlockSpec(rows_spec_shape, row_map),
                  w_spec, b_spec, w_spec, b_spec, w_spec, b_spec],
        out_specs=pl.BlockSpec(rows_spec_shape, lambda i, be, nu: (i, 0)),
        scratch_shapes=[pltpu.VMEM((D_MODEL, D_MODEL), BF16)] * 3,
    )
    return pl.pallas_call(
        _expert_kernel,
        grid_spec=grid_spec,
        out_shape=jax.ShapeDtypeStruct((n_rows * ROW_TILE, LANES), F32),
        compiler_params=pltpu.CompilerParams(
            dimension_semantics=("arbitrary",),
            vmem_limit_bytes=VMEM_LIMIT_BYTES),
        name="experts",
    )(blk_e, n_used, xs, w_gate, b_gate[:, None, :], w_up, b_up[:, None, :],
      w_down, b_down[:, None, :])


def _combine_kernel(final, dest_ref, dest_next_ref, ys_ref, x1_ref, route_ref, gf_ref, out_ref,
                    buf, sems):
    i = pl.program_id(0)
    n_steps = pl.num_programs(0)
    slot = lax.rem(i, 2)

    def gather(idx_ref, s):
        def row_body(it, carry):
            for j in range(ROWS_PER_ITER):
                r = it * ROWS_PER_ITER + j
                for k in range(TOP_K):
                    pltpu.make_async_copy(ys_ref.at[_tile_rows(idx_ref[r * TOP_K + k]), :],
                                          buf.at[s, k, _tile_rows(r), :],
                                          sems.at[s]).start(priority=k % 2)
            return carry

        lax.fori_loop(0, COMBINE_TOKENS // ROWS_PER_ITER, row_body, 0)

    @pl.when(i == 0)
    def _():
        gather(dest_ref, 0)

    @pl.when(i + 1 < n_steps)
    def _():
        gather(dest_next_ref, 1 - slot)

    for k in range(TOP_K):
        pltpu.make_async_copy(ys_ref.at[pl.ds(0, COMBINE_TOKENS * ROW_TILE), :],
                              buf.at[slot, k], sems.at[slot]).wait()

    route = route_ref[...]
    acc = x1_ref[...]
    for k in range(TOP_K):
        acc = acc + route[:, TOP_K + k:TOP_K + k + 1] * _from_row_tiles(
            buf.at[slot, k], COMBINE_TOKENS)
    out_ref[...] = _rms(acc, gf_ref[...]) if final else acc


def _combine(dest, ys, x1, route, gf, final):
    n = x1.shape[0]
    n_steps = n // COMBINE_TOKENS
    idx_block = (COMBINE_TOKENS * TOP_K,)
    return pl.pallas_call(
        functools.partial(_combine_kernel, final),
        grid=(n_steps,),
        in_specs=[
            pl.BlockSpec(idx_block, lambda i: (i,), memory_space=pltpu.SMEM),
            pl.BlockSpec(idx_block, lambda i: (jnp.minimum(i + 1, n_steps - 1),),
                         memory_space=pltpu.SMEM),
            pl.BlockSpec(memory_space=pl.ANY),
            pl.BlockSpec((COMBINE_TOKENS, D_MODEL), lambda i: (i, 0)),
            pl.BlockSpec((COMBINE_TOKENS, LANES), lambda i: (i, 0)),
            pl.BlockSpec((1, D_MODEL), lambda i: (0, 0)),
        ],
        out_specs=pl.BlockSpec((COMBINE_TOKENS, D_MODEL), lambda i: (i, 0)),
        out_shape=jax.ShapeDtypeStruct((n, D_MODEL), F32),
        scratch_shapes=[
            pltpu.VMEM((2, TOP_K, COMBINE_TOKENS * ROW_TILE, LANES), F32),
            pltpu.SemaphoreType.DMA((2,)),
        ],
        compiler_params=pltpu.CompilerParams(
            dimension_semantics=("arbitrary",),
            vmem_limit_bytes=VMEM_LIMIT_BYTES),
        name="combine",
    )(dest, dest, ys, x1, route, gf)


def kernel(x, norm1_g, w_in, hgrn_lb, hgrn_norm_g, pool_w, pool_scale, w_out, norm2_g,
           router_w, router_b, w_gate, b_gate, w_up, b_up, w_down, b_down, norm_f_g):
    depth = w_in.shape[0]
    batch, seq, _ = x.shape
    n = batch * seq
    n_rows = n * TOP_K + N_EXPERTS * ROW_BLOCK
    lb_all = jnp.cumsum(jax.nn.softmax(hgrn_lb.astype(F32), axis=0), axis=0)

    xt = x.reshape(n, D_MODEL)
    for l in range(depth):
        wr = jnp.pad(router_w[l], ((0, 0), (0, LANES - N_EXPERTS)))
        wr_hi = wr.astype(BF16)
        wr_lo = (wr - wr_hi.astype(F32)).astype(BF16)
        br = jnp.pad(router_b[l], (0, LANES - N_EXPERTS), constant_values=NEG_BIG)[None, :]
        x1, h2, route, cnt = _mixer(
            xt, norm1_g[l][None, :], w_in[l].astype(BF16), lb_all[l][None, :],
            hgrn_norm_g[l][None, :], pool_w[l].astype(BF16), pool_scale[l][None, :],
            w_out[l].astype(BF16), norm2_g[l][None, :], wr_hi, wr_lo, br, seq)

        counts = cnt[0, :N_EXPERTS].astype(jnp.int32)
        padded = (counts + ROW_BLOCK - 1) // ROW_BLOCK * ROW_BLOCK
        pend = jnp.cumsum(padded).astype(jnp.int32)
        pstart = pend - padded
        n_used = (pend[-1:] // ROW_BLOCK).astype(jnp.int32)
        blk_row0 = jnp.arange(n_rows // ROW_BLOCK, dtype=jnp.int32) * ROW_BLOCK
        blk_e = jnp.minimum(
            jnp.sum((pend[None, :] <= blk_row0[:, None]).astype(jnp.int32), axis=1),
            N_EXPERTS - 1)
        top_e = route[:, 0:TOP_K].astype(jnp.int32)
        slot = route[:, 2 * TOP_K:3 * TOP_K].astype(jnp.int32)
        expert_ids = jnp.arange(N_EXPERTS, dtype=jnp.int32)
        dest = slot + jnp.sum(jnp.where(top_e[..., None] == expert_ids, pstart, 0), axis=-1)
        dest = dest.reshape(n * TOP_K)

        xs = _dispatch(pstart, pend, dest, h2, n_rows)
        ys = _experts(blk_e, n_used, xs, w_gate[l], b_gate[l], w_up[l], b_up[l],
                      w_down[l], b_down[l])
        xt = _combine(dest, ys, x1, route, norm_f_g[None, :], l == depth - 1)
    return xt.reshape(batch, seq, D_MODEL)
```

```python
import functools
import math

import numpy as np
import jax
import jax.numpy as jnp
from jax import lax
from jax.experimental import pallas as pl
from jax.experimental.pallas import tpu as pltpu

F32 = jnp.float32
BF16 = jnp.bfloat16

D_MODEL = 1024
HGRN_WIDTH = 512
HEAD_DIM = 128
N_HEADS = HGRN_WIDTH // HEAD_DIM
POOL_WIDTH = 512
POOL_WINDOWS = (2, 4, 8, 16)
POOL_GW = POOL_WIDTH // len(POOL_WINDOWS)
IN_COLS = 4 * HGRN_WIDTH + POOL_WIDTH
N_EXPERTS = 32
TOP_K = 4
SWIGLU_LIMIT = 7.0
SWIGLU_ALPHA = 1.702
NORM_EPS = 1e-6

LANES = 128
ROW_TILE = D_MODEL // LANES
VMEM_LIMIT_BYTES = 56 * 1024 * 1024

MIX_TOKENS = 512
CHUNK = 64
N_LEVELS = int(math.log2(CHUNK))
CHUNKS_PER_ITER = 4
HALO = max(POOL_WINDOWS)
ROW_BLOCK = 512
DISPATCH_TOKENS = 1024
COMBINE_TOKENS = 256
ROWS_PER_ITER = 8
NEG_BIG = -1e30


def _dot(a, b):
    return jnp.dot(a, b, preferred_element_type=F32)


def _dot_nt(a, b):
    return lax.dot_general(a, b, (((1,), (1,)), ((), ())), preferred_element_type=F32)


def _dot_tn(a, b):
    return lax.dot_general(a, b, (((0,), (0,)), ((), ())), preferred_element_type=F32)


def _sigmoid(x):
    return 1.0 / (1.0 + jnp.exp(-x))


def _rms(x, g):
    return x * lax.rsqrt(jnp.mean(x * x, axis=-1, keepdims=True) + NORM_EPS) * g


def _split_bf16(x):
    hi = x.astype(BF16)
    lo = (x - hi.astype(F32)).astype(BF16)
    return hi, lo


def _to_row_tiles(ref, value):
    t = value.shape[0]
    for j in range(ROW_TILE):
        ref[pl.ds(j, t, stride=ROW_TILE), :] = value[:, j * LANES:(j + 1) * LANES]


def _from_row_tiles(ref, t):
    return jnp.concatenate(
        [ref[pl.ds(j, t, stride=ROW_TILE), :] for j in range(ROW_TILE)], axis=1)


def _tile_rows(row):
    return pl.ds(pl.multiple_of(row * ROW_TILE, ROW_TILE), ROW_TILE)


def _decay_tables(ch):
    t = np.arange(ch)[:, None]
    u = np.arange(ch)[None, :]
    mats = [u <= t, u > t]
    masks = []
    m = 1
    while m < ch:
        right = (t // m) % 2 == 1
        mid = (t // (2 * m)) * (2 * m) + m
        mats.append(np.where(right, (u >= mid) & (u <= t), (u > t) & (u <= mid - 1)))
        masks.append(right & ((u // m) % 2 == 0) & ((u // (2 * m)) == (t // (2 * m))))
        m *= 2
    return (np.concatenate(mats, 0).astype(np.float32),
            np.stack(masks).astype(np.float32))


def _hgrn_chunk(qp, fp, v, gp, lb, hg, tabs, masks_ref, st_ref):
    heads = [slice(h * HEAD_DIM, (h + 1) * HEAD_DIM) for h in range(N_HEADS)]
    q = qp * _sigmoid(qp)
    f = lb + (1.0 - lb) * _sigmoid(fp)
    lf = jnp.log(f)
    k = 1.0 - f
    x = jnp.exp(_dot(tabs, jnp.concatenate(_split_bf16(lf), axis=0)))
    eb = x[0:CHUNK]
    esuf = x[CHUNK:2 * CHUNK]
    vb = v.astype(BF16)
    q_in = (q * eb).astype(BF16)
    k_end = (k * esuf).astype(BF16)
    qk = q * k
    st = [st_ref[h] for h in range(N_HEADS)]
    o = [_dot_nt(q_in[:, hs], st[h].astype(BF16)) for h, hs in enumerate(heads)]
    upd = [_dot_tn(vb[:, hs], k_end[:, hs]) for hs in heads]
    s = [jnp.zeros((CHUNK, CHUNK), F32) for _ in heads]
    for lvl in range(N_LEVELS):
        xl = x[(2 + lvl) * CHUNK:(3 + lvl) * CHUNK]
        ql = (q * xl).astype(BF16)
        kl = (k * xl).astype(BF16)
        mask = masks_ref[lvl]
        for h, hs in enumerate(heads):
            s[h] = s[h] + mask * _dot_nt(ql[:, hs], kl[:, hs])
    outs = []
    for h, hs in enumerate(heads):
        oh = o[h] + jnp.sum(qk[:, hs], axis=-1, keepdims=True) * v[:, hs]
        oh = oh + _dot(s[h].astype(BF16), vb[:, hs])
        st_ref[h] = eb[CHUNK - 1:CHUNK, hs] * st[h] + upd[h]
        oh = oh * lax.rsqrt(jnp.mean(oh * oh, axis=-1, keepdims=True) + NORM_EPS) * hg
        outs.append(oh)
    return jnp.concatenate(outs, axis=1) * (gp * _sigmoid(gp))


def _mixer_kernel(x_ref, g1_ref, win_ref, lb_ref, hg_ref, pw_ref, ps_ref, wout_ref, g2_ref,
                  wrh_ref, wrl_ref, br_ref, tabs_ref, masks_ref, tri_ref,
                  x1_ref, h2_ref, route_ref, cnt_ref,
                  proj_scr, mix_scr, st_scr, ubuf_scr, cnt_scr):
    b_i = pl.program_id(0)
    s_i = pl.program_id(1)

    @pl.when(s_i == 0)
    def _():
        st_scr[...] = jnp.zeros_like(st_scr)
        ubuf_scr[0:HALO, :] = jnp.zeros((HALO, POOL_WIDTH), F32)

    @pl.when((b_i == 0) & (s_i == 0))
    def _():
        cnt_scr[...] = jnp.zeros_like(cnt_scr)

    x = x_ref[...]
    h = _rms(x, g1_ref[...]).astype(BF16)
    proj_scr[...] = _dot(h, win_ref[...])

    tabs = tabs_ref[...]

    def chunk_body(c, carry):
        for j in range(CHUNKS_PER_ITER):
            rows = pl.ds(pl.multiple_of((c * CHUNKS_PER_ITER + j) * CHUNK, CHUNK), CHUNK)
            out = _hgrn_chunk(
                proj_scr[rows, 0:HGRN_WIDTH],
                proj_scr[rows, HGRN_WIDTH:2 * HGRN_WIDTH],
                proj_scr[rows, 2 * HGRN_WIDTH:3 * HGRN_WIDTH],
                proj_scr[rows, 3 * HGRN_WIDTH:4 * HGRN_WIDTH],
                lb_ref[...], hg_ref[...], tabs, masks_ref, st_scr)
            mix_scr[rows, 0:HGRN_WIDTH] = out.astype(BF16)
        return carry

    lax.fori_loop(0, MIX_TOKENS // (CHUNK * CHUNKS_PER_ITER), chunk_body, 0)

    ubuf_scr[HALO:HALO + MIX_TOKENS, :] = proj_scr[:, 4 * HGRN_WIDTH:IN_COLS]
    pos = (s_i * MIX_TOKENS + 1
           + lax.broadcasted_iota(jnp.int32, (MIX_TOKENS, 1), 0)).astype(F32)
    for g, win in enumerate(POOL_WINDOWS):
        lo = g * POOL_GW
        u = ubuf_scr[HALO:HALO + MIX_TOKENS, lo:lo + POOL_GW]
        acc = u
        for j in range(1, win):
            acc = acc + ubuf_scr[HALO - j:HALO - j + MIX_TOKENS, lo:lo + POOL_GW]
        d = acc / jnp.minimum(pos, float(win)) - u
        y = _dot(d.astype(BF16), pw_ref[g]) * ps_ref[:, lo:lo + POOL_GW]
        mix_scr[:, HGRN_WIDTH + lo:HGRN_WIDTH + lo + POOL_GW] = y.astype(BF16)
    ubuf_scr[0:HALO, :] = ubuf_scr[MIX_TOKENS:MIX_TOKENS + HALO, :]

    x1 = x + _dot(mix_scr[...], wout_ref[...])
    x1_ref[...] = x1
    h2 = _rms(x1, g2_ref[...])
    _to_row_tiles(h2_ref, h2)

    h_hi, h_lo = _split_bf16(h2)
    wrh = wrh_ref[...]
    logits = _dot(h_hi, wrh) + _dot(h_lo, wrh) + _dot(h_hi, wrl_ref[...]) + br_ref[...]
    lane = lax.broadcasted_iota(jnp.int32, (MIX_TOKENS, LANES), 1)
    work = logits
    tops, idxs, sels = [], [], []
    for _ in range(TOP_K):
        m = jnp.max(work, axis=-1, keepdims=True)
        idx = jnp.min(jnp.where(work == m, lane, LANES), axis=-1, keepdims=True)
        sel = lane == idx
        work = jnp.where(sel, 2.0 * NEG_BIG, work)
        tops.append(m)
        idxs.append(idx)
        sels.append(sel)
    exps = [jnp.exp(m - tops[0]) for m in tops]
    denom = exps[0] + exps[1] + exps[2] + exps[3]
    gates = [e / denom for e in exps]

    sel_any = jnp.zeros((MIX_TOKENS, LANES), F32)
    for sel in sels:
        sel_any = jnp.where(sel, 1.0, sel_any)
    cnt = cnt_scr[...]
    before = _dot(tri_ref[...], sel_any.astype(BF16)) + cnt
    cnt_new = cnt + jnp.sum(sel_any, axis=0, keepdims=True)
    cnt_scr[...] = cnt_new
    cnt_ref[...] = jnp.broadcast_to(cnt_new, cnt_ref.shape)

    route = jnp.zeros((MIX_TOKENS, LANES), F32)
    for k in range(TOP_K):
        slot = jnp.sum(jnp.where(sels[k], before, 0.0), axis=-1, keepdims=True)
        route = jnp.where(lane == k, idxs[k].astype(F32), route)
        route = jnp.where(lane == TOP_K + k, gates[k], route)
        route = jnp.where(lane == 2 * TOP_K + k, slot, route)
    route_ref[...] = route


def _mixer(x2d, g1, w_in, lb, hg, pool_w, pool_scale, w_out, g2, wr_hi, wr_lo, br, seq):
    n = x2d.shape[0]
    steps_per_seq = seq // MIX_TOKENS
    tabs_np, masks_np = _decay_tables(CHUNK)
    tabs = jnp.asarray(np.concatenate([tabs_np, tabs_np], axis=1), BF16)
    masks = jnp.asarray(masks_np, F32)
    tri = jnp.asarray(np.tril(np.ones((MIX_TOKENS, MIX_TOKENS), np.float32), -1), BF16)

    def tok(b, s):
        return (b * steps_per_seq + s, 0)

    def const2(b, s):
        return (0, 0)

    def const3(b, s):
        return (0, 0, 0)

    in_specs = [
        pl.BlockSpec((MIX_TOKENS, D_MODEL), tok),
        pl.BlockSpec((1, D_MODEL), const2),
        pl.BlockSpec((D_MODEL, IN_COLS), const2),
        pl.BlockSpec((1, HGRN_WIDTH), const2),
        pl.BlockSpec((1, HEAD_DIM), const2),
        pl.BlockSpec((len(POOL_WINDOWS), POOL_GW, POOL_GW), const3),
        pl.BlockSpec((1, POOL_WIDTH), const2),
        pl.BlockSpec((D_MODEL, D_MODEL), const2),
        pl.BlockSpec((1, D_MODEL), const2),
        pl.BlockSpec((D_MODEL, LANES), const2),
        pl.BlockSpec((D_MODEL, LANES), const2),
        pl.BlockSpec((1, LANES), const2),
        pl.BlockSpec(tabs.shape, const2),
        pl.BlockSpec(masks.shape, const3),
        pl.BlockSpec(tri.shape, const2),
    ]
    out_specs = [
        pl.BlockSpec((MIX_TOKENS, D_MODEL), tok),
        pl.BlockSpec((MIX_TOKENS * ROW_TILE, LANES), tok),
        pl.BlockSpec((MIX_TOKENS, LANES), tok),
        pl.BlockSpec((8, LANES), const2),
    ]
    out_shape = [
        jax.ShapeDtypeStruct((n, D_MODEL), F32),
        jax.ShapeDtypeStruct((n * ROW_TILE, LANES), F32),
        jax.ShapeDtypeStruct((n, LANES), F32),
        jax.ShapeDtypeStruct((8, LANES), F32),
    ]
    scratch = [
        pltpu.VMEM((MIX_TOKENS, IN_COLS), F32),
        pltpu.VMEM((MIX_TOKENS, D_MODEL), BF16),
        pltpu.VMEM((N_HEADS, HEAD_DIM, HEAD_DIM), F32),
        pltpu.VMEM((HALO + MIX_TOKENS, POOL_WIDTH), F32),
        pltpu.VMEM((1, LANES), F32),
    ]
    return pl.pallas_call(
        _mixer_kernel,
        grid=(n // seq, steps_per_seq),
        in_specs=in_specs,
        out_specs=out_specs,
        out_shape=out_shape,
        scratch_shapes=scratch,
        compiler_params=pltpu.CompilerParams(
            dimension_semantics=("arbitrary", "arbitrary"),
            vmem_limit_bytes=VMEM_LIMIT_BYTES),
        name="mixer",
    )(x2d, g1, w_in, lb, hg, pool_w, pool_scale, w_out, g2, wr_hi, wr_lo, br, tabs, masks, tri)


def _row_copy(src, src_row, dst, dst_row, sem):
    return pltpu.make_async_copy(src.at[_tile_rows(src_row), :], dst.at[_tile_rows(dst_row), :], sem)


def _dispatch_kernel(pstart_ref, pend_ref, dest_ref, h2_ref, xs_ref, zrow, sem, zsem):
    i = pl.program_id(0)

    @pl.when(i == 0)
    def _():
        zrow[...] = jnp.zeros_like(zrow)
        block_rows = ROW_BLOCK * ROW_TILE
        n_blocks = xs_ref.shape[0] // block_rows

        def block_copy(b):
            return pltpu.make_async_copy(
                zrow, xs_ref.at[pl.ds(pl.multiple_of(b * block_rows, block_rows), block_rows), :],
                zsem)

        def block_start(b, carry):
            block_copy(b).start()
            return carry

        def block_wait(b, carry):
            block_copy(b).wait()
            return carry

        first_unused = pend_ref[N_EXPERTS - 1] // ROW_BLOCK
        lax.fori_loop(first_unused, n_blocks, block_start, 0)
        lax.fori_loop(first_unused, n_blocks, block_wait, 0)

        def tail_copy(e):
            return block_copy(pend_ref[e] // ROW_BLOCK - 1)

        def tail_start(e, carry):
            @pl.when(pend_ref[e] > pstart_ref[e])
            def _():
                tail_copy(e).start()
            return carry

        def tail_wait(e, carry):
            @pl.when(pend_ref[e] > pstart_ref[e])
            def _():
                tail_copy(e).wait()
            return carry

        lax.fori_loop(0, N_EXPERTS, tail_start, 0)
        lax.fori_loop(0, N_EXPERTS, tail_wait, 0)

    def row_body(i2, carry):
        for j in range(ROWS_PER_ITER):
            r = i2 * ROWS_PER_ITER + j
            for k in range(TOP_K):
                _row_copy(h2_ref, r, xs_ref, dest_ref[r * TOP_K + k], sem).start(priority=k % 2)
        return carry

    lax.fori_loop(0, DISPATCH_TOKENS // ROWS_PER_ITER, row_body, 0)
    for k in range(TOP_K):
        pltpu.make_async_copy(h2_ref, xs_ref.at[pl.ds(0, DISPATCH_TOKENS * ROW_TILE), :],
                              sem).wait()


def _dispatch(pstart, pend, dest, h2, n_rows):
    n = h2.shape[0] // ROW_TILE
    grid_spec = pltpu.PrefetchScalarGridSpec(
        num_scalar_prefetch=2,
        grid=(n // DISPATCH_TOKENS,),
        in_specs=[
            pl.BlockSpec((DISPATCH_TOKENS * TOP_K,), lambda i, *_: (i,),
                         memory_space=pltpu.SMEM),
            pl.BlockSpec((DISPATCH_TOKENS * ROW_TILE, LANES), lambda i, *_: (i, 0)),
        ],
        out_specs=pl.BlockSpec(memory_space=pl.ANY),
        scratch_shapes=[
            pltpu.VMEM((ROW_BLOCK * ROW_TILE, LANES), F32),
            pltpu.SemaphoreType.DMA,
            pltpu.SemaphoreType.DMA,
        ],
    )
    return pl.pallas_call(
        _dispatch_kernel,
        grid_spec=grid_spec,
        out_shape=jax.ShapeDtypeStruct((n_rows * ROW_TILE, LANES), F32),
        compiler_params=pltpu.CompilerParams(
            dimension_semantics=("arbitrary",),
            vmem_limit_bytes=VMEM_LIMIT_BYTES),
        name="dispatch",
    )(pstart, pend, dest, h2)


def _expert_kernel(blk_e_ref, nused_ref, xs_ref, wg_ref, bg_ref, wu_ref, bu_ref, wd_ref, bd_ref,
                   ys_ref, wg_s, wu_s, wd_s):
    i = pl.program_id(0)
    active = i < nused_ref[0]
    changed = (i == 0) | (blk_e_ref[i] != blk_e_ref[jnp.maximum(i - 1, 0)])

    @pl.when(active & changed)
    def _():
        wg_s[...] = wg_ref[0].astype(BF16)
        wu_s[...] = wu_ref[0].astype(BF16)
        wd_s[...] = wd_ref[0].astype(BF16)

    @pl.when(active)
    def _():
        xb = _from_row_tiles(xs_ref, ROW_BLOCK).astype(BF16)
        gt = jnp.minimum(_dot(xb, wg_s[...]) + bg_ref[0], SWIGLU_LIMIT)
        up = jnp.clip(_dot(xb, wu_s[...]) + bu_ref[0], -SWIGLU_LIMIT, SWIGLU_LIMIT)
        act = (up + 1.0) * (gt * _sigmoid(SWIGLU_ALPHA * gt))
        _to_row_tiles(ys_ref, _dot(act.astype(BF16), wd_s[...]) + bd_ref[0])

    @pl.when(jnp.logical_not(active))
    def _():
        ys_ref[...] = jnp.zeros_like(ys_ref)


def _experts(blk_e, n_used, xs, w_gate, b_gate, w_up, b_up, w_down, b_down):
    n_rows = xs.shape[0] // ROW_TILE
    n_blocks = n_rows // ROW_BLOCK
    rows_spec_shape = (ROW_BLOCK * ROW_TILE, LANES)

    def blk(i, be, nu):
        return jnp.minimum(i, nu[0] - 1)

    def row_map(i, be, nu):
        return (blk(i, be, nu), 0)

    def w_map(i, be, nu):
        return (be[blk(i, be, nu)], 0, 0)

    w_spec = pl.BlockSpec((1, D_MODEL, D_MODEL), w_map)
    b_spec = pl.BlockSpec((1, 1, D_MODEL), w_map)
    grid_spec = pltpu.PrefetchScalarGridSpec(
        num_scalar_prefetch=2,
        grid=(n_blocks,),
        in_specs=[pl.BlockSpec(rows_spec_shape, row_map),
                  w_spec, b_spec, w_spec, b_spec, w_spec, b_spec],
        out_specs=pl.BlockSpec(rows_spec_shape, lambda i, be, nu: (i, 0)),
        scratch_shapes=[pltpu.VMEM((D_MODEL, D_MODEL), BF16)] * 3,
    )
    return pl.pallas_call(
        _expert_kernel,
        grid_spec=grid_spec,
        out_shape=jax.ShapeDtypeStruct((n_rows * ROW_TILE, LANES), F32),
        compiler_params=pltpu.CompilerParams(
            dimension_semantics=("arbitrary",),
            vmem_limit_bytes=VMEM_LIMIT_BYTES),
        name="experts",
    )(blk_e, n_used, xs, w_gate, b_gate[:, None, :], w_up, b_up[:, None, :],
      w_down, b_down[:, None, :])


def _combine_kernel(final, dest_ref, dest_next_ref, ys_ref, x1_ref, route_ref, gf_ref, out_ref,
                    buf, sems):
    i = pl.program_id(0)
    n_steps = pl.num_programs(0)
    slot = lax.rem(i, 2)

    def gather(idx_ref, s):
        def row_body(it, carry):
            for j in range(ROWS_PER_ITER):
                r = it * ROWS_PER_ITER + j
                for k in range(TOP_K):
                    pltpu.make_async_copy(ys_ref.at[_tile_rows(idx_ref[r * TOP_K + k]), :],
                                          buf.at[s, k, _tile_rows(r), :],
                                          sems.at[s]).start(priority=k % 2)
            return carry

        lax.fori_loop(0, COMBINE_TOKENS // ROWS_PER_ITER, row_body, 0)

    @pl.when(i == 0)
    def _():
        gather(dest_ref, 0)

    @pl.when(i + 1 < n_steps)
    def _():
        gather(dest_next_ref, 1 - slot)

    for k in range(TOP_K):
        pltpu.make_async_copy(ys_ref.at[pl.ds(0, COMBINE_TOKENS * ROW_TILE), :],
                              buf.at[slot, k], sems.at[slot]).wait()

    route = route_ref[...]
    acc = x1_ref[...]
    for k in range(TOP_K):
        acc = acc + route[:, TOP_K + k:TOP_K + k + 1] * _from_row_tiles(
            buf.at[slot, k], COMBINE_TOKENS)
    out_ref[...] = _rms(acc, gf_ref[...]) if final else acc


def _combine(dest, ys, x1, route, gf, final):
    n = x1.shape[0]
    n_steps = n // COMBINE_TOKENS
    idx_block = (COMBINE_TOKENS * TOP_K,)
    return pl.pallas_call(
        functools.partial(_combine_kernel, final),
        grid=(n_steps,),
        in_specs=[
            pl.BlockSpec(idx_block, lambda i: (i,), memory_space=pltpu.SMEM),
            pl.BlockSpec(idx_block, lambda i: (jnp.minimum(i + 1, n_steps - 1),),
                         memory_space=pltpu.SMEM),
            pl.BlockSpec(memory_space=pl.ANY),
            pl.BlockSpec((COMBINE_TOKENS, D_MODEL), lambda i: (i, 0)),
            pl.BlockSpec((COMBINE_TOKENS, LANES), lambda i: (i, 0)),
            pl.BlockSpec((1, D_MODEL), lambda i: (0, 0)),
        ],
        out_specs=pl.BlockSpec((COMBINE_TOKENS, D_MODEL), lambda i: (i, 0)),
        out_shape=jax.ShapeDtypeStruct((n, D_MODEL), F32),
        scratch_shapes=[
            pltpu.VMEM((2, TOP_K, COMBINE_TOKENS * ROW_TILE, LANES), F32),
            pltpu.SemaphoreType.DMA((2,)),
        ],
        compiler_params=pltpu.CompilerParams(
            dimension_semantics=("arbitrary",),
            vmem_limit_bytes=VMEM_LIMIT_BYTES),
        name="combine",
    )(dest, dest, ys, x1, route, gf)


def kernel(x, norm1_g, w_in, hgrn_lb, hgrn_norm_g, pool_w, pool_scale, w_out, norm2_g,
           router_w, router_b, w_gate, b_gate, w_up, b_up, w_down, b_down, norm_f_g):
    depth = w_in.shape[0]
    batch, seq, _ = x.shape
    n = batch * seq
    n_rows = n * TOP_K + N_EXPERTS * ROW_BLOCK
    lb_all = jnp.cumsum(jax.nn.softmax(hgrn_lb.astype(F32), axis=0), axis=0)

    xt = x.reshape(n, D_MODEL)
    for l in range(depth):
        wr = jnp.pad(router_w[l], ((0, 0), (0, LANES - N_EXPERTS)))
        wr_hi = wr.astype(BF16)
        wr_lo = (wr - wr_hi.astype(F32)).astype(BF16)
        br = jnp.pad(router_b[l], (0, LANES - N_EXPERTS), constant_values=NEG_BIG)[None, :]
        x1, h2, route, cnt = _mixer(
            xt, norm1_g[l][None, :], w_in[l].astype(BF16), lb_all[l][None, :],
            hgrn_norm_g[l][None, :], pool_w[l].astype(BF16), pool_scale[l][None, :],
            w_out[l].astype(BF16), norm2_g[l][None, :], wr_hi, wr_lo, br, seq)

        counts = cnt[0, :N_EXPERTS].astype(jnp.int32)
        padded = (counts + ROW_BLOCK - 1) // ROW_BLOCK * ROW_BLOCK
        pend = jnp.cumsum(padded).astype(jnp.int32)
        pstart = pend - padded
        n_used = (pend[-1:] // ROW_BLOCK).astype(jnp.int32)
        blk_row0 = jnp.arange(n_rows // ROW_BLOCK, dtype=jnp.int32) * ROW_BLOCK
        blk_e = jnp.minimum(
            jnp.sum((pend[None, :] <= blk_row0[:, None]).astype(jnp.int32), axis=1),
            N_EXPERTS - 1)
        top_e = route[:, 0:TOP_K].astype(jnp.int32)
        slot = route[:, 2 * TOP_K:3 * TOP_K].astype(jnp.int32)
        expert_ids = jnp.arange(N_EXPERTS, dtype=jnp.int32)
        dest = slot + jnp.sum(jnp.where(top_e[..., None] == expert_ids, pstart, 0), axis=-1)
        dest = dest.reshape(n * TOP_K)

        xs = _dispatch(pstart, pend, dest, h2, n_rows)
        ys = _experts(blk_e, n_used, xs, w_gate[l], b_gate[l], w_up[l], b_up[l],
                      w_down[l], b_down[l])
        xt = _combine(dest, ys, x1, route, norm_f_g[None, :], l == depth - 1)
    return xt.reshape(batch, seq, D_MODEL)
```

```python
import functools
import math

import numpy as np
import jax
import jax.numpy as jnp
from jax import lax
from jax.experimental import pallas as pl
from jax.experimental.pallas import tpu as pltpu

F32 = jnp.float32
BF16 = jnp.bfloat16

D_MODEL = 1024
HGRN_WIDTH = 512
HEAD_DIM = 128
N_HEADS = HGRN_WIDTH // HEAD_DIM
POOL_WIDTH = 512
POOL_WINDOWS = (2, 4, 8, 16)
POOL_GW = POOL_WIDTH // len(POOL_WINDOWS)
IN_COLS = 4 * HGRN_WIDTH + POOL_WIDTH
N_EXPERTS = 32
TOP_K = 4
SWIGLU_LIMIT = 7.0
SWIGLU_ALPHA = 1.702
NORM_EPS = 1e-6

LANES = 128
ROW_TILE = D_MODEL // LANES
VMEM_LIMIT_BYTES = 56 * 1024 * 1024

MIX_TOKENS = 512
CHUNK = 64
N_LEVELS = int(math.log2(CHUNK))
CHUNKS_PER_ITER = 4
HALO = max(POOL_WINDOWS)
ROW_BLOCK = 512
DISPATCH_TOKENS = 1024
COMBINE_TOKENS = 256
ROWS_PER_ITER = 8
ROUTE_ROWS = 16
LOG2_E = 1.4426950408889634


def _dot(a, b):
    return jnp.dot(a, b, preferred_element_type=F32)


def _dot_nt(a, b):
    return lax.dot_general(a, b, (((1,), (1,)), ((), ())), preferred_element_type=F32)


def _dot_tn(a, b):
    return lax.dot_general(a, b, (((0,), (0,)), ((), ())), preferred_element_type=F32)


def _sigmoid(x):
    return 1.0 / (1.0 + jnp.exp2(x * (-LOG2_E)))


def _rms(x, g):
    return x * lax.rsqrt(jnp.mean(x * x, axis=-1, keepdims=True) + NORM_EPS) * g


def _split_bf16(x):
    hi = x.astype(BF16)
    lo = (x - hi.astype(F32)).astype(BF16)
    return hi, lo


def _to_row_tiles(ref, value):
    t = value.shape[0]
    for j in range(ROW_TILE):
        ref[pl.ds(j, t, stride=ROW_TILE), :] = value[:, j * LANES:(j + 1) * LANES]


def _from_row_tiles(ref, t):
    return jnp.concatenate(
        [ref[pl.ds(j, t, stride=ROW_TILE), :] for j in range(ROW_TILE)], axis=1)


def _tile_rows(row):
    return pl.ds(pl.multiple_of(row * ROW_TILE, ROW_TILE), ROW_TILE)


def _decay_tables(ch):
    t = np.arange(ch)[:, None]
    u = np.arange(ch)[None, :]
    mats = [u <= t, u > t]
    masks = []
    m = 1
    while m < ch:
        right = (t // m) % 2 == 1
        mid = (t // (2 * m)) * (2 * m) + m
        mats.append(np.where(right, (u >= mid) & (u <= t), (u > t) & (u <= mid - 1)))
        masks.append(right & ((u // m) % 2 == 0) & ((u // (2 * m)) == (t // (2 * m))))
        m *= 2
    return (np.concatenate(mats, 0).astype(np.float32),
            np.stack(masks).astype(np.float32))


def _hgrn_chunk(qp, fp, v, gp, lb, hg, tabs, masks_ref, st_ref):
    heads = [slice(h * HEAD_DIM, (h + 1) * HEAD_DIM) for h in range(N_HEADS)]
    q = qp * _sigmoid(qp)
    f = lb + (1.0 - lb) * _sigmoid(fp)
    lf2 = jnp.log(f) * LOG2_E
    k = 1.0 - f
    x = jnp.exp2(_dot(tabs, jnp.concatenate(_split_bf16(lf2), axis=0)))
    eb = x[0:CHUNK]
    esuf = x[CHUNK:2 * CHUNK]
    vb = v.astype(BF16)
    q_in = (q * eb).astype(BF16)
    k_end = (k * esuf).astype(BF16)
    qk = q * k
    st = [st_ref[h] for h in range(N_HEADS)]
    o = [_dot_nt(q_in[:, hs], st[h].astype(BF16)) for h, hs in enumerate(heads)]
    upd = [_dot_tn(vb[:, hs], k_end[:, hs]) for hs in heads]
    s = [jnp.zeros((CHUNK, CHUNK), F32) for _ in heads]
    row = lax.broadcasted_iota(jnp.int32, (CHUNK, HGRN_WIDTH), 0)
    for lvl in range(N_LEVELS):
        m = 1 << lvl
        if m % 8 == 0:
            qk_rows = jnp.concatenate(
                [(q if (r0 // m) % 2 else k)[r0:r0 + m] for r0 in range(0, CHUNK, m)], axis=0)
        else:
            qk_rows = jnp.where((row & m) != 0, q, k)
        z = (qk_rows * x[(2 + lvl) * CHUNK:(3 + lvl) * CHUNK]).astype(BF16)
        mask = masks_ref[lvl] != 0.0
        for h, hs in enumerate(heads):
            s[h] = jnp.where(mask, _dot_nt(z[:, hs], z[:, hs]), s[h])
    outs = []
    for h, hs in enumerate(heads):
        oh = o[h] + jnp.sum(qk[:, hs], axis=-1, keepdims=True) * v[:, hs]
        oh = oh + _dot(s[h].astype(BF16), vb[:, hs])
        st_ref[h] = eb[CHUNK - 1:CHUNK, hs] * st[h] + upd[h]
        oh = oh * lax.rsqrt(jnp.mean(oh * oh, axis=-1, keepdims=True) + NORM_EPS) * hg
        outs.append(oh)
    return jnp.concatenate(outs, axis=1) * (gp * _sigmoid(gp))


def _mixer_kernel(x_ref, g1_ref, win_ref, lb_ref, hg_ref, pw_ref, ps_ref, wout_ref, g2_ref,
                  wrh_ref, wrl_ref, br_ref, tabs_ref, masks_ref, tri_ref,
                  x1_ref, h2_ref, route_ref, cnt_ref,
                  proj_scr, mix_scr, st_scr, ubuf_scr, cnt_scr):
    b_i = pl.program_id(0)
    s_i = pl.program_id(1)

    @pl.when(s_i == 0)
    def _():
        st_scr[...] = jnp.zeros_like(st_scr)
        ubuf_scr[0:HALO, :] = jnp.zeros((HALO, POOL_WIDTH), F32)

    @pl.when((b_i == 0) & (s_i == 0))
    def _():
        cnt_scr[...] = jnp.zeros_like(cnt_scr)

    x = x_ref[...]
    h = _rms(x, g1_ref[...]).astype(BF16)
    proj_scr[...] = _dot(h, win_ref[...])

    tabs = tabs_ref[...]

    def chunk_body(c, carry):
        for j in range(CHUNKS_PER_ITER):
            rows = pl.ds(pl.multiple_of((c * CHUNKS_PER_ITER + j) * CHUNK, CHUNK), CHUNK)
            out = _hgrn_chunk(
                proj_scr[rows, 0:HGRN_WIDTH],
                proj_scr[rows, HGRN_WIDTH:2 * HGRN_WIDTH],
                proj_scr[rows, 2 * HGRN_WIDTH:3 * HGRN_WIDTH],
                proj_scr[rows, 3 * HGRN_WIDTH:4 * HGRN_WIDTH],
                lb_ref[...], hg_ref[...], tabs, masks_ref, st_scr)
            mix_scr[rows, 0:HGRN_WIDTH] = out.astype(BF16)
        return carry

    lax.fori_loop(0, MIX_TOKENS // (CHUNK * CHUNKS_PER_ITER), chunk_body, 0)

    ubuf_scr[HALO:HALO + MIX_TOKENS, :] = proj_scr[:, 4 * HGRN_WIDTH:IN_COLS]
    pos = (s_i * MIX_TOKENS + 1
           + lax.broadcasted_iota(jnp.int32, (MIX_TOKENS, 1), 0)).astype(F32)
    for g, win in enumerate(POOL_WINDOWS):
        lo = g * POOL_GW
        ext = ubuf_scr[:, lo:lo + POOL_GW]
        acc = ext
        span = 1
        while span < win:
            acc = acc + pltpu.roll(acc, span, axis=0)
            span *= 2
        u = ext[HALO:]
        d = acc[HALO:] / jnp.minimum(pos, float(win)) - u
        y = _dot(d.astype(BF16), pw_ref[g]) * ps_ref[:, lo:lo + POOL_GW]
        mix_scr[:, HGRN_WIDTH + lo:HGRN_WIDTH + lo + POOL_GW] = y.astype(BF16)
    ubuf_scr[0:HALO, :] = ubuf_scr[MIX_TOKENS:MIX_TOKENS + HALO, :]

    x1 = x + _dot(mix_scr[...], wout_ref[...])
    x1_ref[...] = x1
    h2 = _rms(x1, g2_ref[...])
    _to_row_tiles(h2_ref, h2)

    h_hi, h_lo = _split_bf16(h2)
    wrh = wrh_ref[...]
    logits = (_dot_nt(wrh, h_hi) + _dot_nt(wrh, h_lo) + _dot_nt(wrl_ref[...], h_hi)
              + br_ref[...])
    eidx = lax.broadcasted_iota(jnp.int32, (N_EXPERTS, MIX_TOKENS), 0)
    work = logits
    tops, idxs, sels = [], [], []
    for _ in range(TOP_K):
        m = jnp.max(work, axis=0, keepdims=True)
        idx = jnp.min(jnp.where(work == m, eidx, N_EXPERTS), axis=0, keepdims=True)
        sel = eidx == idx
        work = jnp.where(sel, -jnp.inf, work)
        tops.append(m)
        idxs.append(idx)
        sels.append(sel)
    exps = [jnp.exp(m - tops[0]) for m in tops]
    denom = exps[0] + exps[1] + exps[2] + exps[3]
    gates = [e / denom for e in exps]

    sel_any = jnp.zeros((N_EXPERTS, MIX_TOKENS), F32)
    for sel in sels:
        sel_any = jnp.where(sel, 1.0, sel_any)
    cnt = cnt_scr[...]
    before = _dot(sel_any.astype(BF16), tri_ref[...]) + cnt
    cnt_new = cnt + jnp.sum(sel_any, axis=1, keepdims=True)
    cnt_scr[...] = cnt_new
    cnt_ref[...] = jnp.broadcast_to(cnt_new, cnt_ref.shape)

    for k in range(TOP_K):
        slot = jnp.sum(jnp.where(sels[k], before, 0.0), axis=0, keepdims=True)
        route_ref[k:k + 1, :] = idxs[k].astype(F32)
        route_ref[TOP_K + k:TOP_K + k + 1, :] = gates[k]
        route_ref[2 * TOP_K + k:2 * TOP_K + k + 1, :] = slot
    route_ref[3 * TOP_K:, :] = jnp.zeros((ROUTE_ROWS - 3 * TOP_K, MIX_TOKENS), F32)


def _mixer(x2d, g1, w_in, lb, hg, pool_w, pool_scale, w_out, g2, wr_hi, wr_lo, br, seq):
    n = x2d.shape[0]
    steps_per_seq = seq // MIX_TOKENS
    tabs_np, masks_np = _decay_tables(CHUNK)
    tabs = jnp.asarray(np.concatenate([tabs_np, tabs_np], axis=1), BF16)
    masks = jnp.asarray(masks_np, F32)
    tri = jnp.asarray(np.triu(np.ones((MIX_TOKENS, MIX_TOKENS), np.float32), 1), BF16)

    def tok(b, s):
        return (b * steps_per_seq + s, 0)

    def tok_col(b, s):
        return (0, b * steps_per_seq + s)

    def const2(b, s):
        return (0, 0)

    def const3(b, s):
        return (0, 0, 0)

    in_specs = [
        pl.BlockSpec((MIX_TOKENS, D_MODEL), tok),
        pl.BlockSpec((1, D_MODEL), const2),
        pl.BlockSpec((D_MODEL, IN_COLS), const2),
        pl.BlockSpec((1, HGRN_WIDTH), const2),
        pl.BlockSpec((1, HEAD_DIM), const2),
        pl.BlockSpec((len(POOL_WINDOWS), POOL_GW, POOL_GW), const3),
        pl.BlockSpec((1, POOL_WIDTH), const2),
        pl.BlockSpec((D_MODEL, D_MODEL), const2),
        pl.BlockSpec((1, D_MODEL), const2),
        pl.BlockSpec((N_EXPERTS, D_MODEL), const2),
        pl.BlockSpec((N_EXPERTS, D_MODEL), const2),
        pl.BlockSpec((N_EXPERTS, 1), const2),
        pl.BlockSpec(tabs.shape, const2),
        pl.BlockSpec(masks.shape, const3),
        pl.BlockSpec(tri.shape, const2),
    ]
    out_specs = [
        pl.BlockSpec((MIX_TOKENS, D_MODEL), tok),
        pl.BlockSpec((MIX_TOKENS * ROW_TILE, LANES), tok),
        pl.BlockSpec((ROUTE_ROWS, MIX_TOKENS), tok_col),
        pl.BlockSpec((N_EXPERTS, LANES), const2),
    ]
    out_shape = [
        jax.ShapeDtypeStruct((n, D_MODEL), F32),
        jax.ShapeDtypeStruct((n * ROW_TILE, LANES), F32),
        jax.ShapeDtypeStruct((ROUTE_ROWS, n), F32),
        jax.ShapeDtypeStruct((N_EXPERTS, LANES), F32),
    ]
    scratch = [
        pltpu.VMEM((MIX_TOKENS, IN_COLS), F32),
        pltpu.VMEM((MIX_TOKENS, D_MODEL), BF16),
        pltpu.VMEM((N_HEADS, HEAD_DIM, HEAD_DIM), F32),
        pltpu.VMEM((HALO + MIX_TOKENS, POOL_WIDTH), F32),
        pltpu.VMEM((N_EXPERTS, 1), F32),
    ]
    return pl.pallas_call(
        _mixer_kernel,
        grid=(n // seq, steps_per_seq),
        in_specs=in_specs,
        out_specs=out_specs,
        out_shape=out_shape,
        scratch_shapes=scratch,
        compiler_params=pltpu.CompilerParams(
            dimension_semantics=("arbitrary", "arbitrary"),
            vmem_limit_bytes=VMEM_LIMIT_BYTES),
        name="mixer",
    )(x2d, g1, w_in, lb, hg, pool_w, pool_scale, w_out, g2, wr_hi, wr_lo, br, tabs, masks, tri)


def _row_copy(src, src_row, dst, dst_row, sem):
    return pltpu.make_async_copy(src.at[_tile_rows(src_row), :], dst.at[_tile_rows(dst_row), :], sem)


def _dispatch_kernel(pstart_ref, pend_ref, dest_ref, h2_ref, xs_ref, zrow, sem, zsem):
    i = pl.program_id(0)

    @pl.when(i == 0)
    def _():
        zrow[...] = jnp.zeros_like(zrow)
        block_rows = ROW_BLOCK * ROW_TILE
        n_blocks = xs_ref.shape[0] // block_rows

        def block_copy(b):
            return pltpu.make_async_copy(
                zrow, xs_ref.at[pl.ds(pl.multiple_of(b * block_rows, block_rows), block_rows), :],
                zsem)

        def block_start(b, carry):
            block_copy(b).start()
            return carry

        def block_wait(b, carry):
            block_copy(b).wait()
            return carry

        first_unused = pend_ref[N_EXPERTS - 1] // ROW_BLOCK
        lax.fori_loop(first_unused, n_blocks, block_start, 0)
        lax.fori_loop(first_unused, n_blocks, block_wait, 0)

        def tail_copy(e):
            return block_copy(pend_ref[e] // ROW_BLOCK - 1)

        def tail_start(e, carry):
            @pl.when(pend_ref[e] > pstart_ref[e])
            def _():
                tail_copy(e).start()
            return carry

        def tail_wait(e, carry):
            @pl.when(pend_ref[e] > pstart_ref[e])
            def _():
                tail_copy(e).wait()
            return carry

        lax.fori_loop(0, N_EXPERTS, tail_start, 0)
        lax.fori_loop(0, N_EXPERTS, tail_wait, 0)

    def row_body(i2, carry):
        for j in range(ROWS_PER_ITER):
            r = i2 * ROWS_PER_ITER + j
            for k in range(TOP_K):
                _row_copy(h2_ref, r, xs_ref, dest_ref[r * TOP_K + k], sem).start(priority=k % 2)
        return carry

    lax.fori_loop(0, DISPATCH_TOKENS // ROWS_PER_ITER, row_body, 0)
    for k in range(TOP_K):
        pltpu.make_async_copy(h2_ref, xs_ref.at[pl.ds(0, DISPATCH_TOKENS * ROW_TILE), :],
                              sem).wait()


def _dispatch(pstart, pend, dest, h2, n_rows):
    n = h2.shape[0] // ROW_TILE
    grid_spec = pltpu.PrefetchScalarGridSpec(
        num_scalar_prefetch=2,
        grid=(n // DISPATCH_TOKENS,),
        in_specs=[
            pl.BlockSpec((DISPATCH_TOKENS * TOP_K,), lambda i, *_: (i,),
                         memory_space=pltpu.SMEM),
            pl.BlockSpec((DISPATCH_TOKENS * ROW_TILE, LANES), lambda i, *_: (i, 0)),
        ],
        out_specs=pl.BlockSpec(memory_space=pl.ANY),
        scratch_shapes=[
            pltpu.VMEM((ROW_BLOCK * ROW_TILE, LANES), F32),
            pltpu.SemaphoreType.DMA,
            pltpu.SemaphoreType.DMA,
        ],
    )
    return pl.pallas_call(
        _dispatch_kernel,
        grid_spec=grid_spec,
        out_shape=jax.ShapeDtypeStruct((n_rows * ROW_TILE, LANES), F32),
        compiler_params=pltpu.CompilerParams(
            dimension_semantics=("arbitrary",),
            vmem_limit_bytes=VMEM_LIMIT_BYTES),
        name="dispatch",
    )(pstart, pend, dest, h2)


def _expert_kernel(blk_e_ref, nused_ref, xs_ref, wg_ref, bg_ref, wu_ref, bu_ref, wd_ref, bd_ref,
                   ys_ref, wg_s, wu_s, wd_s):
    i = pl.program_id(0)
    active = i < nused_ref[0]
    changed = (i == 0) | (blk_e_ref[i] != blk_e_ref[jnp.maximum(i - 1, 0)])

    @pl.when(active & changed)
    def _():
        wg_s[...] = wg_ref[0].astype(BF16)
        wu_s[...] = wu_ref[0].astype(BF16)
        wd_s[...] = wd_ref[0].astype(BF16)

    @pl.when(active)
    def _():
        xb = _from_row_tiles(xs_ref, ROW_BLOCK).astype(BF16)
        gt = jnp.minimum(_dot(xb, wg_s[...]) + bg_ref[0], SWIGLU_LIMIT)
        up = jnp.clip(_dot(xb, wu_s[...]) + bu_ref[0], -SWIGLU_LIMIT, SWIGLU_LIMIT)
        act = (up + 1.0) * (gt * _sigmoid(SWIGLU_ALPHA * gt))
        _to_row_tiles(ys_ref, _dot(act.astype(BF16), wd_s[...]) + bd_ref[0])

    @pl.when(jnp.logical_not(active))
    def _():
        ys_ref[...] = jnp.zeros_like(ys_ref)


def _experts(blk_e, n_used, xs, w_gate, b_gate, w_up, b_up, w_down, b_down):
    n_rows = xs.shape[0] // ROW_TILE
    n_blocks = n_rows // ROW_BLOCK
    rows_spec_shape = (ROW_BLOCK * ROW_TILE, LANES)

    def blk(i, be, nu):
        return jnp.minimum(i, nu[0] - 1)

    def row_map(i, be, nu):
        return (blk(i, be, nu), 0)

    def w_map(i, be, nu):
        return (be[blk(i, be, nu)], 0, 0)

    w_spec = pl.BlockSpec((1, D_MODEL, D_MODEL), w_map)
    b_spec = pl.BlockSpec((1, 1, D_MODEL), w_map)
    grid_spec = pltpu.PrefetchScalarGridSpec(
        num_scalar_prefetch=2,
        grid=(n_blocks,),
        in_specs=[pl.BlockSpec(rows_spec_shape, row_map),
                  w_spec, b_spec, w_spec, b_spec, w_spec, b_spec],
        out_specs=pl.BlockSpec(rows_spec_shape, lambda i, be, nu: (i, 0)),
        scratch_shapes=[pltpu.VMEM((D_MODEL, D_MODEL), BF16)] * 3,
    )
    return pl.pallas_call(
        _expert_kernel,
        grid_spec=grid_spec,
        out_shape=jax.ShapeDtypeStruct((n_rows * ROW_TILE, LANES), F32),
        compiler_params=pltpu.CompilerParams(
            dimension_semantics=("arbitrary",),
            vmem_limit_bytes=VMEM_LIMIT_BYTES),
        name="experts",
    )(blk_e, n_used, xs, w_gate, b_gate[:, None, :], w_up, b_up[:, None, :],
      w_down, b_down[:, None, :])


def _combine_kernel(final, dest_ref, dest_next_ref, ys_ref, x1_ref, route_ref, gf_ref, out_ref,
                    buf, sems):
    i = pl.program_id(0)
    n_steps = pl.num_programs(0)
    slot = lax.rem(i, 2)

    def gather(idx_ref, s):
        def row_body(it, carry):
            for j in range(ROWS_PER_ITER):
                r = it * ROWS_PER_ITER + j
                for k in range(TOP_K):
                    pltpu.make_async_copy(ys_ref.at[_tile_rows(idx_ref[r * TOP_K + k]), :],
                                          buf.at[s, k, _tile_rows(r), :],
                                          sems.at[s]).start(priority=k % 2)
            return carry

        lax.fori_loop(0, COMBINE_TOKENS // ROWS_PER_ITER, row_body, 0)

    @pl.when(i == 0)
    def _():
        gather(dest_ref, 0)

    @pl.when(i + 1 < n_steps)
    def _():
        gather(dest_next_ref, 1 - slot)

    for k in range(TOP_K):
        pltpu.make_async_copy(ys_ref.at[pl.ds(0, COMBINE_TOKENS * ROW_TILE), :],
                              buf.at[slot, k], sems.at[slot]).wait()

    route = route_ref[...]
    acc = x1_ref[...]
    for k in range(TOP_K):
        acc = acc + route[:, k:k + 1] * _from_row_tiles(
            buf.at[slot, k], COMBINE_TOKENS)
    out_ref[...] = _rms(acc, gf_ref[...]) if final else acc


def _combine(dest, ys, x1, route, gf, final):
    n = x1.shape[0]
    n_steps = n // COMBINE_TOKENS
    idx_block = (COMBINE_TOKENS * TOP_K,)
    return pl.pallas_call(
        functools.partial(_combine_kernel, final),
        grid=(n_steps,),
        in_specs=[
            pl.BlockSpec(idx_block, lambda i: (i,), memory_space=pltpu.SMEM),
            pl.BlockSpec(idx_block, lambda i: (jnp.minimum(i + 1, n_steps - 1),),
                         memory_space=pltpu.SMEM),
            pl.BlockSpec(memory_space=pl.ANY),
            pl.BlockSpec((COMBINE_TOKENS, D_MODEL), lambda i: (i, 0)),
            pl.BlockSpec((COMBINE_TOKENS, LANES), lambda i: (i, 0)),
            pl.BlockSpec((1, D_MODEL), lambda i: (0, 0)),
        ],
        out_specs=pl.BlockSpec((COMBINE_TOKENS, D_MODEL), lambda i: (i, 0)),
        out_shape=jax.ShapeDtypeStruct((n, D_MODEL), F32),
        scratch_shapes=[
            pltpu.VMEM((2, TOP_K, COMBINE_TOKENS * ROW_TILE, LANES), F32),
            pltpu.SemaphoreType.DMA((2,)),
        ],
        compiler_params=pltpu.CompilerParams(
            dimension_semantics=("arbitrary",),
            vmem_limit_bytes=VMEM_LIMIT_BYTES),
        name="combine",
    )(dest, dest, ys, x1, route, gf)


def kernel(x, norm1_g, w_in, hgrn_lb, hgrn_norm_g, pool_w, pool_scale, w_out, norm2_g,
           router_w, router_b, w_gate, b_gate, w_up, b_up, w_down, b_down, norm_f_g):
    depth = w_in.shape[0]
    batch, seq, _ = x.shape
    n = batch * seq
    n_rows = n * TOP_K + N_EXPERTS * ROW_BLOCK
    lb_all = jnp.cumsum(jax.nn.softmax(hgrn_lb.astype(F32), axis=0), axis=0)

    xt = x.reshape(n, D_MODEL)
    for l in range(depth):
        wr = router_w[l].T
        wr_hi = wr.astype(BF16)
        wr_lo = (wr - wr_hi.astype(F32)).astype(BF16)
        br = router_b[l][:, None]
        x1, h2, route, cnt = _mixer(
            xt, norm1_g[l][None, :], w_in[l].astype(BF16), lb_all[l][None, :],
            hgrn_norm_g[l][None, :], pool_w[l].astype(BF16), pool_scale[l][None, :],
            w_out[l].astype(BF16), norm2_g[l][None, :], wr_hi, wr_lo, br, seq)

        counts = cnt[:, 0].astype(jnp.int32)
        padded = (counts + ROW_BLOCK - 1) // ROW_BLOCK * ROW_BLOCK
        pend = jnp.cumsum(padded).astype(jnp.int32)
        pstart = pend - padded
        n_used = (pend[-1:] // ROW_BLOCK).astype(jnp.int32)
        blk_row0 = jnp.arange(n_rows // ROW_BLOCK, dtype=jnp.int32) * ROW_BLOCK
        blk_e = jnp.minimum(
            jnp.sum((pend[None, :] <= blk_row0[:, None]).astype(jnp.int32), axis=1),
            N_EXPERTS - 1)
        top_e = route[0:TOP_K].T.astype(jnp.int32)
        slot = route[2 * TOP_K:3 * TOP_K].T.astype(jnp.int32)
        gates = jnp.pad(route[TOP_K:2 * TOP_K].T, ((0, 0), (0, LANES - TOP_K)))
        expert_ids = jnp.arange(N_EXPERTS, dtype=jnp.int32)
        dest = slot + jnp.sum(jnp.where(top_e[..., None] == expert_ids, pstart, 0), axis=-1)
        dest = dest.reshape(n * TOP_K)

        xs = _dispatch(pstart, pend, dest, h2, n_rows)
        ys = _experts(blk_e, n_used, xs, w_gate[l], b_gate[l], w_up[l], b_up[l],
                      w_down[l], b_down[l])
        xt = _combine(dest, ys, x1, gates, norm_f_g[None, :], l == depth - 1)
    return xt.reshape(batch, seq, D_MODEL)
```

```python
import functools
import math

import numpy as np
import jax
import jax.numpy as jnp
from jax import lax
from jax.experimental import pallas as pl
from jax.experimental.pallas import tpu as pltpu

F32 = jnp.float32
BF16 = jnp.bfloat16

D_MODEL = 1024
HGRN_WIDTH = 512
HEAD_DIM = 128
N_HEADS = HGRN_WIDTH // HEAD_DIM
POOL_WIDTH = 512
POOL_WINDOWS = (2, 4, 8, 16)
POOL_GW = POOL_WIDTH // len(POOL_WINDOWS)
IN_COLS = 4 * HGRN_WIDTH + POOL_WIDTH
N_EXPERTS = 32
TOP_K = 4
SWIGLU_LIMIT = 7.0
SWIGLU_ALPHA = 1.702
NORM_EPS = 1e-6

LANES = 128
ROW_TILE = D_MODEL // LANES
VMEM_LIMIT_BYTES = 56 * 1024 * 1024

MIX_TOKENS = 512
CHUNK = 64
N_LEVELS = int(math.log2(CHUNK))
HALO = max(POOL_WINDOWS)
ROW_BLOCK = 512
ZERO_ROWS = ROW_BLOCK // 2
PAGE_PAD = 384
COMBINE_TOKENS = 256
ROWS_PER_ITER = 8
ROUTE_ROWS = 16
LOG2_E = 1.4426950408889634


def _dot(a, b):
    return jnp.dot(a, b, preferred_element_type=F32)


def _dot_nt(a, b):
    return lax.dot_general(a, b, (((1,), (1,)), ((), ())), preferred_element_type=F32)


def _dot_tn(a, b):
    return lax.dot_general(a, b, (((0,), (0,)), ((), ())), preferred_element_type=F32)


def _sigmoid(x):
    return 1.0 / (1.0 + jnp.exp2(x * (-LOG2_E)))


def _rms(x, g):
    return x * lax.rsqrt(jnp.mean(x * x, axis=-1, keepdims=True) + NORM_EPS) * g


def _split_bf16(x):
    hi = x.astype(BF16)
    lo = (x - hi.astype(F32)).astype(BF16)
    return hi, lo


def _to_row_tiles(ref, value):
    t = value.shape[0]
    for j in range(ROW_TILE):
        ref[pl.ds(j, t, stride=ROW_TILE), :] = value[:, j * LANES:(j + 1) * LANES]


def _from_row_tiles(ref, t):
    return jnp.concatenate(
        [ref[pl.ds(j, t, stride=ROW_TILE), :] for j in range(ROW_TILE)], axis=1)


def _tile_rows(row):
    return pl.ds(pl.multiple_of(row * ROW_TILE, ROW_TILE), ROW_TILE)


def _decay_tables(ch):
    t = np.arange(ch)[:, None]
    u = np.arange(ch)[None, :]
    mats = [u <= t, u > t]
    masks = []
    m = 1
    while m < ch:
        right = (t // m) % 2 == 1
        mid = (t // (2 * m)) * (2 * m) + m
        mats.append(np.where(right, (u >= mid) & (u <= t), (u > t) & (u <= mid - 1)))
        masks.append(right & ((u // m) % 2 == 0) & ((u // (2 * m)) == (t // (2 * m))))
        m *= 2
    return (np.concatenate(mats, 0).astype(np.float32),
            np.stack(masks).astype(np.float32))


def _hgrn_chunk(qp, fp, v, gp, lb, hg, tabs, masks_ref, st_ref):
    heads = [slice(h * HEAD_DIM, (h + 1) * HEAD_DIM) for h in range(N_HEADS)]
    q = qp * _sigmoid(qp)
    f = lb + (1.0 - lb) * _sigmoid(fp)
    lf2 = jnp.log(f) * LOG2_E
    k = 1.0 - f
    x = jnp.exp2(_dot(tabs, jnp.concatenate(_split_bf16(lf2), axis=0)))
    eb = x[0:CHUNK]
    esuf = x[CHUNK:2 * CHUNK]
    vb = v.astype(BF16)
    q_in = (q * eb).astype(BF16)
    k_end = (k * esuf).astype(BF16)
    qk = q * k
    st = [st_ref[h] for h in range(N_HEADS)]
    o = [_dot_nt(q_in[:, hs], st[h].astype(BF16)) for h, hs in enumerate(heads)]
    upd = [_dot_tn(vb[:, hs], k_end[:, hs]) for hs in heads]
    s = [jnp.zeros((CHUNK, CHUNK), F32) for _ in heads]
    row = lax.broadcasted_iota(jnp.int32, (CHUNK, HGRN_WIDTH), 0)
    for lvl in range(N_LEVELS):
        m = 1 << lvl
        if m % 8 == 0:
            qk_rows = jnp.concatenate(
                [(q if (r0 // m) % 2 else k)[r0:r0 + m] for r0 in range(0, CHUNK, m)], axis=0)
        else:
            qk_rows = jnp.where((row & m) != 0, q, k)
        z = (qk_rows * x[(2 + lvl) * CHUNK:(3 + lvl) * CHUNK]).astype(BF16)
        mask = masks_ref[lvl] != 0.0
        for h, hs in enumerate(heads):
            s[h] = jnp.where(mask, _dot_nt(z[:, hs], z[:, hs]), s[h])
    outs = []
    for h, hs in enumerate(heads):
        oh = o[h] + jnp.sum(qk[:, hs], axis=-1, keepdims=True) * v[:, hs]
        oh = oh + _dot(s[h].astype(BF16), vb[:, hs])
        st_ref[h] = eb[CHUNK - 1:CHUNK, hs] * st[h] + upd[h]
        oh = oh * lax.rsqrt(jnp.mean(oh * oh, axis=-1, keepdims=True) + NORM_EPS) * hg
        outs.append(oh)
    return jnp.concatenate(outs, axis=1) * (gp * _sigmoid(gp))


def _row_copy(src, src_row, dst, dst_row, sem):
    return pltpu.make_async_copy(src.at[_tile_rows(src_row), :], dst.at[_tile_rows(dst_row), :], sem)


def _mixer_kernel(n_pages,
                  x_ref, g1_ref, win_ref, lb_ref, hg_ref, pw_ref, ps_ref, wout_ref, g2_ref,
                  wrh_ref, wrl_ref, br_ref, tabs_ref, masks_ref, tri_ref, trie_ref,
                  x1_ref, route_ref, meta_ref, xs_ref,
                  proj_scr, mix_scr, st_scr, ubuf_scr, cnt_scr, page_scr, nfree_scr, pexp_scr,
                  h2buf, dst_vmem, dst_smem, tail_vmem, tail_smem, zblk, row_sem, idx_sem, zsem):
    b_i = pl.program_id(0)
    s_i = pl.program_id(1)
    step = b_i * pl.num_programs(1) + s_i
    n_steps = pl.num_programs(0) * pl.num_programs(1)

    @pl.when(s_i == 0)
    def _():
        st_scr[...] = jnp.zeros_like(st_scr)
        ubuf_scr[0:HALO, :] = jnp.zeros((HALO, POOL_WIDTH), F32)

    @pl.when(step == 0)
    def _():
        cnt_scr[...] = jnp.zeros_like(cnt_scr)
        page_scr[...] = jnp.zeros_like(page_scr)
        nfree_scr[...] = jnp.zeros_like(nfree_scr)
        pexp_scr[...] = jnp.full(pexp_scr.shape, -1.0, F32)
        zblk[...] = jnp.zeros_like(zblk)
        h2buf[...] = jnp.zeros_like(h2buf)

        def spare_body(r, carry):
            for k in range(TOP_K):
                dst_smem[k, r] = n_pages * ROW_BLOCK + r * TOP_K + k
            return carry

        lax.fori_loop(0, MIX_TOKENS, spare_body, 0)

    @pl.when(step > 0)
    def _():
        pltpu.make_async_copy(dst_vmem, dst_smem, idx_sem).wait()

    x = x_ref[...]
    h = _rms(x, g1_ref[...]).astype(BF16)
    proj_scr[...] = _dot(h, win_ref[...])

    tabs = tabs_ref[...]

    for c in range(MIX_TOKENS // CHUNK):
        for r in range(c * CHUNK, (c + 1) * CHUNK):
            for k in range(TOP_K):
                _row_copy(h2buf, r, xs_ref, dst_smem[k, r], row_sem).start(priority=k % 2)
        rows = slice(c * CHUNK, (c + 1) * CHUNK)
        out = _hgrn_chunk(
            proj_scr[rows, 0:HGRN_WIDTH],
            proj_scr[rows, HGRN_WIDTH:2 * HGRN_WIDTH],
            proj_scr[rows, 2 * HGRN_WIDTH:3 * HGRN_WIDTH],
            proj_scr[rows, 3 * HGRN_WIDTH:4 * HGRN_WIDTH],
            lb_ref[...], hg_ref[...], tabs, masks_ref, st_scr)
        mix_scr[rows, 0:HGRN_WIDTH] = out.astype(BF16)

    ubuf_scr[HALO:HALO + MIX_TOKENS, :] = proj_scr[:, 4 * HGRN_WIDTH:IN_COLS]
    pos = (s_i * MIX_TOKENS + 1
           + lax.broadcasted_iota(jnp.int32, (MIX_TOKENS, 1), 0)).astype(F32)
    for g, win in enumerate(POOL_WINDOWS):
        lo = g * POOL_GW
        ext = ubuf_scr[:, lo:lo + POOL_GW]
        acc = ext
        span = 1
        while span < win:
            acc = acc + pltpu.roll(acc, span, axis=0)
            span *= 2
        u = ext[HALO:]
        d = acc[HALO:] / jnp.minimum(pos, float(win)) - u
        y = _dot(d.astype(BF16), pw_ref[g]) * ps_ref[:, lo:lo + POOL_GW]
        mix_scr[:, HGRN_WIDTH + lo:HGRN_WIDTH + lo + POOL_GW] = y.astype(BF16)
    ubuf_scr[0:HALO, :] = ubuf_scr[MIX_TOKENS:MIX_TOKENS + HALO, :]

    x1 = x + _dot(mix_scr[...], wout_ref[...])
    x1_ref[...] = x1
    h2 = _rms(x1, g2_ref[...])

    h_hi, h_lo = _split_bf16(h2)
    wrh = wrh_ref[...]
    logits = (_dot_nt(wrh, h_hi) + _dot_nt(wrh, h_lo) + _dot_nt(wrl_ref[...], h_hi)
              + br_ref[...])
    eidx = lax.broadcasted_iota(jnp.int32, (N_EXPERTS, MIX_TOKENS), 0)
    work = logits
    tops, sels = [], []
    for _ in range(TOP_K):
        m = jnp.max(work, axis=0, keepdims=True)
        idx = jnp.min(jnp.where(work == m, eidx, N_EXPERTS), axis=0, keepdims=True)
        sel = eidx == idx
        work = jnp.where(sel, -jnp.inf, work)
        tops.append(m)
        sels.append(sel)
    exps = [jnp.exp(m - tops[0]) for m in tops]
    denom = exps[0] + exps[1] + exps[2] + exps[3]
    gates = [e / denom for e in exps]

    sel_any = jnp.zeros((N_EXPERTS, MIX_TOKENS), F32)
    for sel in sels:
        sel_any = jnp.where(sel, 1.0, sel_any)
    cnt = cnt_scr[...]
    before = _dot(sel_any.astype(BF16), tri_ref[...]) + cnt
    cnt_new = cnt + jnp.sum(sel_any, axis=1, keepdims=True)
    cnt_scr[...] = cnt_new

    shift = int(math.log2(ROW_BLOCK))
    before_i = before.astype(jnp.int32)
    cnt_i = cnt.astype(jnp.int32)
    page_lo = cnt_i >> shift
    starts_page = (cnt_i & (ROW_BLOCK - 1)) == 0
    page_hi = (cnt_new.astype(jnp.int32) - 1) >> shift
    need = (cnt_new > cnt) & (starts_page | (page_hi > page_lo))
    need_f = jnp.where(need, 1.0, 0.0)
    earlier = _dot(trie_ref[...], jnp.broadcast_to(need_f, (N_EXPERTS, LANES)).astype(BF16))
    nfree = nfree_scr[...]
    new_id = nfree + earlier[:, 0:1]
    new_idx = jnp.where(starts_page, page_lo, page_lo + 1)
    cur_page = page_scr[...]
    page = jnp.where(need & ((before_i >> shift) == new_idx), new_id, cur_page)
    dest_all = page * float(ROW_BLOCK) + (before_i & (ROW_BLOCK - 1)).astype(F32)
    page_scr[...] = jnp.where(need, new_id, cur_page)
    nfree_new = nfree + jnp.sum(need_f, axis=0, keepdims=True)
    nfree_scr[...] = nfree_new
    page_lane = lax.broadcasted_iota(jnp.int32, (N_EXPERTS, PAGE_PAD), 1).astype(F32)
    owner = jnp.where(need & (page_lane == new_id), eidx[:, 0:1].astype(F32), -1.0)
    pexp = jnp.maximum(pexp_scr[...], jnp.max(owner, axis=0, keepdims=True))
    pexp_scr[...] = pexp

    dests = []
    for k in range(TOP_K):
        dest = jnp.sum(jnp.where(sels[k], dest_all, 0.0), axis=0, keepdims=True)
        dests.append(dest)
        route_ref[k:k + 1, :] = dest
        route_ref[TOP_K + k:TOP_K + k + 1, :] = gates[k]
    route_ref[2 * TOP_K:, :] = jnp.zeros((ROUTE_ROWS - 2 * TOP_K, MIX_TOKENS), F32)
    meta_ref[0:1, :] = pexp
    meta_ref[1:2, :] = jnp.broadcast_to(nfree_new[0:1, :], (1, PAGE_PAD))
    meta_ref[2:, :] = jnp.zeros((6, PAGE_PAD), F32)

    for k in range(TOP_K):
        pltpu.make_async_copy(h2buf, xs_ref.at[pl.ds(0, MIX_TOKENS * ROW_TILE), :],
                              row_sem).wait()

    _to_row_tiles(h2buf, h2)
    dst_vmem[...] = jnp.concatenate(
        dests + [jnp.zeros((8 - TOP_K, MIX_TOKENS), F32)], axis=0).astype(jnp.int32)
    pltpu.make_async_copy(dst_vmem, dst_smem, idx_sem).start()

    @pl.when(step == n_steps - 1)
    def _():
        pltpu.make_async_copy(dst_vmem, dst_smem, idx_sem).wait()

        def row_body(it, carry):
            for j in range(ROWS_PER_ITER):
                r = it * ROWS_PER_ITER + j
                for k in range(TOP_K):
                    _row_copy(h2buf, r, xs_ref, dst_smem[k, r], row_sem).start(priority=k % 2)
            return carry

        lax.fori_loop(0, MIX_TOKENS // ROWS_PER_ITER, row_body, 0)
        for k in range(TOP_K):
            pltpu.make_async_copy(h2buf, xs_ref.at[pl.ds(0, MIX_TOKENS * ROW_TILE), :],
                                  row_sem).wait()

        tail_vmem[0] = jnp.broadcast_to(cnt_new, (N_EXPERTS, LANES)).astype(jnp.int32)
        tail_vmem[1] = jnp.broadcast_to(page_scr[...], (N_EXPERTS, LANES)).astype(jnp.int32)
        tail_vmem[2] = jnp.broadcast_to(nfree_new, (N_EXPERTS, LANES)).astype(jnp.int32)
        cp = pltpu.make_async_copy(tail_vmem, tail_smem, idx_sem)
        cp.start()
        cp.wait()

        def zero_copy(first_row, rows):
            return pltpu.make_async_copy(
                zblk.at[pl.ds(0, rows * ROW_TILE), :],
                xs_ref.at[pl.ds(pl.multiple_of(first_row * ROW_TILE, ROW_TILE), rows * ROW_TILE), :],
                zsem)

        def tail_pass(wait):
            def body(e, carry):
                used = tail_smem[0, e, 0] & (ROW_BLOCK - 1)
                first = tail_smem[1, e, 0] * ROW_BLOCK + used
                left = jnp.where(used == 0, 0, ROW_BLOCK - used)
                rows = ZERO_ROWS
                while rows >= 1:
                    @pl.when((left & rows) != 0)
                    def _():
                        cp = zero_copy(first, rows)
                        cp.wait() if wait else cp.start()
                    first = first + jnp.where((left & rows) != 0, rows, 0)
                    rows //= 2
                return carry

            lax.fori_loop(0, N_EXPERTS, body, 0)

        def page_pass(wait):
            def body(p, carry):
                for half in range(ROW_BLOCK // ZERO_ROWS):
                    cp = zero_copy(p * ROW_BLOCK + half * ZERO_ROWS, ZERO_ROWS)
                    cp.wait() if wait else cp.start()
                return carry

            lax.fori_loop(tail_smem[2, 0, 0], n_pages, body, 0)

        tail_pass(False)
        page_pass(False)
        tail_pass(True)
        page_pass(True)


def _mixer(x2d, g1, w_in, lb, hg, pool_w, pool_scale, w_out, g2, wr_hi, wr_lo, br, seq, n_pages):
    n = x2d.shape[0]
    steps_per_seq = seq // MIX_TOKENS
    assert MIX_TOKENS <= ROW_BLOCK and n_pages <= PAGE_PAD
    spare_pages = MIX_TOKENS * TOP_K // ROW_BLOCK
    tabs_np, masks_np = _decay_tables(CHUNK)
    tabs = jnp.asarray(np.concatenate([tabs_np, tabs_np], axis=1), BF16)
    masks = jnp.asarray(masks_np, F32)
    tri = jnp.asarray(np.triu(np.ones((MIX_TOKENS, MIX_TOKENS), np.float32), 1), BF16)
    tri_e = jnp.asarray(np.tril(np.ones((N_EXPERTS, N_EXPERTS), np.float32), -1), BF16)

    def tok(b, s):
        return (b * steps_per_seq + s, 0)

    def tok_col(b, s):
        return (0, b * steps_per_seq + s)

    def const2(b, s):
        return (0, 0)

    def const3(b, s):
        return (0, 0, 0)

    in_specs = [
        pl.BlockSpec((MIX_TOKENS, D_MODEL), tok),
        pl.BlockSpec((1, D_MODEL), const2),
        pl.BlockSpec((D_MODEL, IN_COLS), const2),
        pl.BlockSpec((1, HGRN_WIDTH), const2),
        pl.BlockSpec((1, HEAD_DIM), const2),
        pl.BlockSpec((len(POOL_WINDOWS), POOL_GW, POOL_GW), const3),
        pl.BlockSpec((1, POOL_WIDTH), const2),
        pl.BlockSpec((D_MODEL, D_MODEL), const2),
        pl.BlockSpec((1, D_MODEL), const2),
        pl.BlockSpec((N_EXPERTS, D_MODEL), const2),
        pl.BlockSpec((N_EXPERTS, D_MODEL), const2),
        pl.BlockSpec((N_EXPERTS, 1), const2),
        pl.BlockSpec(tabs.shape, const2),
        pl.BlockSpec(masks.shape, const3),
        pl.BlockSpec(tri.shape, const2),
        pl.BlockSpec(tri_e.shape, const2),
    ]
    out_specs = [
        pl.BlockSpec((MIX_TOKENS, D_MODEL), tok),
        pl.BlockSpec((ROUTE_ROWS, MIX_TOKENS), tok_col),
        pl.BlockSpec((8, PAGE_PAD), const2),
        pl.BlockSpec(memory_space=pl.ANY),
    ]
    out_shape = [
        jax.ShapeDtypeStruct((n, D_MODEL), F32),
        jax.ShapeDtypeStruct((ROUTE_ROWS, n), F32),
        jax.ShapeDtypeStruct((8, PAGE_PAD), F32),
        jax.ShapeDtypeStruct(((n_pages + spare_pages) * ROW_BLOCK * ROW_TILE, LANES), F32),
    ]
    scratch = [
        pltpu.VMEM((MIX_TOKENS, IN_COLS), F32),
        pltpu.VMEM((MIX_TOKENS, D_MODEL), BF16),
        pltpu.VMEM((N_HEADS, HEAD_DIM, HEAD_DIM), F32),
        pltpu.VMEM((HALO + MIX_TOKENS, POOL_WIDTH), F32),
        pltpu.VMEM((N_EXPERTS, 1), F32),
        pltpu.VMEM((N_EXPERTS, 1), F32),
        pltpu.VMEM((N_EXPERTS, 1), F32),
        pltpu.VMEM((1, PAGE_PAD), F32),
        pltpu.VMEM((MIX_TOKENS * ROW_TILE, LANES), F32),
        pltpu.VMEM((8, MIX_TOKENS), jnp.int32),
        pltpu.SMEM((8, MIX_TOKENS), jnp.int32),
        pltpu.VMEM((3, N_EXPERTS, LANES), jnp.int32),
        pltpu.SMEM((3, N_EXPERTS, LANES), jnp.int32),
        pltpu.VMEM((ZERO_ROWS * ROW_TILE, LANES), F32),
        pltpu.SemaphoreType.DMA,
        pltpu.SemaphoreType.DMA,
        pltpu.SemaphoreType.DMA,
    ]
    return pl.pallas_call(
        functools.partial(_mixer_kernel, n_pages),
        grid=(n // seq, steps_per_seq),
        in_specs=in_specs,
        out_specs=out_specs,
        out_shape=out_shape,
        scratch_shapes=scratch,
        compiler_params=pltpu.CompilerParams(
            dimension_semantics=("arbitrary", "arbitrary"),
            vmem_limit_bytes=VMEM_LIMIT_BYTES),
        name="mixer",
    )(x2d, g1, w_in, lb, hg, pool_w, pool_scale, w_out, g2, wr_hi, wr_lo, br, tabs, masks, tri,
      tri_e)


def _expert_kernel(blk_e_ref, nused_ref, blk_page_ref, xs_ref, wg_ref, bg_ref, wu_ref, bu_ref,
                   wd_ref, bd_ref, ys_ref, wg_s, wu_s, wd_s):
    i = pl.program_id(0)
    active = i < nused_ref[0]
    changed = (i == 0) | (blk_e_ref[i] != blk_e_ref[jnp.maximum(i - 1, 0)])

    @pl.when(active & changed)
    def _():
        wg_s[...] = wg_ref[0].astype(BF16)
        wu_s[...] = wu_ref[0].astype(BF16)
        wd_s[...] = wd_ref[0].astype(BF16)

    @pl.when(active)
    def _():
        xb = _from_row_tiles(xs_ref, ROW_BLOCK).astype(BF16)
        gt = jnp.minimum(_dot(xb, wg_s[...]) + bg_ref[0], SWIGLU_LIMIT)
        up = jnp.clip(_dot(xb, wu_s[...]) + bu_ref[0], -SWIGLU_LIMIT, SWIGLU_LIMIT)
        act = (up + 1.0) * (gt * _sigmoid(SWIGLU_ALPHA * gt))
        _to_row_tiles(ys_ref, _dot(act.astype(BF16), wd_s[...]) + bd_ref[0])

    @pl.when(jnp.logical_not(active))
    def _():
        ys_ref[...] = jnp.zeros_like(ys_ref)


def _experts(blk_e, n_used, blk_page, xs, n_blocks, w_gate, b_gate, w_up, b_up, w_down, b_down):
    rows_spec_shape = (ROW_BLOCK * ROW_TILE, LANES)

    def blk(i, be, nu):
        return jnp.minimum(i, nu[0] - 1)

    def row_map(i, be, nu, bp):
        return (bp[blk(i, be, nu)], 0)

    def w_map(i, be, nu, bp):
        return (be[blk(i, be, nu)], 0, 0)

    w_spec = pl.BlockSpec((1, D_MODEL, D_MODEL), w_map)
    b_spec = pl.BlockSpec((1, 1, D_MODEL), w_map)
    grid_spec = pltpu.PrefetchScalarGridSpec(
        num_scalar_prefetch=3,
        grid=(n_blocks,),
        in_specs=[pl.BlockSpec(rows_spec_shape, row_map),
                  w_spec, b_spec, w_spec, b_spec, w_spec, b_spec],
        out_specs=pl.BlockSpec(rows_spec_shape, lambda i, be, nu, bp: (bp[i], 0)),
        scratch_shapes=[pltpu.VMEM((D_MODEL, D_MODEL), BF16)] * 3,
    )
    return pl.pallas_call(
        _expert_kernel,
        grid_spec=grid_spec,
        out_shape=jax.ShapeDtypeStruct((n_blocks * ROW_BLOCK * ROW_TILE, LANES), F32),
        compiler_params=pltpu.CompilerParams(
            dimension_semantics=("arbitrary",),
            vmem_limit_bytes=VMEM_LIMIT_BYTES),
        name="experts",
    )(blk_e, n_used, blk_page, xs, w_gate, b_gate[:, None, :], w_up, b_up[:, None, :],
      w_down, b_down[:, None, :])


def _combine_kernel(final, dest_ref, dest_next_ref, ys_ref, x1_ref, route_ref, gf_ref, out_ref,
                    buf, sems):
    i = pl.program_id(0)
    n_steps = pl.num_programs(0)
    slot = lax.rem(i, 2)

    def gather(idx_ref, s):
        def row_body(it, carry):
            for j in range(ROWS_PER_ITER):
                r = it * ROWS_PER_ITER + j
                for k in range(TOP_K):
                    pltpu.make_async_copy(ys_ref.at[_tile_rows(idx_ref[r * TOP_K + k]), :],
                                          buf.at[s, k, _tile_rows(r), :],
                                          sems.at[s]).start(priority=k % 2)
            return carry

        lax.fori_loop(0, COMBINE_TOKENS // ROWS_PER_ITER, row_body, 0)

    @pl.when(i == 0)
    def _():
        gather(dest_ref, 0)

    @pl.when(i + 1 < n_steps)
    def _():
        gather(dest_next_ref, 1 - slot)

    for k in range(TOP_K):
        pltpu.make_async_copy(ys_ref.at[pl.ds(0, COMBINE_TOKENS * ROW_TILE), :],
                              buf.at[slot, k], sems.at[slot]).wait()

    route = route_ref[...]
    acc = x1_ref[...]
    for k in range(TOP_K):
        acc = acc + route[:, k:k + 1] * _from_row_tiles(
            buf.at[slot, k], COMBINE_TOKENS)
    out_ref[...] = _rms(acc, gf_ref[...]) if final else acc


def _combine(dest, ys, x1, route, gf, final):
    n = x1.shape[0]
    n_steps = n // COMBINE_TOKENS
    idx_block = (COMBINE_TOKENS * TOP_K,)
    return pl.pallas_call(
        functools.partial(_combine_kernel, final),
        grid=(n_steps,),
        in_specs=[
            pl.BlockSpec(idx_block, lambda i: (i,), memory_space=pltpu.SMEM),
            pl.BlockSpec(idx_block, lambda i: (jnp.minimum(i + 1, n_steps - 1),),
                         memory_space=pltpu.SMEM),
            pl.BlockSpec(memory_space=pl.ANY),
            pl.BlockSpec((COMBINE_TOKENS, D_MODEL), lambda i: (i, 0)),
            pl.BlockSpec((COMBINE_TOKENS, LANES), lambda i: (i, 0)),
            pl.BlockSpec((1, D_MODEL), lambda i: (0, 0)),
        ],
        out_specs=pl.BlockSpec((COMBINE_TOKENS, D_MODEL), lambda i: (i, 0)),
        out_shape=jax.ShapeDtypeStruct((n, D_MODEL), F32),
        scratch_shapes=[
            pltpu.VMEM((2, TOP_K, COMBINE_TOKENS * ROW_TILE, LANES), F32),
            pltpu.SemaphoreType.DMA((2,)),
        ],
        compiler_params=pltpu.CompilerParams(
            dimension_semantics=("arbitrary",),
            vmem_limit_bytes=VMEM_LIMIT_BYTES),
        name="combine",
    )(dest, dest, ys, x1, route, gf)


def kernel(x, norm1_g, w_in, hgrn_lb, hgrn_norm_g, pool_w, pool_scale, w_out, norm2_g,
           router_w, router_b, w_gate, b_gate, w_up, b_up, w_down, b_down, norm_f_g):
    depth = w_in.shape[0]
    batch, seq, _ = x.shape
    n = batch * seq
    n_pages = n * TOP_K // ROW_BLOCK + N_EXPERTS
    lb_all = jnp.cumsum(jax.nn.softmax(hgrn_lb.astype(F32), axis=0), axis=0)

    xt = x.reshape(n, D_MODEL)
    for l in range(depth):
        wr = router_w[l].T
        wr_hi = wr.astype(BF16)
        wr_lo = (wr - wr_hi.astype(F32)).astype(BF16)
        br = router_b[l][:, None]
        x1, route, meta, xs = _mixer(
            xt, norm1_g[l][None, :], w_in[l].astype(BF16), lb_all[l][None, :],
            hgrn_norm_g[l][None, :], pool_w[l].astype(BF16), pool_scale[l][None, :],
            w_out[l].astype(BF16), norm2_g[l][None, :], wr_hi, wr_lo, br, seq, n_pages)

        page_expert = meta[0, :n_pages].astype(jnp.int32)
        n_used = meta[1, 0:1].astype(jnp.int32)
        page_ids = jnp.arange(n_pages, dtype=jnp.int32)
        key = jnp.where(page_expert < 0, N_EXPERTS, page_expert) * PAGE_PAD + page_ids
        rank = jnp.sum((key[None, :] < key[:, None]).astype(jnp.int32), axis=1)
        at_step = rank[None, :] == page_ids[:, None]
        blk_page = jnp.sum(jnp.where(at_step, page_ids[None, :], 0), axis=1)
        blk_e = jnp.clip(jnp.sum(jnp.where(at_step, page_expert[None, :], 0), axis=1),
                         0, N_EXPERTS - 1)
        dest = route[0:TOP_K].T.astype(jnp.int32).reshape(n * TOP_K)
        gates = jnp.pad(route[TOP_K:2 * TOP_K].T, ((0, 0), (0, LANES - TOP_K)))

        ys = _experts(blk_e, n_used, blk_page, xs, n_pages, w_gate[l], b_gate[l], w_up[l],
                      b_up[l], w_down[l], b_down[l])
        xt = _combine(dest, ys, x1, gates, norm_f_g[None, :], l == depth - 1)
    return xt.reshape(batch, seq, D_MODEL)
```

```python
import functools
import math

import numpy as np
import jax
import jax.numpy as jnp
from jax import lax
from jax.experimental import pallas as pl
from jax.experimental.pallas import tpu as pltpu

F32 = jnp.float32
BF16 = jnp.bfloat16

D_MODEL = 1024
HGRN_WIDTH = 512
HEAD_DIM = 128
N_HEADS = HGRN_WIDTH // HEAD_DIM
POOL_WIDTH = 512
POOL_WINDOWS = (2, 4, 8, 16)
POOL_GW = POOL_WIDTH // len(POOL_WINDOWS)
IN_COLS = 4 * HGRN_WIDTH + POOL_WIDTH
N_EXPERTS = 32
TOP_K = 4
SWIGLU_LIMIT = 7.0
SWIGLU_ALPHA = 1.702
NORM_EPS = 1e-6

LANES = 128
ROW_TILE = D_MODEL // LANES
VMEM_LIMIT_BYTES = 56 * 1024 * 1024

MIX_TOKENS = 512
CHUNK = 64
N_LEVELS = int(math.log2(CHUNK))
HALO = max(POOL_WINDOWS)
ROW_BLOCK = 512
ZERO_ROWS = ROW_BLOCK // 2
PAGE_PAD = 384
COMBINE_TOKENS = 256
ROWS_PER_ITER = 8
ROUTE_ROWS = 16
LOG2_E = 1.4426950408889634


def _dot(a, b):
    return jnp.dot(a, b, preferred_element_type=F32)


def _dot_nt(a, b):
    return lax.dot_general(a, b, (((1,), (1,)), ((), ())), preferred_element_type=F32)


def _dot_tn(a, b):
    return lax.dot_general(a, b, (((0,), (0,)), ((), ())), preferred_element_type=F32)


def _sigmoid(x):
    return 1.0 / (1.0 + jnp.exp2(x * (-LOG2_E)))


def _rms(x, g):
    return x * lax.rsqrt(jnp.mean(x * x, axis=-1, keepdims=True) + NORM_EPS) * g


def _split_bf16(x):
    hi = x.astype(BF16)
    lo = (x - hi.astype(F32)).astype(BF16)
    return hi, lo


def _to_row_tiles(ref, value):
    t = value.shape[0]
    for j in range(ROW_TILE):
        ref[pl.ds(j, t, stride=ROW_TILE), :] = value[:, j * LANES:(j + 1) * LANES]


def _from_row_tiles(ref, t):
    return jnp.concatenate(
        [ref[pl.ds(j, t, stride=ROW_TILE), :] for j in range(ROW_TILE)], axis=1)


def _tile_rows(row):
    return pl.ds(pl.multiple_of(row * ROW_TILE, ROW_TILE), ROW_TILE)


def _decay_tables(ch):
    t = np.arange(ch)[:, None]
    u = np.arange(ch)[None, :]
    mats = [u <= t, u > t]
    masks = []
    m = 1
    while m < ch:
        right = (t // m) % 2 == 1
        mid = (t // (2 * m)) * (2 * m) + m
        mats.append(np.where(right, (u >= mid) & (u <= t), (u > t) & (u <= mid - 1)))
        masks.append(right & ((u // m) % 2 == 0) & ((u // (2 * m)) == (t // (2 * m))))
        m *= 2
    return (np.concatenate(mats, 0).astype(np.float32),
            np.stack(masks).astype(np.float32))


def _hgrn_chunk(qp, fp, v, gp, lb, hg, tabs, masks_ref, st_ref):
    heads = [slice(h * HEAD_DIM, (h + 1) * HEAD_DIM) for h in range(N_HEADS)]
    q = qp * _sigmoid(qp)
    f = lb + (1.0 - lb) * _sigmoid(fp)
    lf2 = jnp.log(f) * LOG2_E
    k = 1.0 - f
    x = jnp.exp2(_dot(tabs, jnp.concatenate(_split_bf16(lf2), axis=0)))
    eb = x[0:CHUNK]
    esuf = x[CHUNK:2 * CHUNK]
    vb = v.astype(BF16)
    q_in = (q * eb).astype(BF16)
    k_end = (k * esuf).astype(BF16)
    qk = q * k
    st = [st_ref[h] for h in range(N_HEADS)]
    o = [_dot_nt(q_in[:, hs], st[h].astype(BF16)) for h, hs in enumerate(heads)]
    upd = [_dot_tn(vb[:, hs], k_end[:, hs]) for hs in heads]
    s = [jnp.zeros((CHUNK, CHUNK), F32) for _ in heads]
    row = lax.broadcasted_iota(jnp.int32, (CHUNK, HGRN_WIDTH), 0)
    for lvl in range(N_LEVELS):
        m = 1 << lvl
        if m % 8 == 0:
            qk_rows = jnp.concatenate(
                [(q if (r0 // m) % 2 else k)[r0:r0 + m] for r0 in range(0, CHUNK, m)], axis=0)
        else:
            qk_rows = jnp.where((row & m) != 0, q, k)
        z = (qk_rows * x[(2 + lvl) * CHUNK:(3 + lvl) * CHUNK]).astype(BF16)
        mask = masks_ref[lvl] != 0.0
        for h, hs in enumerate(heads):
            s[h] = jnp.where(mask, _dot_nt(z[:, hs], z[:, hs]), s[h])
    outs = []
    for h, hs in enumerate(heads):
        oh = o[h] + jnp.sum(qk[:, hs], axis=-1, keepdims=True) * v[:, hs]
        oh = oh + _dot(s[h].astype(BF16), vb[:, hs])
        st_ref[h] = eb[CHUNK - 1:CHUNK, hs] * st[h] + upd[h]
        oh = oh * lax.rsqrt(jnp.mean(oh * oh, axis=-1, keepdims=True) + NORM_EPS) * hg
        outs.append(oh)
    return jnp.concatenate(outs, axis=1) * (gp * _sigmoid(gp))


def _row_copy(src, src_row, dst, dst_row, sem):
    return pltpu.make_async_copy(src.at[_tile_rows(src_row), :], dst.at[_tile_rows(dst_row), :], sem)


def _mixer_kernel(n_pages,
                  x_ref, g1_ref, win_ref, lb_ref, hg_ref, pw_ref, ps_ref, wout_ref, g2_ref,
                  wrh_ref, wrl_ref, br_ref, tabs_ref, masks_ref, tri_ref, trie_ref,
                  x1_ref, route_ref, meta_ref, xs_ref,
                  proj_scr, mix_scr, st_scr, ubuf_scr, cnt_scr, page_scr, nfree_scr, pexp_scr,
                  h2buf, dst_vmem, dst_smem, tail_vmem, tail_smem, zblk, row_sem, idx_sem, zsem):
    b_i = pl.program_id(0)
    s_i = pl.program_id(1)
    step = b_i * pl.num_programs(1) + s_i
    n_steps = pl.num_programs(0) * pl.num_programs(1)
    slot = lax.rem(step, 2)
    prev = 1 - slot

    @pl.when(s_i == 0)
    def _():
        st_scr[...] = jnp.zeros_like(st_scr)
        ubuf_scr[0:HALO, :] = jnp.zeros((HALO, POOL_WIDTH), F32)

    @pl.when(step == 0)
    def _():
        cnt_scr[...] = jnp.zeros_like(cnt_scr)
        page_scr[...] = jnp.zeros_like(page_scr)
        nfree_scr[...] = jnp.zeros_like(nfree_scr)
        pexp_scr[...] = jnp.full(pexp_scr.shape, -1.0, F32)
        zblk[...] = jnp.zeros_like(zblk)
        h2buf[1] = jnp.zeros(h2buf.shape[1:], F32)

        def spare_body(r, carry):
            for k in range(TOP_K):
                dst_smem[k, r] = n_pages * ROW_BLOCK + r * TOP_K + k
            return carry

        lax.fori_loop(0, MIX_TOKENS, spare_body, 0)

    @pl.when(step > 0)
    def _():
        pltpu.make_async_copy(dst_vmem, dst_smem, idx_sem).wait()
        for k in range(TOP_K):
            pltpu.make_async_copy(h2buf.at[slot], xs_ref.at[pl.ds(0, MIX_TOKENS * ROW_TILE), :],
                                  row_sem).wait()

    x = x_ref[...]
    h = _rms(x, g1_ref[...]).astype(BF16)
    proj_scr[...] = _dot(h, win_ref[...])

    tabs = tabs_ref[...]

    for c in range(MIX_TOKENS // CHUNK):
        for r in range(c * CHUNK, (c + 1) * CHUNK):
            for k in range(TOP_K):
                _row_copy(h2buf.at[prev], r, xs_ref, dst_smem[k, r], row_sem).start(priority=k % 2)
        rows = slice(c * CHUNK, (c + 1) * CHUNK)
        out = _hgrn_chunk(
            proj_scr[rows, 0:HGRN_WIDTH],
            proj_scr[rows, HGRN_WIDTH:2 * HGRN_WIDTH],
            proj_scr[rows, 2 * HGRN_WIDTH:3 * HGRN_WIDTH],
            proj_scr[rows, 3 * HGRN_WIDTH:4 * HGRN_WIDTH],
            lb_ref[...], hg_ref[...], tabs, masks_ref, st_scr)
        mix_scr[rows, 0:HGRN_WIDTH] = out.astype(BF16)

    ubuf_scr[HALO:HALO + MIX_TOKENS, :] = proj_scr[:, 4 * HGRN_WIDTH:IN_COLS]
    pos = (s_i * MIX_TOKENS + 1
           + lax.broadcasted_iota(jnp.int32, (MIX_TOKENS, 1), 0)).astype(F32)
    for g, win in enumerate(POOL_WINDOWS):
        lo = g * POOL_GW
        ext = ubuf_scr[:, lo:lo + POOL_GW]
        acc = ext
        span = 1
        while span < win:
            acc = acc + pltpu.roll(acc, span, axis=0)
            span *= 2
        u = ext[HALO:]
        d = acc[HALO:] / jnp.minimum(pos, float(win)) - u
        y = _dot(d.astype(BF16), pw_ref[g]) * ps_ref[:, lo:lo + POOL_GW]
        mix_scr[:, HGRN_WIDTH + lo:HGRN_WIDTH + lo + POOL_GW] = y.astype(BF16)
    ubuf_scr[0:HALO, :] = ubuf_scr[MIX_TOKENS:MIX_TOKENS + HALO, :]

    x1 = x + _dot(mix_scr[...], wout_ref[...])
    x1_ref[...] = x1
    h2 = _rms(x1, g2_ref[...])

    h_hi, h_lo = _split_bf16(h2)
    wrh = wrh_ref[...]
    logits = (_dot_nt(wrh, h_hi) + _dot_nt(wrh, h_lo) + _dot_nt(wrl_ref[...], h_hi)
              + br_ref[...])
    eidx = lax.broadcasted_iota(jnp.int32, (N_EXPERTS, MIX_TOKENS), 0)
    work = logits
    tops, sels = [], []
    for _ in range(TOP_K):
        m = jnp.max(work, axis=0, keepdims=True)
        idx = jnp.min(jnp.where(work == m, eidx, N_EXPERTS), axis=0, keepdims=True)
        sel = eidx == idx
        work = jnp.where(sel, -jnp.inf, work)
        tops.append(m)
        sels.append(sel)
    exps = [jnp.exp(m - tops[0]) for m in tops]
    denom = exps[0] + exps[1] + exps[2] + exps[3]
    gates = [e / denom for e in exps]

    sel_any = jnp.zeros((N_EXPERTS, MIX_TOKENS), F32)
    for sel in sels:
        sel_any = jnp.where(sel, 1.0, sel_any)
    cnt = cnt_scr[...]
    before = _dot(sel_any.astype(BF16), tri_ref[...]) + cnt
    cnt_new = cnt + jnp.sum(sel_any, axis=1, keepdims=True)
    cnt_scr[...] = cnt_new

    shift = int(math.log2(ROW_BLOCK))
    before_i = before.astype(jnp.int32)
    cnt_i = cnt.astype(jnp.int32)
    page_lo = cnt_i >> shift
    starts_page = (cnt_i & (ROW_BLOCK - 1)) == 0
    page_hi = (cnt_new.astype(jnp.int32) - 1) >> shift
    need = (cnt_new > cnt) & (starts_page | (page_hi > page_lo))
    need_f = jnp.where(need, 1.0, 0.0)
    earlier = _dot(trie_ref[...], jnp.broadcast_to(need_f, (N_EXPERTS, LANES)).astype(BF16))
    nfree = nfree_scr[...]
    new_id = nfree + earlier[:, 0:1]
    new_idx = jnp.where(starts_page, page_lo, page_lo + 1)
    cur_page = page_scr[...]
    page = jnp.where(need & ((before_i >> shift) == new_idx), new_id, cur_page)
    dest_all = page * float(ROW_BLOCK) + (before_i & (ROW_BLOCK - 1)).astype(F32)
    page_scr[...] = jnp.where(need, new_id, cur_page)
    nfree_new = nfree + jnp.sum(need_f, axis=0, keepdims=True)
    nfree_scr[...] = nfree_new
    page_lane = lax.broadcasted_iota(jnp.int32, (N_EXPERTS, PAGE_PAD), 1).astype(F32)
    owner = jnp.where(need & (page_lane == new_id), eidx[:, 0:1].astype(F32), -1.0)
    pexp = jnp.maximum(pexp_scr[...], jnp.max(owner, axis=0, keepdims=True))
    pexp_scr[...] = pexp

    dests = []
    for k in range(TOP_K):
        dest = jnp.sum(jnp.where(sels[k], dest_all, 0.0), axis=0, keepdims=True)
        dests.append(dest)
        route_ref[k:k + 1, :] = dest
        route_ref[TOP_K + k:TOP_K + k + 1, :] = gates[k]
    route_ref[2 * TOP_K:, :] = jnp.zeros((ROUTE_ROWS - 2 * TOP_K, MIX_TOKENS), F32)
    meta_ref[0:1, :] = pexp
    meta_ref[1:2, :] = jnp.broadcast_to(nfree_new[0:1, :], (1, PAGE_PAD))
    meta_ref[2:, :] = jnp.zeros((6, PAGE_PAD), F32)

    _to_row_tiles(h2buf.at[slot], h2)
    dst_vmem[...] = jnp.concatenate(
        dests + [jnp.zeros((8 - TOP_K, MIX_TOKENS), F32)], axis=0).astype(jnp.int32)
    pltpu.make_async_copy(dst_vmem, dst_smem, idx_sem).start()

    @pl.when(step == n_steps - 1)
    def _():
        pltpu.make_async_copy(dst_vmem, dst_smem, idx_sem).wait()
        for k in range(TOP_K):
            pltpu.make_async_copy(h2buf.at[prev], xs_ref.at[pl.ds(0, MIX_TOKENS * ROW_TILE), :],
                                  row_sem).wait()

        def row_body(it, carry):
            for j in range(ROWS_PER_ITER):
                r = it * ROWS_PER_ITER + j
                for k in range(TOP_K):
                    _row_copy(h2buf.at[slot], r, xs_ref, dst_smem[k, r],
                              row_sem).start(priority=k % 2)
            return carry

        lax.fori_loop(0, MIX_TOKENS // ROWS_PER_ITER, row_body, 0)
        for k in range(TOP_K):
            pltpu.make_async_copy(h2buf.at[slot], xs_ref.at[pl.ds(0, MIX_TOKENS * ROW_TILE), :],
                                  row_sem).wait()

        tail_vmem[0] = jnp.broadcast_to(cnt_new, (N_EXPERTS, LANES)).astype(jnp.int32)
        tail_vmem[1] = jnp.broadcast_to(page_scr[...], (N_EXPERTS, LANES)).astype(jnp.int32)
        tail_vmem[2] = jnp.broadcast_to(nfree_new, (N_EXPERTS, LANES)).astype(jnp.int32)
        cp = pltpu.make_async_copy(tail_vmem, tail_smem, idx_sem)
        cp.start()
        cp.wait()

        def zero_copy(first_row, rows):
            return pltpu.make_async_copy(
                zblk.at[pl.ds(0, rows * ROW_TILE), :],
                xs_ref.at[pl.ds(pl.multiple_of(first_row * ROW_TILE, ROW_TILE), rows * ROW_TILE), :],
                zsem)

        def tail_pass(wait):
            def body(e, carry):
                used = tail_smem[0, e, 0] & (ROW_BLOCK - 1)
                first = tail_smem[1, e, 0] * ROW_BLOCK + used
                left = jnp.where(used == 0, 0, ROW_BLOCK - used)
                rows = ZERO_ROWS
                while rows >= 1:
                    @pl.when((left & rows) != 0)
                    def _():
                        cp = zero_copy(first, rows)
                        cp.wait() if wait else cp.start()
                    first = first + jnp.where((left & rows) != 0, rows, 0)
                    rows //= 2
                return carry

            lax.fori_loop(0, N_EXPERTS, body, 0)

        def page_pass(wait):
            def body(p, carry):
                for half in range(ROW_BLOCK // ZERO_ROWS):
                    cp = zero_copy(p * ROW_BLOCK + half * ZERO_ROWS, ZERO_ROWS)
                    cp.wait() if wait else cp.start()
                return carry

            lax.fori_loop(tail_smem[2, 0, 0], n_pages, body, 0)

        tail_pass(False)
        page_pass(False)
        tail_pass(True)
        page_pass(True)


def _mixer(x2d, g1, w_in, lb, hg, pool_w, pool_scale, w_out, g2, wr_hi, wr_lo, br, seq, n_pages):
    n = x2d.shape[0]
    steps_per_seq = seq // MIX_TOKENS
    assert MIX_TOKENS <= ROW_BLOCK and n_pages <= PAGE_PAD
    spare_pages = MIX_TOKENS * TOP_K // ROW_BLOCK
    tabs_np, masks_np = _decay_tables(CHUNK)
    tabs = jnp.asarray(np.concatenate([tabs_np, tabs_np], axis=1), BF16)
    masks = jnp.asarray(masks_np, F32)
    tri = jnp.asarray(np.triu(np.ones((MIX_TOKENS, MIX_TOKENS), np.float32), 1), BF16)
    tri_e = jnp.asarray(np.tril(np.ones((N_EXPERTS, N_EXPERTS), np.float32), -1), BF16)

    def tok(b, s):
        return (b * steps_per_seq + s, 0)

    def tok_col(b, s):
        return (0, b * steps_per_seq + s)

    def const2(b, s):
        return (0, 0)

    def const3(b, s):
        return (0, 0, 0)

    in_specs = [
        pl.BlockSpec((MIX_TOKENS, D_MODEL), tok),
        pl.BlockSpec((1, D_MODEL), const2),
        pl.BlockSpec((D_MODEL, IN_COLS), const2),
        pl.BlockSpec((1, HGRN_WIDTH), const2),
        pl.BlockSpec((1, HEAD_DIM), const2),
        pl.BlockSpec((len(POOL_WINDOWS), POOL_GW, POOL_GW), const3),
        pl.BlockSpec((1, POOL_WIDTH), const2),
        pl.BlockSpec((D_MODEL, D_MODEL), const2),
        pl.BlockSpec((1, D_MODEL), const2),
        pl.BlockSpec((N_EXPERTS, D_MODEL), const2),
        pl.BlockSpec((N_EXPERTS, D_MODEL), const2),
        pl.BlockSpec((N_EXPERTS, 1), const2),
        pl.BlockSpec(tabs.shape, const2),
        pl.BlockSpec(masks.shape, const3),
        pl.BlockSpec(tri.shape, const2),
        pl.BlockSpec(tri_e.shape, const2),
    ]
    out_specs = [
        pl.BlockSpec((MIX_TOKENS, D_MODEL), tok),
        pl.BlockSpec((ROUTE_ROWS, MIX_TOKENS), tok_col),
        pl.BlockSpec((8, PAGE_PAD), const2),
        pl.BlockSpec(memory_space=pl.ANY),
    ]
    out_shape = [
        jax.ShapeDtypeStruct((n, D_MODEL), F32),
        jax.ShapeDtypeStruct((ROUTE_ROWS, n), F32),
        jax.ShapeDtypeStruct((8, PAGE_PAD), F32),
        jax.ShapeDtypeStruct(((n_pages + spare_pages) * ROW_BLOCK * ROW_TILE, LANES), F32),
    ]
    scratch = [
        pltpu.VMEM((MIX_TOKENS, IN_COLS), F32),
        pltpu.VMEM((MIX_TOKENS, D_MODEL), BF16),
        pltpu.VMEM((N_HEADS, HEAD_DIM, HEAD_DIM), F32),
        pltpu.VMEM((HALO + MIX_TOKENS, POOL_WIDTH), F32),
        pltpu.VMEM((N_EXPERTS, 1), F32),
        pltpu.VMEM((N_EXPERTS, 1), F32),
        pltpu.VMEM((N_EXPERTS, 1), F32),
        pltpu.VMEM((1, PAGE_PAD), F32),
        pltpu.VMEM((2, MIX_TOKENS * ROW_TILE, LANES), F32),
        pltpu.VMEM((8, MIX_TOKENS), jnp.int32),
        pltpu.SMEM((8, MIX_TOKENS), jnp.int32),
        pltpu.VMEM((3, N_EXPERTS, LANES), jnp.int32),
        pltpu.SMEM((3, N_EXPERTS, LANES), jnp.int32),
        pltpu.VMEM((ZERO_ROWS * ROW_TILE, LANES), F32),
        pltpu.SemaphoreType.DMA,
        pltpu.SemaphoreType.DMA,
        pltpu.SemaphoreType.DMA,
    ]
    return pl.pallas_call(
        functools.partial(_mixer_kernel, n_pages),
        grid=(n // seq, steps_per_seq),
        in_specs=in_specs,
        out_specs=out_specs,
        out_shape=out_shape,
        scratch_shapes=scratch,
        compiler_params=pltpu.CompilerParams(
            dimension_semantics=("arbitrary", "arbitrary"),
            vmem_limit_bytes=VMEM_LIMIT_BYTES),
        name="mixer",
    )(x2d, g1, w_in, lb, hg, pool_w, pool_scale, w_out, g2, wr_hi, wr_lo, br, tabs, masks, tri,
      tri_e)


def _expert_kernel(blk_e_ref, nused_ref, blk_page_ref, xs_ref, wg_ref, bg_ref, wu_ref, bu_ref,
                   wd_ref, bd_ref, ys_ref, wg_s, wu_s, wd_s):
    i = pl.program_id(0)
    active = i < nused_ref[0]
    changed = (i == 0) | (blk_e_ref[i] != blk_e_ref[jnp.maximum(i - 1, 0)])

    @pl.when(active & changed)
    def _():
        wg_s[...] = wg_ref[0].astype(BF16)
        wu_s[...] = wu_ref[0].astype(BF16)
        wd_s[...] = wd_ref[0].astype(BF16)

    @pl.when(active)
    def _():
        xb = _from_row_tiles(xs_ref, ROW_BLOCK).astype(BF16)
        gt = jnp.minimum(_dot(xb, wg_s[...]) + bg_ref[0], SWIGLU_LIMIT)
        up = jnp.clip(_dot(xb, wu_s[...]) + bu_ref[0], -SWIGLU_LIMIT, SWIGLU_LIMIT)
        act = (up + 1.0) * (gt * _sigmoid(SWIGLU_ALPHA * gt))
        _to_row_tiles(ys_ref, _dot(act.astype(BF16), wd_s[...]) + bd_ref[0])

    @pl.when(jnp.logical_not(active))
    def _():
        ys_ref[...] = jnp.zeros_like(ys_ref)


def _experts(blk_e, n_used, blk_page, xs, n_blocks, w_gate, b_gate, w_up, b_up, w_down, b_down):
    rows_spec_shape = (ROW_BLOCK * ROW_TILE, LANES)

    def blk(i, be, nu):
        return jnp.minimum(i, nu[0] - 1)

    def row_map(i, be, nu, bp):
        return (bp[blk(i, be, nu)], 0)

    def w_map(i, be, nu, bp):
        return (be[blk(i, be, nu)], 0, 0)

    w_spec = pl.BlockSpec((1, D_MODEL, D_MODEL), w_map)
    b_spec = pl.BlockSpec((1, 1, D_MODEL), w_map)
    grid_spec = pltpu.PrefetchScalarGridSpec(
        num_scalar_prefetch=3,
        grid=(n_blocks,),
        in_specs=[pl.BlockSpec(rows_spec_shape, row_map),
                  w_spec, b_spec, w_spec, b_spec, w_spec, b_spec],
        out_specs=pl.BlockSpec(rows_spec_shape, lambda i, be, nu, bp: (bp[i], 0)),
        scratch_shapes=[pltpu.VMEM((D_MODEL, D_MODEL), BF16)] * 3,
    )
    return pl.pallas_call(
        _expert_kernel,
        grid_spec=grid_spec,
        out_shape=jax.ShapeDtypeStruct((n_blocks * ROW_BLOCK * ROW_TILE, LANES), F32),
        compiler_params=pltpu.CompilerParams(
            dimension_semantics=("arbitrary",),
            vmem_limit_bytes=VMEM_LIMIT_BYTES),
        name="experts",
    )(blk_e, n_used, blk_page, xs, w_gate, b_gate[:, None, :], w_up, b_up[:, None, :],
      w_down, b_down[:, None, :])


def _combine_kernel(final, dest_ref, dest_next_ref, ys_ref, x1_ref, route_ref, gf_ref, out_ref,
                    buf, sems):
    i = pl.program_id(0)
    n_steps = pl.num_programs(0)
    slot = lax.rem(i, 2)

    def gather(idx_ref, s):
        def row_body(it, carry):
            for j in range(ROWS_PER_ITER):
                r = it * ROWS_PER_ITER + j
                for k in range(TOP_K):
                    pltpu.make_async_copy(ys_ref.at[_tile_rows(idx_ref[r * TOP_K + k]), :],
                                          buf.at[s, k, _tile_rows(r), :],
                                          sems.at[s]).start(priority=k % 2)
            return carry

        lax.fori_loop(0, COMBINE_TOKENS // ROWS_PER_ITER, row_body, 0)

    @pl.when(i == 0)
    def _():
        gather(dest_ref, 0)

    @pl.when(i + 1 < n_steps)
    def _():
        gather(dest_next_ref, 1 - slot)

    for k in range(TOP_K):
        pltpu.make_async_copy(ys_ref.at[pl.ds(0, COMBINE_TOKENS * ROW_TILE), :],
                              buf.at[slot, k], sems.at[slot]).wait()

    route = route_ref[...]
    acc = x1_ref[...]
    for k in range(TOP_K):
        acc = acc + route[:, k:k + 1] * _from_row_tiles(
            buf.at[slot, k], COMBINE_TOKENS)
    out_ref[...] = _rms(acc, gf_ref[...]) if final else acc


def _combine(dest, ys, x1, route, gf, final):
    n = x1.shape[0]
    n_steps = n // COMBINE_TOKENS
    idx_block = (COMBINE_TOKENS * TOP_K,)
    return pl.pallas_call(
        functools.partial(_combine_kernel, final),
        grid=(n_steps,),
        in_specs=[
            pl.BlockSpec(idx_block, lambda i: (i,), memory_space=pltpu.SMEM),
            pl.BlockSpec(idx_block, lambda i: (jnp.minimum(i + 1, n_steps - 1),),
                         memory_space=pltpu.SMEM),
            pl.BlockSpec(memory_space=pl.ANY),
            pl.BlockSpec((COMBINE_TOKENS, D_MODEL), lambda i: (i, 0)),
            pl.BlockSpec((COMBINE_TOKENS, LANES), lambda i: (i, 0)),
            pl.BlockSpec((1, D_MODEL), lambda i: (0, 0)),
        ],
        out_specs=pl.BlockSpec((COMBINE_TOKENS, D_MODEL), lambda i: (i, 0)),
        out_shape=jax.ShapeDtypeStruct((n, D_MODEL), F32),
        scratch_shapes=[
            pltpu.VMEM((2, TOP_K, COMBINE_TOKENS * ROW_TILE, LANES), F32),
            pltpu.SemaphoreType.DMA((2,)),
        ],
        compiler_params=pltpu.CompilerParams(
            dimension_semantics=("arbitrary",),
            vmem_limit_bytes=VMEM_LIMIT_BYTES),
        name="combine",
    )(dest, dest, ys, x1, route, gf)


def kernel(x, norm1_g, w_in, hgrn_lb, hgrn_norm_g, pool_w, pool_scale, w_out, norm2_g,
           router_w, router_b, w_gate, b_gate, w_up, b_up, w_down, b_down, norm_f_g):
    depth = w_in.shape[0]
    batch, seq, _ = x.shape
    n = batch * seq
    n_pages = n * TOP_K // ROW_BLOCK + N_EXPERTS
    lb_all = jnp.cumsum(jax.nn.softmax(hgrn_lb.astype(F32), axis=0), axis=0)

    xt = x.reshape(n, D_MODEL)
    for l in range(depth):
        wr = router_w[l].T
        wr_hi = wr.astype(BF16)
        wr_lo = (wr - wr_hi.astype(F32)).astype(BF16)
        br = router_b[l][:, None]
        x1, route, meta, xs = _mixer(
            xt, norm1_g[l][None, :], w_in[l].astype(BF16), lb_all[l][None, :],
            hgrn_norm_g[l][None, :], pool_w[l].astype(BF16), pool_scale[l][None, :],
            w_out[l].astype(BF16), norm2_g[l][None, :], wr_hi, wr_lo, br, seq, n_pages)

        page_expert = meta[0, :n_pages].astype(jnp.int32)
        n_used = meta[1, 0:1].astype(jnp.int32)
        page_ids = jnp.arange(n_pages, dtype=jnp.int32)
        key = jnp.where(page_expert < 0, N_EXPERTS, page_expert) * PAGE_PAD + page_ids
        rank = jnp.sum((key[None, :] < key[:, None]).astype(jnp.int32), axis=1)
        at_step = rank[None, :] == page_ids[:, None]
        blk_page = jnp.sum(jnp.where(at_step, page_ids[None, :], 0), axis=1)
        blk_e = jnp.clip(jnp.sum(jnp.where(at_step, page_expert[None, :], 0), axis=1),
                         0, N_EXPERTS - 1)
        dest = route[0:TOP_K].T.astype(jnp.int32).reshape(n * TOP_K)
        gates = jnp.pad(route[TOP_K:2 * TOP_K].T, ((0, 0), (0, LANES - TOP_K)))

        ys = _experts(blk_e, n_used, blk_page, xs, n_pages, w_gate[l], b_gate[l], w_up[l],
                      b_up[l], w_down[l], b_down[l])
        xt = _combine(dest, ys, x1, gates, norm_f_g[None, :], l == depth - 1)
    return xt.reshape(batch, seq, D_MODEL)
```

```python
import functools
import math

import numpy as np
import jax
import jax.numpy as jnp
from jax import lax
from jax.experimental import pallas as pl
from jax.experimental.pallas import tpu as pltpu

F32 = jnp.float32
BF16 = jnp.bfloat16

D_MODEL = 1024
HGRN_WIDTH = 512
HEAD_DIM = 128
N_HEADS = HGRN_WIDTH // HEAD_DIM
POOL_WIDTH = 512
POOL_WINDOWS = (2, 4, 8, 16)
POOL_GW = POOL_WIDTH // len(POOL_WINDOWS)
IN_COLS = 4 * HGRN_WIDTH + POOL_WIDTH
N_EXPERTS = 32
TOP_K = 4
SWIGLU_LIMIT = 7.0
SWIGLU_ALPHA = 1.702
NORM_EPS = 1e-6

LANES = 128
ROW_TILE = D_MODEL // LANES
VMEM_LIMIT_BYTES = 56 * 1024 * 1024

MIX_TOKENS = 512
CHUNK = 64
N_LEVELS = int(math.log2(CHUNK))
HALO = max(POOL_WINDOWS)
ROW_BLOCK = 512
ZERO_ROWS = ROW_BLOCK // 2
PAGE_PAD = 384
COMBINE_TOKENS = 256
ROWS_PER_ITER = 8
ROUTE_ROWS = 16
LOG2_E = 1.4426950408889634


def _dot(a, b):
    return jnp.dot(a, b, preferred_element_type=F32)


def _dot_nt(a, b):
    return lax.dot_general(a, b, (((1,), (1,)), ((), ())), preferred_element_type=F32)


def _dot_tn(a, b):
    return lax.dot_general(a, b, (((0,), (0,)), ((), ())), preferred_element_type=F32)


def _sigmoid(x):
    return 1.0 / (1.0 + jnp.exp2(x * (-LOG2_E)))


def _rms(x, g):
    return x * lax.rsqrt(jnp.mean(x * x, axis=-1, keepdims=True) + NORM_EPS) * g


def _split_bf16(x):
    hi = x.astype(BF16)
    lo = (x - hi.astype(F32)).astype(BF16)
    return hi, lo


def _to_row_tiles(ref, value):
    t = value.shape[0]
    for j in range(ROW_TILE):
        ref[pl.ds(j, t, stride=ROW_TILE), :] = value[:, j * LANES:(j + 1) * LANES]


def _from_row_tiles(ref, t):
    return jnp.concatenate(
        [ref[pl.ds(j, t, stride=ROW_TILE), :] for j in range(ROW_TILE)], axis=1)


def _tile_rows(row):
    return pl.ds(pl.multiple_of(row * ROW_TILE, ROW_TILE), ROW_TILE)


def _decay_tables(ch):
    t = np.arange(ch)[:, None]
    u = np.arange(ch)[None, :]
    mats = [u <= t, u > t]
    masks = []
    m = 1
    while m < ch:
        right = (t // m) % 2 == 1
        mid = (t // (2 * m)) * (2 * m) + m
        mats.append(np.where(right, (u >= mid) & (u <= t), (u > t) & (u <= mid - 1)))
        masks.append(right & ((u // m) % 2 == 0) & ((u // (2 * m)) == (t // (2 * m))))
        m *= 2
    return (np.concatenate(mats, 0).astype(np.float32),
            np.stack(masks).astype(np.float32))


def _hgrn_chunk(qp, fp, v, gp, lb, hg, tabs, masks_ref, st_ref):
    heads = [slice(h * HEAD_DIM, (h + 1) * HEAD_DIM) for h in range(N_HEADS)]
    q = qp * _sigmoid(qp)
    f = lb + (1.0 - lb) * _sigmoid(fp)
    lf2 = jnp.log(f) * LOG2_E
    k = 1.0 - f
    x = jnp.exp2(_dot(tabs, jnp.concatenate(_split_bf16(lf2), axis=0)))
    eb = x[0:CHUNK]
    esuf = x[CHUNK:2 * CHUNK]
    vb = v.astype(BF16)
    q_in = (q * eb).astype(BF16)
    k_end = (k * esuf).astype(BF16)
    qk = q * k
    st = [st_ref[h] for h in range(N_HEADS)]
    o = [_dot_nt(q_in[:, hs], st[h].astype(BF16)) for h, hs in enumerate(heads)]
    upd = [_dot_tn(vb[:, hs], k_end[:, hs]) for hs in heads]
    s = [jnp.zeros((CHUNK, CHUNK), F32) for _ in heads]
    row = lax.broadcasted_iota(jnp.int32, (CHUNK, HGRN_WIDTH), 0)
    for lvl in range(N_LEVELS):
        m = 1 << lvl
        if m % 8 == 0:
            qk_rows = jnp.concatenate(
                [(q if (r0 // m) % 2 else k)[r0:r0 + m] for r0 in range(0, CHUNK, m)], axis=0)
        else:
            qk_rows = jnp.where((row & m) != 0, q, k)
        z = (qk_rows * x[(2 + lvl) * CHUNK:(3 + lvl) * CHUNK]).astype(BF16)
        mask = masks_ref[lvl] != 0.0
        for h, hs in enumerate(heads):
            s[h] = jnp.where(mask, _dot_nt(z[:, hs], z[:, hs]), s[h])
    outs = []
    for h, hs in enumerate(heads):
        oh = o[h] + jnp.sum(qk[:, hs], axis=-1, keepdims=True) * v[:, hs]
        oh = oh + _dot(s[h].astype(BF16), vb[:, hs])
        st_ref[h] = eb[CHUNK - 1:CHUNK, hs] * st[h] + upd[h]
        oh = oh * lax.rsqrt(jnp.mean(oh * oh, axis=-1, keepdims=True) + NORM_EPS) * hg
        outs.append(oh)
    return jnp.concatenate(outs, axis=1) * (gp * _sigmoid(gp))


def _row_copy(src, src_row, dst, dst_row, sem):
    return pltpu.make_async_copy(src.at[_tile_rows(src_row), :], dst.at[_tile_rows(dst_row), :], sem)


def _mixer_kernel(n_pages,
                  x_ref, g1_ref, win_ref, lb_ref, hg_ref, pw_ref, ps_ref, wout_ref, g2_ref,
                  wrh_ref, wrl_ref, br_ref, tabs_ref, masks_ref, tri_ref, trie_ref,
                  x1_ref, route_ref, meta_ref, xs_ref,
                  proj_scr, mix_scr, st_scr, ubuf_scr, cnt_scr, page_scr, nfree_scr, pexp_scr,
                  h2buf, dst_vmem, dst_smem, tail_vmem, tail_smem, zblk, row_sem, idx_sem, zsem):
    b_i = pl.program_id(0)
    s_i = pl.program_id(1)
    step = b_i * pl.num_programs(1) + s_i
    n_steps = pl.num_programs(0) * pl.num_programs(1)
    slot = lax.rem(step, 2)
    prev = 1 - slot

    @pl.when(s_i == 0)
    def _():
        st_scr[...] = jnp.zeros_like(st_scr)
        ubuf_scr[0:HALO, :] = jnp.zeros((HALO, POOL_WIDTH), F32)

    @pl.when(step == 0)
    def _():
        cnt_scr[...] = jnp.zeros_like(cnt_scr)
        page_scr[...] = jnp.zeros_like(page_scr)
        nfree_scr[...] = jnp.zeros_like(nfree_scr)
        pexp_scr[...] = jnp.full(pexp_scr.shape, -1.0, F32)
        zblk[...] = jnp.zeros_like(zblk)
        h2buf[1] = jnp.zeros(h2buf.shape[1:], F32)

        def spare_body(r, carry):
            for k in range(TOP_K):
                dst_smem[k, r] = n_pages * ROW_BLOCK + r * TOP_K + k
            return carry

        lax.fori_loop(0, MIX_TOKENS, spare_body, 0)

    @pl.when(step > 0)
    def _():
        pltpu.make_async_copy(dst_vmem, dst_smem, idx_sem).wait()
        for k in range(TOP_K):
            pltpu.make_async_copy(h2buf.at[slot], xs_ref.at[pl.ds(0, MIX_TOKENS * ROW_TILE), :],
                                  row_sem).wait()

    x = x_ref[...]
    h = _rms(x, g1_ref[...]).astype(BF16)
    proj_scr[...] = _dot(h, win_ref[...])

    tabs = tabs_ref[...]

    for c in range(MIX_TOKENS // CHUNK):
        for r in range(c * CHUNK, (c + 1) * CHUNK):
            for k in range(TOP_K):
                _row_copy(h2buf.at[prev], r, xs_ref, dst_smem[k, r], row_sem).start(priority=k % 2)
        rows = slice(c * CHUNK, (c + 1) * CHUNK)
        out = _hgrn_chunk(
            proj_scr[rows, 0:HGRN_WIDTH],
            proj_scr[rows, HGRN_WIDTH:2 * HGRN_WIDTH],
            proj_scr[rows, 2 * HGRN_WIDTH:3 * HGRN_WIDTH],
            proj_scr[rows, 3 * HGRN_WIDTH:4 * HGRN_WIDTH],
            lb_ref[...], hg_ref[...], tabs, masks_ref, st_scr)
        mix_scr[rows, 0:HGRN_WIDTH] = out.astype(BF16)

    ubuf_scr[HALO:HALO + MIX_TOKENS, :] = proj_scr[:, 4 * HGRN_WIDTH:IN_COLS]
    pos = (s_i * MIX_TOKENS + 1
           + lax.broadcasted_iota(jnp.int32, (MIX_TOKENS, 1), 0)).astype(F32)
    for g, win in enumerate(POOL_WINDOWS):
        lo = g * POOL_GW
        ext = ubuf_scr[:, lo:lo + POOL_GW]
        acc = ext
        span = 1
        while span < win:
            acc = acc + pltpu.roll(acc, span, axis=0)
            span *= 2
        u = ext[HALO:]
        d = acc[HALO:] / jnp.minimum(pos, float(win)) - u
        y = _dot(d.astype(BF16), pw_ref[g]) * ps_ref[:, lo:lo + POOL_GW]
        mix_scr[:, HGRN_WIDTH + lo:HGRN_WIDTH + lo + POOL_GW] = y.astype(BF16)
    ubuf_scr[0:HALO, :] = ubuf_scr[MIX_TOKENS:MIX_TOKENS + HALO, :]

    x1 = x + _dot(mix_scr[...], wout_ref[...])
    x1_ref[...] = x1
    h2 = _rms(x1, g2_ref[...])

    h_hi, h_lo = _split_bf16(h2)
    wrh = wrh_ref[...]
    logits = (_dot_nt(wrh, h_hi) + _dot_nt(wrh, h_lo) + _dot_nt(wrl_ref[...], h_hi)
              + br_ref[...])
    eidx = lax.broadcasted_iota(jnp.int32, (N_EXPERTS, MIX_TOKENS), 0)
    work = logits
    tops, sels = [], []
    for _ in range(TOP_K):
        m = jnp.max(work, axis=0, keepdims=True)
        idx = jnp.min(jnp.where(work == m, eidx, N_EXPERTS), axis=0, keepdims=True)
        sel = eidx == idx
        work = jnp.where(sel, -jnp.inf, work)
        tops.append(m)
        sels.append(sel)
    exps = [jnp.exp(m - tops[0]) for m in tops]
    denom = exps[0] + exps[1] + exps[2] + exps[3]
    gates = [e / denom for e in exps]

    sel_any = jnp.zeros((N_EXPERTS, MIX_TOKENS), F32)
    for sel in sels:
        sel_any = jnp.where(sel, 1.0, sel_any)
    cnt = cnt_scr[...]
    before = _dot(sel_any.astype(BF16), tri_ref[...]) + cnt
    cnt_new = cnt + jnp.sum(sel_any, axis=1, keepdims=True)
    cnt_scr[...] = cnt_new

    shift = int(math.log2(ROW_BLOCK))
    before_i = before.astype(jnp.int32)
    cnt_i = cnt.astype(jnp.int32)
    page_lo = cnt_i >> shift
    starts_page = (cnt_i & (ROW_BLOCK - 1)) == 0
    page_hi = (cnt_new.astype(jnp.int32) - 1) >> shift
    need = (cnt_new > cnt) & (starts_page | (page_hi > page_lo))
    need_f = jnp.where(need, 1.0, 0.0)
    earlier = _dot(trie_ref[...], jnp.broadcast_to(need_f, (N_EXPERTS, LANES)).astype(BF16))
    nfree = nfree_scr[...]
    new_id = nfree + earlier[:, 0:1]
    new_idx = jnp.where(starts_page, page_lo, page_lo + 1)
    cur_page = page_scr[...]
    page = jnp.where(need & ((before_i >> shift) == new_idx), new_id, cur_page)
    dest_all = page * float(ROW_BLOCK) + (before_i & (ROW_BLOCK - 1)).astype(F32)
    page_scr[...] = jnp.where(need, new_id, cur_page)
    nfree_new = nfree + jnp.sum(need_f, axis=0, keepdims=True)
    nfree_scr[...] = nfree_new
    page_lane = lax.broadcasted_iota(jnp.int32, (N_EXPERTS, PAGE_PAD), 1).astype(F32)
    owner = jnp.where(need & (page_lane == new_id), eidx[:, 0:1].astype(F32), -1.0)
    pexp = jnp.maximum(pexp_scr[...], jnp.max(owner, axis=0, keepdims=True))
    pexp_scr[...] = pexp

    dests = []
    for k in range(TOP_K):
        dest = jnp.sum(jnp.where(sels[k], dest_all, 0.0), axis=0, keepdims=True)
        dests.append(dest)
        route_ref[k:k + 1, :] = dest
        route_ref[TOP_K + k:TOP_K + k + 1, :] = gates[k]
    route_ref[2 * TOP_K:, :] = jnp.zeros((ROUTE_ROWS - 2 * TOP_K, MIX_TOKENS), F32)
    meta_ref[0:1, :] = pexp
    meta_ref[1:2, :] = jnp.broadcast_to(nfree_new[0:1, :], (1, PAGE_PAD))
    meta_ref[2:, :] = jnp.zeros((6, PAGE_PAD), F32)

    _to_row_tiles(h2buf.at[slot], h2)
    dst_vmem[...] = jnp.concatenate(
        dests + [jnp.zeros((8 - TOP_K, MIX_TOKENS), F32)], axis=0).astype(jnp.int32)
    pltpu.make_async_copy(dst_vmem, dst_smem, idx_sem).start()

    @pl.when(step == n_steps - 1)
    def _():
        pltpu.make_async_copy(dst_vmem, dst_smem, idx_sem).wait()
        for k in range(TOP_K):
            pltpu.make_async_copy(h2buf.at[prev], xs_ref.at[pl.ds(0, MIX_TOKENS * ROW_TILE), :],
                                  row_sem).wait()

        def row_body(it, carry):
            for j in range(ROWS_PER_ITER):
                r = it * ROWS_PER_ITER + j
                for k in range(TOP_K):
                    _row_copy(h2buf.at[slot], r, xs_ref, dst_smem[k, r],
                              row_sem).start(priority=k % 2)
            return carry

        lax.fori_loop(0, MIX_TOKENS // ROWS_PER_ITER, row_body, 0)
        for k in range(TOP_K):
            pltpu.make_async_copy(h2buf.at[slot], xs_ref.at[pl.ds(0, MIX_TOKENS * ROW_TILE), :],
                                  row_sem).wait()

        tail_vmem[0] = jnp.broadcast_to(cnt_new, (N_EXPERTS, LANES)).astype(jnp.int32)
        tail_vmem[1] = jnp.broadcast_to(page_scr[...], (N_EXPERTS, LANES)).astype(jnp.int32)
        tail_vmem[2] = jnp.broadcast_to(nfree_new, (N_EXPERTS, LANES)).astype(jnp.int32)
        cp = pltpu.make_async_copy(tail_vmem, tail_smem, idx_sem)
        cp.start()
        cp.wait()

        def zero_copy(first_row, rows):
            return pltpu.make_async_copy(
                zblk.at[pl.ds(0, rows * ROW_TILE), :],
                xs_ref.at[pl.ds(pl.multiple_of(first_row * ROW_TILE, ROW_TILE), rows * ROW_TILE), :],
                zsem)

        def tail_pass(wait):
            def body(e, carry):
                used = tail_smem[0, e, 0] & (ROW_BLOCK - 1)
                first = tail_smem[1, e, 0] * ROW_BLOCK + used
                left = jnp.where(used == 0, 0, ROW_BLOCK - used)
                rows = ZERO_ROWS
                while rows >= 1:
                    @pl.when((left & rows) != 0)
                    def _():
                        cp = zero_copy(first, rows)
                        cp.wait() if wait else cp.start()
                    first = first + jnp.where((left & rows) != 0, rows, 0)
                    rows //= 2
                return carry

            lax.fori_loop(0, N_EXPERTS, body, 0)

        def page_pass(wait):
            def body(p, carry):
                for half in range(ROW_BLOCK // ZERO_ROWS):
                    cp = zero_copy(p * ROW_BLOCK + half * ZERO_ROWS, ZERO_ROWS)
                    cp.wait() if wait else cp.start()
                return carry

            lax.fori_loop(tail_smem[2, 0, 0], n_pages, body, 0)

        tail_pass(False)
        page_pass(False)
        tail_pass(True)
        page_pass(True)


def _mixer(x2d, g1, w_in, lb, hg, pool_w, pool_scale, w_out, g2, wr_hi, wr_lo, br, seq, n_pages):
    n = x2d.shape[0]
    steps_per_seq = seq // MIX_TOKENS
    assert MIX_TOKENS <= ROW_BLOCK and n_pages <= PAGE_PAD
    spare_pages = MIX_TOKENS * TOP_K // ROW_BLOCK
    tabs_np, masks_np = _decay_tables(CHUNK)
    tabs = jnp.asarray(np.concatenate([tabs_np, tabs_np], axis=1), BF16)
    masks = jnp.asarray(masks_np, F32)
    tri = jnp.asarray(np.triu(np.ones((MIX_TOKENS, MIX_TOKENS), np.float32), 1), BF16)
    tri_e = jnp.asarray(np.tril(np.ones((N_EXPERTS, N_EXPERTS), np.float32), -1), BF16)

    def tok(b, s):
        return (b * steps_per_seq + s, 0)

    def tok_col(b, s):
        return (0, b * steps_per_seq + s)

    def const2(b, s):
        return (0, 0)

    def const3(b, s):
        return (0, 0, 0)

    in_specs = [
        pl.BlockSpec((MIX_TOKENS, D_MODEL), tok),
        pl.BlockSpec((1, D_MODEL), const2),
        pl.BlockSpec((D_MODEL, IN_COLS), const2),
        pl.BlockSpec((1, HGRN_WIDTH), const2),
        pl.BlockSpec((1, HEAD_DIM), const2),
        pl.BlockSpec((len(POOL_WINDOWS), POOL_GW, POOL_GW), const3),
        pl.BlockSpec((1, POOL_WIDTH), const2),
        pl.BlockSpec((D_MODEL, D_MODEL), const2),
        pl.BlockSpec((1, D_MODEL), const2),
        pl.BlockSpec((N_EXPERTS, D_MODEL), const2),
        pl.BlockSpec((N_EXPERTS, D_MODEL), const2),
        pl.BlockSpec((N_EXPERTS, 1), const2),
        pl.BlockSpec(tabs.shape, const2),
        pl.BlockSpec(masks.shape, const3),
        pl.BlockSpec(tri.shape, const2),
        pl.BlockSpec(tri_e.shape, const2),
    ]
    out_specs = [
        pl.BlockSpec((MIX_TOKENS, D_MODEL), tok),
        pl.BlockSpec((ROUTE_ROWS, MIX_TOKENS), tok_col),
        pl.BlockSpec((8, PAGE_PAD), const2),
        pl.BlockSpec(memory_space=pl.ANY),
    ]
    out_shape = [
        jax.ShapeDtypeStruct((n, D_MODEL), F32),
        jax.ShapeDtypeStruct((ROUTE_ROWS, n), F32),
        jax.ShapeDtypeStruct((8, PAGE_PAD), F32),
        jax.ShapeDtypeStruct(((n_pages + spare_pages) * ROW_BLOCK * ROW_TILE, LANES), F32),
    ]
    scratch = [
        pltpu.VMEM((MIX_TOKENS, IN_COLS), F32),
        pltpu.VMEM((MIX_TOKENS, D_MODEL), BF16),
        pltpu.VMEM((N_HEADS, HEAD_DIM, HEAD_DIM), F32),
        pltpu.VMEM((HALO + MIX_TOKENS, POOL_WIDTH), F32),
        pltpu.VMEM((N_EXPERTS, 1), F32),
        pltpu.VMEM((N_EXPERTS, 1), F32),
        pltpu.VMEM((N_EXPERTS, 1), F32),
        pltpu.VMEM((1, PAGE_PAD), F32),
        pltpu.VMEM((2, MIX_TOKENS * ROW_TILE, LANES), F32),
        pltpu.VMEM((8, MIX_TOKENS), jnp.int32),
        pltpu.SMEM((8, MIX_TOKENS), jnp.int32),
        pltpu.VMEM((3, N_EXPERTS, LANES), jnp.int32),
        pltpu.SMEM((3, N_EXPERTS, LANES), jnp.int32),
        pltpu.VMEM((ZERO_ROWS * ROW_TILE, LANES), F32),
        pltpu.SemaphoreType.DMA,
        pltpu.SemaphoreType.DMA,
        pltpu.SemaphoreType.DMA,
    ]
    return pl.pallas_call(
        functools.partial(_mixer_kernel, n_pages),
        grid=(n // seq, steps_per_seq),
        in_specs=in_specs,
        out_specs=out_specs,
        out_shape=out_shape,
        scratch_shapes=scratch,
        compiler_params=pltpu.CompilerParams(
            dimension_semantics=("arbitrary", "arbitrary"),
            vmem_limit_bytes=VMEM_LIMIT_BYTES),
        name="mixer",
    )(x2d, g1, w_in, lb, hg, pool_w, pool_scale, w_out, g2, wr_hi, wr_lo, br, tabs, masks, tri,
      tri_e)


def _expert_kernel(blk_e_ref, nused_ref, blk_page_ref, xs_ref, wg_ref, bg_ref, wu_ref, bu_ref,
                   wd_ref, bd_ref, ys_ref, wg_s, wu_s, wd_s):
    i = pl.program_id(0)
    active = i < nused_ref[0]
    changed = (i == 0) | (blk_e_ref[i] != blk_e_ref[jnp.maximum(i - 1, 0)])

    @pl.when(active & changed)
    def _():
        wg_s[...] = wg_ref[0].astype(BF16)
        wu_s[...] = wu_ref[0].astype(BF16)
        wd_s[...] = wd_ref[0].astype(BF16)

    @pl.when(active)
    def _():
        xb = _from_row_tiles(xs_ref, ROW_BLOCK).astype(BF16)
        gt = jnp.minimum(_dot(xb, wg_s[...]) + bg_ref[0], SWIGLU_LIMIT)
        up = jnp.clip(_dot(xb, wu_s[...]) + bu_ref[0], -SWIGLU_LIMIT, SWIGLU_LIMIT)
        act = (up + 1.0) * (gt * _sigmoid(SWIGLU_ALPHA * gt))
        _to_row_tiles(ys_ref, _dot(act.astype(BF16), wd_s[...]) + bd_ref[0])

    @pl.when(jnp.logical_not(active))
    def _():
        ys_ref[...] = jnp.zeros_like(ys_ref)


def _experts(blk_e, n_used, blk_page, xs, n_blocks, w_gate, b_gate, w_up, b_up, w_down, b_down):
    rows_spec_shape = (ROW_BLOCK * ROW_TILE, LANES)

    def blk(i, be, nu):
        return jnp.minimum(i, nu[0] - 1)

    def row_map(i, be, nu, bp):
        return (bp[blk(i, be, nu)], 0)

    def w_map(i, be, nu, bp):
        return (be[blk(i, be, nu)], 0, 0)

    w_spec = pl.BlockSpec((1, D_MODEL, D_MODEL), w_map)
    b_spec = pl.BlockSpec((1, 1, D_MODEL), w_map)
    grid_spec = pltpu.PrefetchScalarGridSpec(
        num_scalar_prefetch=3,
        grid=(n_blocks,),
        in_specs=[pl.BlockSpec(rows_spec_shape, row_map),
                  w_spec, b_spec, w_spec, b_spec, w_spec, b_spec],
        out_specs=pl.BlockSpec(rows_spec_shape, lambda i, be, nu, bp: (bp[i], 0)),
        scratch_shapes=[pltpu.VMEM((D_MODEL, D_MODEL), BF16)] * 3,
    )
    return pl.pallas_call(
        _expert_kernel,
        grid_spec=grid_spec,
        out_shape=jax.ShapeDtypeStruct((n_blocks * ROW_BLOCK * ROW_TILE, LANES), F32),
        compiler_params=pltpu.CompilerParams(
            dimension_semantics=("arbitrary",),
            vmem_limit_bytes=VMEM_LIMIT_BYTES),
        name="experts",
    )(blk_e, n_used, blk_page, xs, w_gate, b_gate[:, None, :], w_up, b_up[:, None, :],
      w_down, b_down[:, None, :])


def _combine_kernel(final, dest_ref, dest_next_ref, ys_ref, x1_ref, route_ref, gf_ref, out_ref,
                    buf, sems):
    i = pl.program_id(0)
    n_steps = pl.num_programs(0)
    slot = lax.rem(i, 2)

    def gather(idx_ref, s):
        def row_body(it, carry):
            for j in range(ROWS_PER_ITER):
                r = it * ROWS_PER_ITER + j
                for k in range(TOP_K):
                    pltpu.make_async_copy(ys_ref.at[_tile_rows(idx_ref[k * COMBINE_TOKENS + r]), :],
                                          buf.at[s, k, _tile_rows(r), :],
                                          sems.at[s]).start(priority=k % 2)
            return carry

        lax.fori_loop(0, COMBINE_TOKENS // ROWS_PER_ITER, row_body, 0)

    @pl.when(i == 0)
    def _():
        gather(dest_ref, 0)

    @pl.when(i + 1 < n_steps)
    def _():
        gather(dest_next_ref, 1 - slot)

    for k in range(TOP_K):
        pltpu.make_async_copy(ys_ref.at[pl.ds(0, COMBINE_TOKENS * ROW_TILE), :],
                              buf.at[slot, k], sems.at[slot]).wait()

    route = route_ref[...].T
    acc = x1_ref[...]
    for k in range(TOP_K):
        acc = acc + route[:, TOP_K + k:TOP_K + k + 1] * _from_row_tiles(
            buf.at[slot, k], COMBINE_TOKENS)
    out_ref[...] = _rms(acc, gf_ref[...]) if final else acc


def _combine(dest, ys, x1, route, gf, final):
    n = x1.shape[0]
    n_steps = n // COMBINE_TOKENS
    idx_block = (COMBINE_TOKENS * TOP_K,)
    return pl.pallas_call(
        functools.partial(_combine_kernel, final),
        grid=(n_steps,),
        in_specs=[
            pl.BlockSpec(idx_block, lambda i: (i,), memory_space=pltpu.SMEM),
            pl.BlockSpec(idx_block, lambda i: (jnp.minimum(i + 1, n_steps - 1),),
                         memory_space=pltpu.SMEM),
            pl.BlockSpec(memory_space=pl.ANY),
            pl.BlockSpec((COMBINE_TOKENS, D_MODEL), lambda i: (i, 0)),
            pl.BlockSpec((ROUTE_ROWS, COMBINE_TOKENS), lambda i: (0, i)),
            pl.BlockSpec((1, D_MODEL), lambda i: (0, 0)),
        ],
        out_specs=pl.BlockSpec((COMBINE_TOKENS, D_MODEL), lambda i: (i, 0)),
        out_shape=jax.ShapeDtypeStruct((n, D_MODEL), F32),
        scratch_shapes=[
            pltpu.VMEM((2, TOP_K, COMBINE_TOKENS * ROW_TILE, LANES), F32),
            pltpu.SemaphoreType.DMA((2,)),
        ],
        compiler_params=pltpu.CompilerParams(
            dimension_semantics=("arbitrary",),
            vmem_limit_bytes=VMEM_LIMIT_BYTES),
        name="combine",
    )(dest, dest, ys, x1, route, gf)


def kernel(x, norm1_g, w_in, hgrn_lb, hgrn_norm_g, pool_w, pool_scale, w_out, norm2_g,
           router_w, router_b, w_gate, b_gate, w_up, b_up, w_down, b_down, norm_f_g):
    depth = w_in.shape[0]
    batch, seq, _ = x.shape
    n = batch * seq
    n_pages = n * TOP_K // ROW_BLOCK + N_EXPERTS
    lb_all = jnp.cumsum(jax.nn.softmax(hgrn_lb.astype(F32), axis=0), axis=0)

    xt = x.reshape(n, D_MODEL)
    for l in range(depth):
        wr = router_w[l].T
        wr_hi = wr.astype(BF16)
        wr_lo = (wr - wr_hi.astype(F32)).astype(BF16)
        br = router_b[l][:, None]
        x1, route, meta, xs = _mixer(
            xt, norm1_g[l][None, :], w_in[l].astype(BF16), lb_all[l][None, :],
            hgrn_norm_g[l][None, :], pool_w[l].astype(BF16), pool_scale[l][None, :],
            w_out[l].astype(BF16), norm2_g[l][None, :], wr_hi, wr_lo, br, seq, n_pages)

        page_expert = meta[0, :n_pages].astype(jnp.int32)
        n_used = meta[1, 0:1].astype(jnp.int32)
        page_ids = jnp.arange(n_pages, dtype=jnp.int32)
        key = jnp.where(page_expert < 0, N_EXPERTS, page_expert) * PAGE_PAD + page_ids
        rank = jnp.sum((key[None, :] < key[:, None]).astype(jnp.int32), axis=1)
        at_step = rank[None, :] == page_ids[:, None]
        blk_page = jnp.sum(jnp.where(at_step, page_ids[None, :], 0), axis=1)
        blk_e = jnp.clip(jnp.sum(jnp.where(at_step, page_expert[None, :], 0), axis=1),
                         0, N_EXPERTS - 1)
        dest = route[0:TOP_K].astype(jnp.int32).reshape(TOP_K, n // COMBINE_TOKENS, COMBINE_TOKENS)
        dest = dest.transpose(1, 0, 2).reshape(n * TOP_K)

        ys = _experts(blk_e, n_used, blk_page, xs, n_pages, w_gate[l], b_gate[l], w_up[l],
                      b_up[l], w_down[l], b_down[l])
        xt = _combine(dest, ys, x1, route, norm_f_g[None, :], l == depth - 1)
    return xt.reshape(batch, seq, D_MODEL)
```

```python
import functools
import math

import numpy as np
import jax
import jax.numpy as jnp
from jax import lax
from jax.experimental import pallas as pl
from jax.experimental.pallas import tpu as pltpu

F32 = jnp.float32
BF16 = jnp.bfloat16

D_MODEL = 1024
HGRN_WIDTH = 512
HEAD_DIM = 128
N_HEADS = HGRN_WIDTH // HEAD_DIM
POOL_WIDTH = 512
POOL_WINDOWS = (2, 4, 8, 16)
POOL_GW = POOL_WIDTH // len(POOL_WINDOWS)
IN_COLS = 4 * HGRN_WIDTH + POOL_WIDTH
N_EXPERTS = 32
TOP_K = 4
SWIGLU_LIMIT = 7.0
SWIGLU_ALPHA = 1.702
NORM_EPS = 1e-6

LANES = 128
ROW_TILE = D_MODEL // LANES
VMEM_LIMIT_BYTES = 56 * 1024 * 1024

MIX_TOKENS = 512
CHUNK = 64
N_LEVELS = int(math.log2(CHUNK))
HALO = max(POOL_WINDOWS)
ROW_BLOCK = 512
ZERO_ROWS = ROW_BLOCK // 2
PAGE_PAD = 384
COMBINE_TOKENS = 512
ROWS_PER_ITER = 32
ROUTE_ROWS = 16
LOG2_E = 1.4426950408889634


def _dot(a, b):
    return jnp.dot(a, b, preferred_element_type=F32)


def _dot_nt(a, b):
    return lax.dot_general(a, b, (((1,), (1,)), ((), ())), preferred_element_type=F32)


def _dot_tn(a, b):
    return lax.dot_general(a, b, (((0,), (0,)), ((), ())), preferred_element_type=F32)


def _sigmoid(x):
    return 1.0 / (1.0 + jnp.exp2(x * (-LOG2_E)))


def _rms(x, g):
    return x * lax.rsqrt(jnp.mean(x * x, axis=-1, keepdims=True) + NORM_EPS) * g


def _split_bf16(x):
    hi = x.astype(BF16)
    lo = (x - hi.astype(F32)).astype(BF16)
    return hi, lo


def _to_row_tiles(ref, value):
    t = value.shape[0]
    for j in range(ROW_TILE):
        ref[pl.ds(j, t, stride=ROW_TILE), :] = value[:, j * LANES:(j + 1) * LANES]


def _from_row_tiles(ref, t):
    return jnp.concatenate(
        [ref[pl.ds(j, t, stride=ROW_TILE), :] for j in range(ROW_TILE)], axis=1)


def _tile_rows(row):
    return pl.ds(pl.multiple_of(row * ROW_TILE, ROW_TILE), ROW_TILE)


def _decay_tables(ch):
    t = np.arange(ch)[:, None]
    u = np.arange(ch)[None, :]
    mats = [u <= t, u > t]
    masks = []
    m = 1
    while m < ch:
        right = (t // m) % 2 == 1
        mid = (t // (2 * m)) * (2 * m) + m
        mats.append(np.where(right, (u >= mid) & (u <= t), (u > t) & (u <= mid - 1)))
        masks.append(right & ((u // m) % 2 == 0) & ((u // (2 * m)) == (t // (2 * m))))
        m *= 2
    return (np.concatenate(mats, 0).astype(np.float32),
            np.stack(masks).astype(np.float32))


def _hgrn_chunk(qp, fp, v, gp, lb, hg, tabs, masks_ref, st_ref):
    heads = [slice(h * HEAD_DIM, (h + 1) * HEAD_DIM) for h in range(N_HEADS)]
    q = qp * _sigmoid(qp)
    f = lb + (1.0 - lb) * _sigmoid(fp)
    lf2 = jnp.log(f) * LOG2_E
    k = 1.0 - f
    x = jnp.exp2(_dot(tabs, jnp.concatenate(_split_bf16(lf2), axis=0)))
    eb = x[0:CHUNK]
    esuf = x[CHUNK:2 * CHUNK]
    vb = v.astype(BF16)
    q_in = (q * eb).astype(BF16)
    k_end = (k * esuf).astype(BF16)
    qk = q * k
    st = [st_ref[h] for h in range(N_HEADS)]
    o = [_dot_nt(q_in[:, hs], st[h].astype(BF16)) for h, hs in enumerate(heads)]
    upd = [_dot_tn(vb[:, hs], k_end[:, hs]) for hs in heads]
    s = [jnp.zeros((CHUNK, CHUNK), F32) for _ in heads]
    row = lax.broadcasted_iota(jnp.int32, (CHUNK, HGRN_WIDTH), 0)
    for lvl in range(N_LEVELS):
        m = 1 << lvl
        if m % 8 == 0:
            qk_rows = jnp.concatenate(
                [(q if (r0 // m) % 2 else k)[r0:r0 + m] for r0 in range(0, CHUNK, m)], axis=0)
        else:
            qk_rows = jnp.where((row & m) != 0, q, k)
        z = (qk_rows * x[(2 + lvl) * CHUNK:(3 + lvl) * CHUNK]).astype(BF16)
        mask = masks_ref[lvl] != 0.0
        for h, hs in enumerate(heads):
            s[h] = jnp.where(mask, _dot_nt(z[:, hs], z[:, hs]), s[h])
    outs = []
    for h, hs in enumerate(heads):
        oh = o[h] + jnp.sum(qk[:, hs], axis=-1, keepdims=True) * v[:, hs]
        oh = oh + _dot(s[h].astype(BF16), vb[:, hs])
        st_ref[h] = eb[CHUNK - 1:CHUNK, hs] * st[h] + upd[h]
        oh = oh * lax.rsqrt(jnp.mean(oh * oh, axis=-1, keepdims=True) + NORM_EPS) * hg
        outs.append(oh)
    return jnp.concatenate(outs, axis=1) * (gp * _sigmoid(gp))


def _row_copy(src, src_row, dst, dst_row, sem):
    return pltpu.make_async_copy(src.at[_tile_rows(src_row), :], dst.at[_tile_rows(dst_row), :], sem)


def _mixer_kernel(n_pages,
                  x_ref, g1_ref, win_ref, lb_ref, hg_ref, pw_ref, ps_ref, wout_ref, g2_ref,
                  wrh_ref, wrl_ref, br_ref, tabs_ref, masks_ref, tri_ref, trie_ref,
                  x1_ref, route_ref, meta_ref, xs_ref,
                  proj_scr, mix_scr, st_scr, ubuf_scr, cnt_scr, page_scr, nfree_scr, pexp_scr,
                  h2buf, dst_vmem, dst_smem, tail_vmem, tail_smem, zblk, row_sem, idx_sem, zsem):
    b_i = pl.program_id(0)
    s_i = pl.program_id(1)
    step = b_i * pl.num_programs(1) + s_i
    n_steps = pl.num_programs(0) * pl.num_programs(1)
    slot = lax.rem(step, 2)
    prev = 1 - slot

    @pl.when(s_i == 0)
    def _():
        st_scr[...] = jnp.zeros_like(st_scr)
        ubuf_scr[0:HALO, :] = jnp.zeros((HALO, POOL_WIDTH), F32)

    @pl.when(step == 0)
    def _():
        cnt_scr[...] = jnp.zeros_like(cnt_scr)
        page_scr[...] = jnp.zeros_like(page_scr)
        nfree_scr[...] = jnp.zeros_like(nfree_scr)
        pexp_scr[...] = jnp.full(pexp_scr.shape, -1.0, F32)
        zblk[...] = jnp.zeros_like(zblk)
        h2buf[1] = jnp.zeros(h2buf.shape[1:], F32)

        def spare_body(r, carry):
            for k in range(TOP_K):
                dst_smem[k, r] = n_pages * ROW_BLOCK + r * TOP_K + k
            return carry

        lax.fori_loop(0, MIX_TOKENS, spare_body, 0)

    @pl.when(step > 0)
    def _():
        pltpu.make_async_copy(dst_vmem, dst_smem, idx_sem).wait()
        for k in range(TOP_K):
            pltpu.make_async_copy(h2buf.at[slot], xs_ref.at[pl.ds(0, MIX_TOKENS * ROW_TILE), :],
                                  row_sem).wait()

    @pl.when(step >= 0)
    def _():
        for r in range(MIX_TOKENS):
            for k in range(TOP_K):
                _row_copy(h2buf.at[prev], r, xs_ref, dst_smem[k, r], row_sem).start(priority=k % 2)
        h = _rms(x_ref[...], g1_ref[...]).astype(BF16)
        proj_scr[...] = _dot(h, win_ref[...])

    x = x_ref[...]

    tabs = tabs_ref[...]

    for c in range(MIX_TOKENS // CHUNK):
        rows = slice(c * CHUNK, (c + 1) * CHUNK)
        out = _hgrn_chunk(
            proj_scr[rows, 0:HGRN_WIDTH],
            proj_scr[rows, HGRN_WIDTH:2 * HGRN_WIDTH],
            proj_scr[rows, 2 * HGRN_WIDTH:3 * HGRN_WIDTH],
            proj_scr[rows, 3 * HGRN_WIDTH:4 * HGRN_WIDTH],
            lb_ref[...], hg_ref[...], tabs, masks_ref, st_scr)
        mix_scr[rows, 0:HGRN_WIDTH] = out.astype(BF16)

    ubuf_scr[HALO:HALO + MIX_TOKENS, :] = proj_scr[:, 4 * HGRN_WIDTH:IN_COLS]
    pos = (s_i * MIX_TOKENS + 1
           + lax.broadcasted_iota(jnp.int32, (MIX_TOKENS, 1), 0)).astype(F32)
    for g, win in enumerate(POOL_WINDOWS):
        lo = g * POOL_GW
        ext = ubuf_scr[:, lo:lo + POOL_GW]
        acc = ext
        span = 1
        while span < win:
            acc = acc + pltpu.roll(acc, span, axis=0)
            span *= 2
        u = ext[HALO:]
        d = acc[HALO:] / jnp.minimum(pos, float(win)) - u
        y = _dot(d.astype(BF16), pw_ref[g]) * ps_ref[:, lo:lo + POOL_GW]
        mix_scr[:, HGRN_WIDTH + lo:HGRN_WIDTH + lo + POOL_GW] = y.astype(BF16)
    ubuf_scr[0:HALO, :] = ubuf_scr[MIX_TOKENS:MIX_TOKENS + HALO, :]

    x1 = x + _dot(mix_scr[...], wout_ref[...])
    x1_ref[...] = x1
    h2 = _rms(x1, g2_ref[...])

    h_hi, h_lo = _split_bf16(h2)
    wrh = wrh_ref[...]
    logits = (_dot_nt(wrh, h_hi) + _dot_nt(wrh, h_lo) + _dot_nt(wrl_ref[...], h_hi)
              + br_ref[...])
    eidx = lax.broadcasted_iota(jnp.int32, (N_EXPERTS, MIX_TOKENS), 0)
    work = logits
    tops, sels = [], []
    for _ in range(TOP_K):
        m = jnp.max(work, axis=0, keepdims=True)
        idx = jnp.min(jnp.where(work == m, eidx, N_EXPERTS), axis=0, keepdims=True)
        sel = eidx == idx
        work = jnp.where(sel, -jnp.inf, work)
        tops.append(m)
        sels.append(sel)
    exps = [jnp.exp(m - tops[0]) for m in tops]
    denom = exps[0] + exps[1] + exps[2] + exps[3]
    gates = [e / denom for e in exps]

    sel_any = jnp.zeros((N_EXPERTS, MIX_TOKENS), F32)
    for sel in sels:
        sel_any = jnp.where(sel, 1.0, sel_any)
    cnt = cnt_scr[...]
    before = _dot(sel_any.astype(BF16), tri_ref[...]) + cnt
    cnt_new = cnt + jnp.sum(sel_any, axis=1, keepdims=True)
    cnt_scr[...] = cnt_new

    shift = int(math.log2(ROW_BLOCK))
    before_i = before.astype(jnp.int32)
    cnt_i = cnt.astype(jnp.int32)
    page_lo = cnt_i >> shift
    starts_page = (cnt_i & (ROW_BLOCK - 1)) == 0
    page_hi = (cnt_new.astype(jnp.int32) - 1) >> shift
    need = (cnt_new > cnt) & (starts_page | (page_hi > page_lo))
    need_f = jnp.where(need, 1.0, 0.0)
    earlier = _dot(trie_ref[...], jnp.broadcast_to(need_f, (N_EXPERTS, LANES)).astype(BF16))
    nfree = nfree_scr[...]
    new_id = nfree + earlier[:, 0:1]
    new_idx = jnp.where(starts_page, page_lo, page_lo + 1)
    cur_page = page_scr[...]
    page = jnp.where(need & ((before_i >> shift) == new_idx), new_id, cur_page)
    dest_all = page * float(ROW_BLOCK) + (before_i & (ROW_BLOCK - 1)).astype(F32)
    page_scr[...] = jnp.where(need, new_id, cur_page)
    nfree_new = nfree + jnp.sum(need_f, axis=0, keepdims=True)
    nfree_scr[...] = nfree_new
    page_lane = lax.broadcasted_iota(jnp.int32, (N_EXPERTS, PAGE_PAD), 1).astype(F32)
    owner = jnp.where(need & (page_lane == new_id), eidx[:, 0:1].astype(F32), -1.0)
    pexp = jnp.maximum(pexp_scr[...], jnp.max(owner, axis=0, keepdims=True))
    pexp_scr[...] = pexp

    dests = []
    for k in range(TOP_K):
        dest = jnp.sum(jnp.where(sels[k], dest_all, 0.0), axis=0, keepdims=True)
        dests.append(dest)
        route_ref[k:k + 1, :] = dest
        route_ref[TOP_K + k:TOP_K + k + 1, :] = gates[k]
    route_ref[2 * TOP_K:, :] = jnp.zeros((ROUTE_ROWS - 2 * TOP_K, MIX_TOKENS), F32)
    meta_ref[0:1, :] = pexp
    meta_ref[1:2, :] = jnp.broadcast_to(nfree_new[0:1, :], (1, PAGE_PAD))
    meta_ref[2:, :] = jnp.zeros((6, PAGE_PAD), F32)

    _to_row_tiles(h2buf.at[slot], h2)
    dst_vmem[...] = jnp.concatenate(
        dests + [jnp.zeros((8 - TOP_K, MIX_TOKENS), F32)], axis=0).astype(jnp.int32)
    pltpu.make_async_copy(dst_vmem, dst_smem, idx_sem).start()

    @pl.when(step == n_steps - 1)
    def _():
        pltpu.make_async_copy(dst_vmem, dst_smem, idx_sem).wait()
        for k in range(TOP_K):
            pltpu.make_async_copy(h2buf.at[prev], xs_ref.at[pl.ds(0, MIX_TOKENS * ROW_TILE), :],
                                  row_sem).wait()

        def row_body(it, carry):
            for j in range(ROWS_PER_ITER):
                r = it * ROWS_PER_ITER + j
                for k in range(TOP_K):
                    _row_copy(h2buf.at[slot], r, xs_ref, dst_smem[k, r],
                              row_sem).start(priority=k % 2)
            return carry

        lax.fori_loop(0, MIX_TOKENS // ROWS_PER_ITER, row_body, 0)
        for k in range(TOP_K):
            pltpu.make_async_copy(h2buf.at[slot], xs_ref.at[pl.ds(0, MIX_TOKENS * ROW_TILE), :],
                                  row_sem).wait()

        tail_vmem[0] = jnp.broadcast_to(cnt_new, (N_EXPERTS, LANES)).astype(jnp.int32)
        tail_vmem[1] = jnp.broadcast_to(page_scr[...], (N_EXPERTS, LANES)).astype(jnp.int32)
        tail_vmem[2] = jnp.broadcast_to(nfree_new, (N_EXPERTS, LANES)).astype(jnp.int32)
        cp = pltpu.make_async_copy(tail_vmem, tail_smem, idx_sem)
        cp.start()
        cp.wait()

        def zero_copy(first_row, rows):
            return pltpu.make_async_copy(
                zblk.at[pl.ds(0, rows * ROW_TILE), :],
                xs_ref.at[pl.ds(pl.multiple_of(first_row * ROW_TILE, ROW_TILE), rows * ROW_TILE), :],
                zsem)

        def tail_pass(wait):
            def body(e, carry):
                used = tail_smem[0, e, 0] & (ROW_BLOCK - 1)
                first = tail_smem[1, e, 0] * ROW_BLOCK + used
                left = jnp.where(used == 0, 0, ROW_BLOCK - used)
                rows = ZERO_ROWS
                while rows >= 1:
                    @pl.when((left & rows) != 0)
                    def _():
                        cp = zero_copy(first, rows)
                        cp.wait() if wait else cp.start()
                    first = first + jnp.where((left & rows) != 0, rows, 0)
                    rows //= 2
                return carry

            lax.fori_loop(0, N_EXPERTS, body, 0)

        def page_pass(wait):
            def body(p, carry):
                for half in range(ROW_BLOCK // ZERO_ROWS):
                    cp = zero_copy(p * ROW_BLOCK + half * ZERO_ROWS, ZERO_ROWS)
                    cp.wait() if wait else cp.start()
                return carry

            lax.fori_loop(tail_smem[2, 0, 0], n_pages, body, 0)

        tail_pass(False)
        page_pass(False)
        tail_pass(True)
        page_pass(True)


def _mixer(x2d, g1, w_in, lb, hg, pool_w, pool_scale, w_out, g2, wr_hi, wr_lo, br, seq, n_pages):
    n = x2d.shape[0]
    steps_per_seq = seq // MIX_TOKENS
    assert MIX_TOKENS <= ROW_BLOCK and n_pages <= PAGE_PAD
    spare_pages = MIX_TOKENS * TOP_K // ROW_BLOCK
    tabs_np, masks_np = _decay_tables(CHUNK)
    tabs = jnp.asarray(np.concatenate([tabs_np, tabs_np], axis=1), BF16)
    masks = jnp.asarray(masks_np, F32)
    tri = jnp.asarray(np.triu(np.ones((MIX_TOKENS, MIX_TOKENS), np.float32), 1), BF16)
    tri_e = jnp.asarray(np.tril(np.ones((N_EXPERTS, N_EXPERTS), np.float32), -1), BF16)

    def tok(b, s):
        return (b * steps_per_seq + s, 0)

    def tok_col(b, s):
        return (0, b * steps_per_seq + s)

    def const2(b, s):
        return (0, 0)

    def const3(b, s):
        return (0, 0, 0)

    in_specs = [
        pl.BlockSpec((MIX_TOKENS, D_MODEL), tok),
        pl.BlockSpec((1, D_MODEL), const2),
        pl.BlockSpec((D_MODEL, IN_COLS), const2),
        pl.BlockSpec((1, HGRN_WIDTH), const2),
        pl.BlockSpec((1, HEAD_DIM), const2),
        pl.BlockSpec((len(POOL_WINDOWS), POOL_GW, POOL_GW), const3),
        pl.BlockSpec((1, POOL_WIDTH), const2),
        pl.BlockSpec((D_MODEL, D_MODEL), const2),
        pl.BlockSpec((1, D_MODEL), const2),
        pl.BlockSpec((N_EXPERTS, D_MODEL), const2),
        pl.BlockSpec((N_EXPERTS, D_MODEL), const2),
        pl.BlockSpec((N_EXPERTS, 1), const2),
        pl.BlockSpec(tabs.shape, const2),
        pl.BlockSpec(masks.shape, const3),
        pl.BlockSpec(tri.shape, const2),
        pl.BlockSpec(tri_e.shape, const2),
    ]
    out_specs = [
        pl.BlockSpec((MIX_TOKENS, D_MODEL), tok),
        pl.BlockSpec((ROUTE_ROWS, MIX_TOKENS), tok_col),
        pl.BlockSpec((8, PAGE_PAD), const2),
        pl.BlockSpec(memory_space=pl.ANY),
    ]
    out_shape = [
        jax.ShapeDtypeStruct((n, D_MODEL), F32),
        jax.ShapeDtypeStruct((ROUTE_ROWS, n), F32),
        jax.ShapeDtypeStruct((8, PAGE_PAD), F32),
        jax.ShapeDtypeStruct(((n_pages + spare_pages) * ROW_BLOCK * ROW_TILE, LANES), F32),
    ]
    scratch = [
        pltpu.VMEM((MIX_TOKENS, IN_COLS), F32),
        pltpu.VMEM((MIX_TOKENS, D_MODEL), BF16),
        pltpu.VMEM((N_HEADS, HEAD_DIM, HEAD_DIM), F32),
        pltpu.VMEM((HALO + MIX_TOKENS, POOL_WIDTH), F32),
        pltpu.VMEM((N_EXPERTS, 1), F32),
        pltpu.VMEM((N_EXPERTS, 1), F32),
        pltpu.VMEM((N_EXPERTS, 1), F32),
        pltpu.VMEM((1, PAGE_PAD), F32),
        pltpu.VMEM((2, MIX_TOKENS * ROW_TILE, LANES), F32),
        pltpu.VMEM((8, MIX_TOKENS), jnp.int32),
        pltpu.SMEM((8, MIX_TOKENS), jnp.int32),
        pltpu.VMEM((3, N_EXPERTS, LANES), jnp.int32),
        pltpu.SMEM((3, N_EXPERTS, LANES), jnp.int32),
        pltpu.VMEM((ZERO_ROWS * ROW_TILE, LANES), F32),
        pltpu.SemaphoreType.DMA,
        pltpu.SemaphoreType.DMA,
        pltpu.SemaphoreType.DMA,
    ]
    return pl.pallas_call(
        functools.partial(_mixer_kernel, n_pages),
        grid=(n // seq, steps_per_seq),
        in_specs=in_specs,
        out_specs=out_specs,
        out_shape=out_shape,
        scratch_shapes=scratch,
        compiler_params=pltpu.CompilerParams(
            dimension_semantics=("arbitrary", "arbitrary"),
            vmem_limit_bytes=VMEM_LIMIT_BYTES),
        name="mixer",
    )(x2d, g1, w_in, lb, hg, pool_w, pool_scale, w_out, g2, wr_hi, wr_lo, br, tabs, masks, tri,
      tri_e)


def _expert_kernel(blk_e_ref, nused_ref, blk_page_ref, xs_ref, wg_ref, bg_ref, wu_ref, bu_ref,
                   wd_ref, bd_ref, ys_ref, wg_s, wu_s, wd_s):
    i = pl.program_id(0)
    active = i < nused_ref[0]
    changed = (i == 0) | (blk_e_ref[i] != blk_e_ref[jnp.maximum(i - 1, 0)])

    @pl.when(active & changed)
    def _():
        wg_s[...] = wg_ref[0].astype(BF16)
        wu_s[...] = wu_ref[0].astype(BF16)
        wd_s[...] = wd_ref[0].astype(BF16)

    @pl.when(active)
    def _():
        xb = _from_row_tiles(xs_ref, ROW_BLOCK).astype(BF16)
        gt = jnp.minimum(_dot(xb, wg_s[...]) + bg_ref[0], SWIGLU_LIMIT)
        up = jnp.clip(_dot(xb, wu_s[...]) + bu_ref[0], -SWIGLU_LIMIT, SWIGLU_LIMIT)
        act = (up + 1.0) * (gt * _sigmoid(SWIGLU_ALPHA * gt))
        _to_row_tiles(ys_ref, _dot(act.astype(BF16), wd_s[...]) + bd_ref[0])

    @pl.when(jnp.logical_not(active))
    def _():
        ys_ref[...] = jnp.zeros_like(ys_ref)


def _experts(blk_e, n_used, blk_page, xs, n_blocks, w_gate, b_gate, w_up, b_up, w_down, b_down):
    rows_spec_shape = (ROW_BLOCK * ROW_TILE, LANES)

    def blk(i, be, nu):
        return jnp.minimum(i, nu[0] - 1)

    def row_map(i, be, nu, bp):
        return (bp[blk(i, be, nu)], 0)

    def w_map(i, be, nu, bp):
        return (be[blk(i, be, nu)], 0, 0)

    w_spec = pl.BlockSpec((1, D_MODEL, D_MODEL), w_map)
    b_spec = pl.BlockSpec((1, 1, D_MODEL), w_map)
    grid_spec = pltpu.PrefetchScalarGridSpec(
        num_scalar_prefetch=3,
        grid=(n_blocks,),
        in_specs=[pl.BlockSpec(rows_spec_shape, row_map),
                  w_spec, b_spec, w_spec, b_spec, w_spec, b_spec],
        out_specs=pl.BlockSpec(rows_spec_shape, lambda i, be, nu, bp: (bp[i], 0)),
        scratch_shapes=[pltpu.VMEM((D_MODEL, D_MODEL), BF16)] * 3,
    )
    return pl.pallas_call(
        _expert_kernel,
        grid_spec=grid_spec,
        out_shape=jax.ShapeDtypeStruct((n_blocks * ROW_BLOCK * ROW_TILE, LANES), F32),
        compiler_params=pltpu.CompilerParams(
            dimension_semantics=("arbitrary",),
            vmem_limit_bytes=VMEM_LIMIT_BYTES),
        name="experts",
    )(blk_e, n_used, blk_page, xs, w_gate, b_gate[:, None, :], w_up, b_up[:, None, :],
      w_down, b_down[:, None, :])


def _combine_kernel(final, dest_ref, dest_next_ref, ys_ref, x1_ref, route_ref, gf_ref, out_ref,
                    buf, sems):
    i = pl.program_id(0)
    n_steps = pl.num_programs(0)
    slot = lax.rem(i, 2)

    def gather(idx_ref, s):
        def row_body(it, carry):
            for j in range(ROWS_PER_ITER):
                r = it * ROWS_PER_ITER + j
                for k in range(TOP_K):
                    pltpu.make_async_copy(ys_ref.at[_tile_rows(idx_ref[k * COMBINE_TOKENS + r]), :],
                                          buf.at[s, k, _tile_rows(r), :],
                                          sems.at[s]).start(priority=k % 2)
            return carry

        lax.fori_loop(0, COMBINE_TOKENS // ROWS_PER_ITER, row_body, 0)

    @pl.when(i == 0)
    def _():
        gather(dest_ref, 0)

    @pl.when(i + 1 < n_steps)
    def _():
        gather(dest_next_ref, 1 - slot)

    for k in range(TOP_K):
        pltpu.make_async_copy(ys_ref.at[pl.ds(0, COMBINE_TOKENS * ROW_TILE), :],
                              buf.at[slot, k], sems.at[slot]).wait()

    route = route_ref[...].T
    acc = x1_ref[...]
    for k in range(TOP_K):
        acc = acc + route[:, TOP_K + k:TOP_K + k + 1] * _from_row_tiles(
            buf.at[slot, k], COMBINE_TOKENS)
    out_ref[...] = _rms(acc, gf_ref[...]) if final else acc


def _combine(dest, ys, x1, route, gf, final):
    n = x1.shape[0]
    n_steps = n // COMBINE_TOKENS
    idx_block = (COMBINE_TOKENS * TOP_K,)
    return pl.pallas_call(
        functools.partial(_combine_kernel, final),
        grid=(n_steps,),
        in_specs=[
            pl.BlockSpec(idx_block, lambda i: (i,), memory_space=pltpu.SMEM),
            pl.BlockSpec(idx_block, lambda i: (jnp.minimum(i + 1, n_steps - 1),),
                         memory_space=pltpu.SMEM),
            pl.BlockSpec(memory_space=pl.ANY),
            pl.BlockSpec((COMBINE_TOKENS, D_MODEL), lambda i: (i, 0)),
            pl.BlockSpec((ROUTE_ROWS, COMBINE_TOKENS), lambda i: (0, i)),
            pl.BlockSpec((1, D_MODEL), lambda i: (0, 0)),
        ],
        out_specs=pl.BlockSpec((COMBINE_TOKENS, D_MODEL), lambda i: (i, 0)),
        out_shape=jax.ShapeDtypeStruct((n, D_MODEL), F32),
        scratch_shapes=[
            pltpu.VMEM((2, TOP_K, COMBINE_TOKENS * ROW_TILE, LANES), F32),
            pltpu.SemaphoreType.DMA((2,)),
        ],
        compiler_params=pltpu.CompilerParams(
            dimension_semantics=("arbitrary",),
            vmem_limit_bytes=VMEM_LIMIT_BYTES),
        name="combine",
    )(dest, dest, ys, x1, route, gf)


def kernel(x, norm1_g, w_in, hgrn_lb, hgrn_norm_g, pool_w, pool_scale, w_out, norm2_g,
           router_w, router_b, w_gate, b_gate, w_up, b_up, w_down, b_down, norm_f_g):
    depth = w_in.shape[0]
    batch, seq, _ = x.shape
    n = batch * seq
    n_pages = n * TOP_K // ROW_BLOCK + N_EXPERTS
    lb_all = jnp.cumsum(jax.nn.softmax(hgrn_lb.astype(F32), axis=0), axis=0)

    xt = x.reshape(n, D_MODEL)
    for l in range(depth):
        wr = router_w[l].T
        wr_hi = wr.astype(BF16)
        wr_lo = (wr - wr_hi.astype(F32)).astype(BF16)
        br = router_b[l][:, None]
        x1, route, meta, xs = _mixer(
            xt, norm1_g[l][None, :], w_in[l].astype(BF16), lb_all[l][None, :],
            hgrn_norm_g[l][None, :], pool_w[l].astype(BF16), pool_scale[l][None, :],
            w_out[l].astype(BF16), norm2_g[l][None, :], wr_hi, wr_lo, br, seq, n_pages)

        page_expert = meta[0, :n_pages].astype(jnp.int32)
        n_used = meta[1, 0:1].astype(jnp.int32)
        page_ids = jnp.arange(n_pages, dtype=jnp.int32)
        key = jnp.where(page_expert < 0, N_EXPERTS, page_expert) * PAGE_PAD + page_ids
        rank = jnp.sum((key[None, :] < key[:, None]).astype(jnp.int32), axis=1)
        at_step = rank[None, :] == page_ids[:, None]
        blk_page = jnp.sum(jnp.where(at_step, page_ids[None, :], 0), axis=1)
        blk_e = jnp.clip(jnp.sum(jnp.where(at_step, page_expert[None, :], 0), axis=1),
                         0, N_EXPERTS - 1)
        dest = route[0:TOP_K].astype(jnp.int32).reshape(TOP_K, n // COMBINE_TOKENS, COMBINE_TOKENS)
        dest = dest.transpose(1, 0, 2).reshape(n * TOP_K)

        ys = _experts(blk_e, n_used, blk_page, xs, n_pages, w_gate[l], b_gate[l], w_up[l],
                      b_up[l], w_down[l], b_down[l])
        xt = _combine(dest, ys, x1, route, norm_f_g[None, :], l == depth - 1)
    return xt.reshape(batch, seq, D_MODEL)
```

```python
import functools
import math

import numpy as np
import jax
import jax.numpy as jnp
from jax import lax
from jax.experimental import pallas as pl
from jax.experimental.pallas import tpu as pltpu

F32 = jnp.float32
BF16 = jnp.bfloat16

D_MODEL = 1024
HGRN_WIDTH = 512
HEAD_DIM = 128
N_HEADS = HGRN_WIDTH // HEAD_DIM
POOL_WIDTH = 512
POOL_WINDOWS = (2, 4, 8, 16)
POOL_GW = POOL_WIDTH // len(POOL_WINDOWS)
IN_COLS = 4 * HGRN_WIDTH + POOL_WIDTH
N_EXPERTS = 32
TOP_K = 4
SWIGLU_LIMIT = 7.0
SWIGLU_ALPHA = 1.702
NORM_EPS = 1e-6

LANES = 128
ROW_TILE = D_MODEL // LANES
VMEM_LIMIT_BYTES = 56 * 1024 * 1024

MIX_TOKENS = 512
CHUNK = 64
N_LEVELS = int(math.log2(CHUNK))
HALO = max(POOL_WINDOWS)
FIRST_COPY_TOKENS = 320
ROW_BLOCK = 512
ZERO_ROWS = ROW_BLOCK // 2
PAGE_PAD = 384
COMBINE_TOKENS = 512
ROWS_PER_ITER = 32
ROUTE_ROWS = 16
LOG2_E = 1.4426950408889634


def _dot(a, b):
    return jnp.dot(a, b, preferred_element_type=F32)


def _dot_nt(a, b):
    return lax.dot_general(a, b, (((1,), (1,)), ((), ())), preferred_element_type=F32)


def _dot_tn(a, b):
    return lax.dot_general(a, b, (((0,), (0,)), ((), ())), preferred_element_type=F32)


def _sigmoid(x):
    return 1.0 / (1.0 + jnp.exp2(x * (-LOG2_E)))


def _rms(x, g):
    return x * lax.rsqrt(jnp.mean(x * x, axis=-1, keepdims=True) + NORM_EPS) * g


def _split_bf16(x):
    hi = x.astype(BF16)
    lo = (x - hi.astype(F32)).astype(BF16)
    return hi, lo


def _to_row_tiles(ref, value):
    t = value.shape[0]
    for j in range(ROW_TILE):
        ref[pl.ds(j, t, stride=ROW_TILE), :] = value[:, j * LANES:(j + 1) * LANES]


def _from_row_tiles(ref, t):
    return jnp.concatenate(
        [ref[pl.ds(j, t, stride=ROW_TILE), :] for j in range(ROW_TILE)], axis=1)


def _tile_rows(row):
    return pl.ds(pl.multiple_of(row * ROW_TILE, ROW_TILE), ROW_TILE)


def _decay_tables(ch):
    t = np.arange(ch)[:, None]
    u = np.arange(ch)[None, :]
    mats = [u <= t, u > t]
    masks = []
    m = 1
    while m < ch:
        right = (t // m) % 2 == 1
        mid = (t // (2 * m)) * (2 * m) + m
        mats.append(np.where(right, (u >= mid) & (u <= t), (u > t) & (u <= mid - 1)))
        masks.append(right & ((u // m) % 2 == 0) & ((u // (2 * m)) == (t // (2 * m))))
        m *= 2
    return (np.concatenate(mats, 0).astype(np.float32),
            np.stack(masks).astype(np.float32))


def _hgrn_chunk(qp, fp, v, gp, lb, hg, tabs, masks_ref, st_ref):
    heads = [slice(h * HEAD_DIM, (h + 1) * HEAD_DIM) for h in range(N_HEADS)]
    q = qp * _sigmoid(qp)
    f = lb + (1.0 - lb) * _sigmoid(fp)
    lf2 = jnp.log(f) * LOG2_E
    k = 1.0 - f
    x = jnp.exp2(_dot(tabs, jnp.concatenate(_split_bf16(lf2), axis=0)))
    eb = x[0:CHUNK]
    esuf = x[CHUNK:2 * CHUNK]
    vb = v.astype(BF16)
    q_in = (q * eb).astype(BF16)
    k_end = (k * esuf).astype(BF16)
    qk = q * k
    st = [st_ref[h] for h in range(N_HEADS)]
    o = [_dot_nt(q_in[:, hs], st[h].astype(BF16)) for h, hs in enumerate(heads)]
    upd = [_dot_tn(vb[:, hs], k_end[:, hs]) for hs in heads]
    s = [jnp.zeros((CHUNK, CHUNK), F32) for _ in heads]
    row = lax.broadcasted_iota(jnp.int32, (CHUNK, HGRN_WIDTH), 0)
    for lvl in range(N_LEVELS):
        m = 1 << lvl
        if m % 8 == 0:
            qk_rows = jnp.concatenate(
                [(q if (r0 // m) % 2 else k)[r0:r0 + m] for r0 in range(0, CHUNK, m)], axis=0)
        else:
            qk_rows = jnp.where((row & m) != 0, q, k)
        z = (qk_rows * x[(2 + lvl) * CHUNK:(3 + lvl) * CHUNK]).astype(BF16)
        mask = masks_ref[lvl] != 0.0
        for h, hs in enumerate(heads):
            s[h] = jnp.where(mask, _dot_nt(z[:, hs], z[:, hs]), s[h])
    outs = []
    for h, hs in enumerate(heads):
        oh = o[h] + jnp.sum(qk[:, hs], axis=-1, keepdims=True) * v[:, hs]
        oh = oh + _dot(s[h].astype(BF16), vb[:, hs])
        st_ref[h] = eb[CHUNK - 1:CHUNK, hs] * st[h] + upd[h]
        oh = oh * lax.rsqrt(jnp.mean(oh * oh, axis=-1, keepdims=True) + NORM_EPS) * hg
        outs.append(oh)
    return jnp.concatenate(outs, axis=1) * (gp * _sigmoid(gp))


def _row_copy(src, src_row, dst, dst_row, sem):
    return pltpu.make_async_copy(src.at[_tile_rows(src_row), :], dst.at[_tile_rows(dst_row), :], sem)


def _mixer_kernel(n_pages,
                  x_ref, g1_ref, win_ref, lb_ref, hg_ref, pw_ref, ps_ref, wout_ref, g2_ref,
                  wrh_ref, wrl_ref, br_ref, tabs_ref, masks_ref, tri_ref, trie_ref,
                  x1_ref, route_ref, meta_ref, xs_ref,
                  proj_scr, mix_scr, st_scr, ubuf_scr, cnt_scr, page_scr, nfree_scr, pexp_scr,
                  h2buf, dst_vmem, dst_smem, tail_vmem, tail_smem, zblk, row_sem, idx_sem, zsem):
    b_i = pl.program_id(0)
    s_i = pl.program_id(1)
    step = b_i * pl.num_programs(1) + s_i
    n_steps = pl.num_programs(0) * pl.num_programs(1)
    slot = lax.rem(step, 2)
    prev = 1 - slot

    @pl.when(s_i == 0)
    def _():
        st_scr[...] = jnp.zeros_like(st_scr)
        ubuf_scr[0:HALO, :] = jnp.zeros((HALO, POOL_WIDTH), F32)

    @pl.when(step == 0)
    def _():
        cnt_scr[...] = jnp.zeros_like(cnt_scr)
        page_scr[...] = jnp.zeros_like(page_scr)
        nfree_scr[...] = jnp.zeros_like(nfree_scr)
        pexp_scr[...] = jnp.full(pexp_scr.shape, -1.0, F32)
        zblk[...] = jnp.zeros_like(zblk)
        h2buf[1] = jnp.zeros(h2buf.shape[1:], F32)

        def spare_body(r, carry):
            for k in range(TOP_K):
                dst_smem[k, r] = n_pages * ROW_BLOCK + r * TOP_K + k
            return carry

        lax.fori_loop(0, MIX_TOKENS, spare_body, 0)

    @pl.when(step > 0)
    def _():
        pltpu.make_async_copy(dst_vmem, dst_smem, idx_sem).wait()
        for k in range(TOP_K):
            pltpu.make_async_copy(h2buf.at[slot], xs_ref.at[pl.ds(0, MIX_TOKENS * ROW_TILE), :],
                                  row_sem).wait()

    @pl.when(step >= 0)
    def _():
        for r in range(FIRST_COPY_TOKENS):
            for k in range(TOP_K):
                _row_copy(h2buf.at[prev], r, xs_ref, dst_smem[k, r], row_sem).start(priority=k % 2)
        h = _rms(x_ref[...], g1_ref[...]).astype(BF16)
        proj_scr[...] = _dot(h, win_ref[...])

    tabs = tabs_ref[...]

    for c in range(MIX_TOKENS // CHUNK):
        rows = slice(c * CHUNK, (c + 1) * CHUNK)
        out = _hgrn_chunk(
            proj_scr[rows, 0:HGRN_WIDTH],
            proj_scr[rows, HGRN_WIDTH:2 * HGRN_WIDTH],
            proj_scr[rows, 2 * HGRN_WIDTH:3 * HGRN_WIDTH],
            proj_scr[rows, 3 * HGRN_WIDTH:4 * HGRN_WIDTH],
            lb_ref[...], hg_ref[...], tabs, masks_ref, st_scr)
        mix_scr[rows, 0:HGRN_WIDTH] = out.astype(BF16)

    ubuf_scr[HALO:HALO + MIX_TOKENS, :] = proj_scr[:, 4 * HGRN_WIDTH:IN_COLS]
    pos = (s_i * MIX_TOKENS + 1
           + lax.broadcasted_iota(jnp.int32, (MIX_TOKENS, 1), 0)).astype(F32)
    for g, win in enumerate(POOL_WINDOWS):
        lo = g * POOL_GW
        ext = ubuf_scr[:, lo:lo + POOL_GW]
        acc = ext
        span = 1
        while span < win:
            acc = acc + pltpu.roll(acc, span, axis=0)
            span *= 2
        u = ext[HALO:]
        d = acc[HALO:] / jnp.minimum(pos, float(win)) - u
        y = _dot(d.astype(BF16), pw_ref[g]) * ps_ref[:, lo:lo + POOL_GW]
        mix_scr[:, HGRN_WIDTH + lo:HGRN_WIDTH + lo + POOL_GW] = y.astype(BF16)
    ubuf_scr[0:HALO, :] = ubuf_scr[MIX_TOKENS:MIX_TOKENS + HALO, :]

    @pl.when(step >= 0)
    def _():
        for r in range(FIRST_COPY_TOKENS, MIX_TOKENS):
            for k in range(TOP_K):
                _row_copy(h2buf.at[prev], r, xs_ref, dst_smem[k, r], row_sem).start(priority=k % 2)
        x1_ref[...] = x_ref[...] + _dot(mix_scr[...], wout_ref[...])

    x1 = x1_ref[...]
    h2 = _rms(x1, g2_ref[...])

    h_hi, h_lo = _split_bf16(h2)
    wrh = wrh_ref[...]
    logits = (_dot_nt(wrh, h_hi) + _dot_nt(wrh, h_lo) + _dot_nt(wrl_ref[...], h_hi)
              + br_ref[...])
    eidx = lax.broadcasted_iota(jnp.int32, (N_EXPERTS, MIX_TOKENS), 0)
    work = logits
    tops, sels = [], []
    for _ in range(TOP_K):
        m = jnp.max(work, axis=0, keepdims=True)
        idx = jnp.min(jnp.where(work == m, eidx, N_EXPERTS), axis=0, keepdims=True)
        sel = eidx == idx
        work = jnp.where(sel, -jnp.inf, work)
        tops.append(m)
        sels.append(sel)
    exps = [jnp.exp(m - tops[0]) for m in tops]
    denom = exps[0] + exps[1] + exps[2] + exps[3]
    gates = [e / denom for e in exps]

    sel_any = jnp.zeros((N_EXPERTS, MIX_TOKENS), F32)
    for sel in sels:
        sel_any = jnp.where(sel, 1.0, sel_any)
    cnt = cnt_scr[...]
    before = _dot(sel_any.astype(BF16), tri_ref[...]) + cnt
    cnt_new = cnt + jnp.sum(sel_any, axis=1, keepdims=True)
    cnt_scr[...] = cnt_new

    shift = int(math.log2(ROW_BLOCK))
    before_i = before.astype(jnp.int32)
    cnt_i = cnt.astype(jnp.int32)
    page_lo = cnt_i >> shift
    starts_page = (cnt_i & (ROW_BLOCK - 1)) == 0
    page_hi = (cnt_new.astype(jnp.int32) - 1) >> shift
    need = (cnt_new > cnt) & (starts_page | (page_hi > page_lo))
    need_f = jnp.where(need, 1.0, 0.0)
    earlier = _dot(trie_ref[...], jnp.broadcast_to(need_f, (N_EXPERTS, LANES)).astype(BF16))
    nfree = nfree_scr[...]
    new_id = nfree + earlier[:, 0:1]
    new_idx = jnp.where(starts_page, page_lo, page_lo + 1)
    cur_page = page_scr[...]
    page = jnp.where(need & ((before_i >> shift) == new_idx), new_id, cur_page)
    dest_all = page * float(ROW_BLOCK) + (before_i & (ROW_BLOCK - 1)).astype(F32)
    page_scr[...] = jnp.where(need, new_id, cur_page)
    nfree_new = nfree + jnp.sum(need_f, axis=0, keepdims=True)
    nfree_scr[...] = nfree_new
    page_lane = lax.broadcasted_iota(jnp.int32, (N_EXPERTS, PAGE_PAD), 1).astype(F32)
    owner = jnp.where(need & (page_lane == new_id), eidx[:, 0:1].astype(F32), -1.0)
    pexp = jnp.maximum(pexp_scr[...], jnp.max(owner, axis=0, keepdims=True))
    pexp_scr[...] = pexp

    dests = []
    for k in range(TOP_K):
        dest = jnp.sum(jnp.where(sels[k], dest_all, 0.0), axis=0, keepdims=True)
        dests.append(dest)
        route_ref[k:k + 1, :] = dest
        route_ref[TOP_K + k:TOP_K + k + 1, :] = gates[k]
    route_ref[2 * TOP_K:, :] = jnp.zeros((ROUTE_ROWS - 2 * TOP_K, MIX_TOKENS), F32)
    meta_ref[0:1, :] = pexp
    meta_ref[1:2, :] = jnp.broadcast_to(nfree_new[0:1, :], (1, PAGE_PAD))
    meta_ref[2:, :] = jnp.zeros((6, PAGE_PAD), F32)

    _to_row_tiles(h2buf.at[slot], h2)
    dst_vmem[...] = jnp.concatenate(
        dests + [jnp.zeros((8 - TOP_K, MIX_TOKENS), F32)], axis=0).astype(jnp.int32)
    pltpu.make_async_copy(dst_vmem, dst_smem, idx_sem).start()

    @pl.when(step == n_steps - 1)
    def _():
        pltpu.make_async_copy(dst_vmem, dst_smem, idx_sem).wait()
        for k in range(TOP_K):
            pltpu.make_async_copy(h2buf.at[prev], xs_ref.at[pl.ds(0, MIX_TOKENS * ROW_TILE), :],
                                  row_sem).wait()

        def row_body(it, carry):
            for j in range(ROWS_PER_ITER):
                r = it * ROWS_PER_ITER + j
                for k in range(TOP_K):
                    _row_copy(h2buf.at[slot], r, xs_ref, dst_smem[k, r],
                              row_sem).start(priority=k % 2)
            return carry

        lax.fori_loop(0, MIX_TOKENS // ROWS_PER_ITER, row_body, 0)
        for k in range(TOP_K):
            pltpu.make_async_copy(h2buf.at[slot], xs_ref.at[pl.ds(0, MIX_TOKENS * ROW_TILE), :],
                                  row_sem).wait()

        tail_vmem[0] = jnp.broadcast_to(cnt_new, (N_EXPERTS, LANES)).astype(jnp.int32)
        tail_vmem[1] = jnp.broadcast_to(page_scr[...], (N_EXPERTS, LANES)).astype(jnp.int32)
        tail_vmem[2] = jnp.broadcast_to(nfree_new, (N_EXPERTS, LANES)).astype(jnp.int32)
        cp = pltpu.make_async_copy(tail_vmem, tail_smem, idx_sem)
        cp.start()
        cp.wait()

        def zero_copy(first_row, rows):
            return pltpu.make_async_copy(
                zblk.at[pl.ds(0, rows * ROW_TILE), :],
                xs_ref.at[pl.ds(pl.multiple_of(first_row * ROW_TILE, ROW_TILE), rows * ROW_TILE), :],
                zsem)

        def tail_pass(wait):
            def body(e, carry):
                used = tail_smem[0, e, 0] & (ROW_BLOCK - 1)
                first = tail_smem[1, e, 0] * ROW_BLOCK + used
                left = jnp.where(used == 0, 0, ROW_BLOCK - used)
                rows = ZERO_ROWS
                while rows >= 1:
                    @pl.when((left & rows) != 0)
                    def _():
                        cp = zero_copy(first, rows)
                        cp.wait() if wait else cp.start()
                    first = first + jnp.where((left & rows) != 0, rows, 0)
                    rows //= 2
                return carry

            lax.fori_loop(0, N_EXPERTS, body, 0)

        def page_pass(wait):
            def body(p, carry):
                for half in range(ROW_BLOCK // ZERO_ROWS):
                    cp = zero_copy(p * ROW_BLOCK + half * ZERO_ROWS, ZERO_ROWS)
                    cp.wait() if wait else cp.start()
                return carry

            lax.fori_loop(tail_smem[2, 0, 0], n_pages, body, 0)

        tail_pass(False)
        page_pass(False)
        tail_pass(True)
        page_pass(True)


def _mixer(x2d, g1, w_in, lb, hg, pool_w, pool_scale, w_out, g2, wr_hi, wr_lo, br, seq, n_pages):
    n = x2d.shape[0]
    steps_per_seq = seq // MIX_TOKENS
    assert MIX_TOKENS <= ROW_BLOCK and n_pages <= PAGE_PAD
    spare_pages = MIX_TOKENS * TOP_K // ROW_BLOCK
    tabs_np, masks_np = _decay_tables(CHUNK)
    tabs = jnp.asarray(np.concatenate([tabs_np, tabs_np], axis=1), BF16)
    masks = jnp.asarray(masks_np, F32)
    tri = jnp.asarray(np.triu(np.ones((MIX_TOKENS, MIX_TOKENS), np.float32), 1), BF16)
    tri_e = jnp.asarray(np.tril(np.ones((N_EXPERTS, N_EXPERTS), np.float32), -1), BF16)

    def tok(b, s):
        return (b * steps_per_seq + s, 0)

    def tok_col(b, s):
        return (0, b * steps_per_seq + s)

    def const2(b, s):
        return (0, 0)

    def const3(b, s):
        return (0, 0, 0)

    in_specs = [
        pl.BlockSpec((MIX_TOKENS, D_MODEL), tok),
        pl.BlockSpec((1, D_MODEL), const2),
        pl.BlockSpec((D_MODEL, IN_COLS), const2),
        pl.BlockSpec((1, HGRN_WIDTH), const2),
        pl.BlockSpec((1, HEAD_DIM), const2),
        pl.BlockSpec((len(POOL_WINDOWS), POOL_GW, POOL_GW), const3),
        pl.BlockSpec((1, POOL_WIDTH), const2),
        pl.BlockSpec((D_MODEL, D_MODEL), const2),
        pl.BlockSpec((1, D_MODEL), const2),
        pl.BlockSpec((N_EXPERTS, D_MODEL), const2),
        pl.BlockSpec((N_EXPERTS, D_MODEL), const2),
        pl.BlockSpec((N_EXPERTS, 1), const2),
        pl.BlockSpec(tabs.shape, const2),
        pl.BlockSpec(masks.shape, const3),
        pl.BlockSpec(tri.shape, const2),
        pl.BlockSpec(tri_e.shape, const2),
    ]
    out_specs = [
        pl.BlockSpec((MIX_TOKENS, D_MODEL), tok),
        pl.BlockSpec((ROUTE_ROWS, MIX_TOKENS), tok_col),
        pl.BlockSpec((8, PAGE_PAD), const2),
        pl.BlockSpec(memory_space=pl.ANY),
    ]
    out_shape = [
        jax.ShapeDtypeStruct((n, D_MODEL), F32),
        jax.ShapeDtypeStruct((ROUTE_ROWS, n), F32),
        jax.ShapeDtypeStruct((8, PAGE_PAD), F32),
        jax.ShapeDtypeStruct(((n_pages + spare_pages) * ROW_BLOCK * ROW_TILE, LANES), F32),
    ]
    scratch = [
        pltpu.VMEM((MIX_TOKENS, IN_COLS), F32),
        pltpu.VMEM((MIX_TOKENS, D_MODEL), BF16),
        pltpu.VMEM((N_HEADS, HEAD_DIM, HEAD_DIM), F32),
        pltpu.VMEM((HALO + MIX_TOKENS, POOL_WIDTH), F32),
        pltpu.VMEM((N_EXPERTS, 1), F32),
        pltpu.VMEM((N_EXPERTS, 1), F32),
        pltpu.VMEM((N_EXPERTS, 1), F32),
        pltpu.VMEM((1, PAGE_PAD), F32),
        pltpu.VMEM((2, MIX_TOKENS * ROW_TILE, LANES), F32),
        pltpu.VMEM((8, MIX_TOKENS), jnp.int32),
        pltpu.SMEM((8, MIX_TOKENS), jnp.int32),
        pltpu.VMEM((3, N_EXPERTS, LANES), jnp.int32),
        pltpu.SMEM((3, N_EXPERTS, LANES), jnp.int32),
        pltpu.VMEM((ZERO_ROWS * ROW_TILE, LANES), F32),
        pltpu.SemaphoreType.DMA,
        pltpu.SemaphoreType.DMA,
        pltpu.SemaphoreType.DMA,
    ]
    return pl.pallas_call(
        functools.partial(_mixer_kernel, n_pages),
        grid=(n // seq, steps_per_seq),
        in_specs=in_specs,
        out_specs=out_specs,
        out_shape=out_shape,
        scratch_shapes=scratch,
        compiler_params=pltpu.CompilerParams(
            dimension_semantics=("arbitrary", "arbitrary"),
            vmem_limit_bytes=VMEM_LIMIT_BYTES),
        name="mixer",
    )(x2d, g1, w_in, lb, hg, pool_w, pool_scale, w_out, g2, wr_hi, wr_lo, br, tabs, masks, tri,
      tri_e)


def _expert_kernel(blk_e_ref, nused_ref, blk_page_ref, xs_ref, wg_ref, bg_ref, wu_ref, bu_ref,
                   wd_ref, bd_ref, ys_ref, wg_s, wu_s, wd_s):
    i = pl.program_id(0)
    active = i < nused_ref[0]
    changed = (i == 0) | (blk_e_ref[i] != blk_e_ref[jnp.maximum(i - 1, 0)])

    @pl.when(active & changed)
    def _():
        wg_s[...] = wg_ref[0].astype(BF16)
        wu_s[...] = wu_ref[0].astype(BF16)
        wd_s[...] = wd_ref[0].astype(BF16)

    @pl.when(active)
    def _():
        xb = _from_row_tiles(xs_ref, ROW_BLOCK).astype(BF16)
        gt = jnp.minimum(_dot(xb, wg_s[...]) + bg_ref[0], SWIGLU_LIMIT)
        up = jnp.clip(_dot(xb, wu_s[...]) + bu_ref[0], -SWIGLU_LIMIT, SWIGLU_LIMIT)
        act = (up + 1.0) * (gt * _sigmoid(SWIGLU_ALPHA * gt))
        _to_row_tiles(ys_ref, _dot(act.astype(BF16), wd_s[...]) + bd_ref[0])

    @pl.when(jnp.logical_not(active))
    def _():
        ys_ref[...] = jnp.zeros_like(ys_ref)


def _experts(blk_e, n_used, blk_page, xs, n_blocks, w_gate, b_gate, w_up, b_up, w_down, b_down):
    rows_spec_shape = (ROW_BLOCK * ROW_TILE, LANES)

    def blk(i, be, nu):
        return jnp.minimum(i, nu[0] - 1)

    def row_map(i, be, nu, bp):
        return (bp[blk(i, be, nu)], 0)

    def w_map(i, be, nu, bp):
        return (be[blk(i, be, nu)], 0, 0)

    w_spec = pl.BlockSpec((1, D_MODEL, D_MODEL), w_map)
    b_spec = pl.BlockSpec((1, 1, D_MODEL), w_map)
    grid_spec = pltpu.PrefetchScalarGridSpec(
        num_scalar_prefetch=3,
        grid=(n_blocks,),
        in_specs=[pl.BlockSpec(rows_spec_shape, row_map),
                  w_spec, b_spec, w_spec, b_spec, w_spec, b_spec],
        out_specs=pl.BlockSpec(rows_spec_shape, lambda i, be, nu, bp: (bp[i], 0)),
        scratch_shapes=[pltpu.VMEM((D_MODEL, D_MODEL), BF16)] * 3,
    )
    return pl.pallas_call(
        _expert_kernel,
        grid_spec=grid_spec,
        out_shape=jax.ShapeDtypeStruct((n_blocks * ROW_BLOCK * ROW_TILE, LANES), F32),
        compiler_params=pltpu.CompilerParams(
            dimension_semantics=("arbitrary",),
            vmem_limit_bytes=VMEM_LIMIT_BYTES),
        name="experts",
    )(blk_e, n_used, blk_page, xs, w_gate, b_gate[:, None, :], w_up, b_up[:, None, :],
      w_down, b_down[:, None, :])


def _combine_kernel(final, dest_ref, dest_next_ref, ys_ref, x1_ref, route_ref, gf_ref, out_ref,
                    buf, sems):
    i = pl.program_id(0)
    n_steps = pl.num_programs(0)
    slot = lax.rem(i, 2)

    def gather(idx_ref, s):
        def row_body(it, carry):
            for j in range(ROWS_PER_ITER):
                r = it * ROWS_PER_ITER + j
                for k in range(TOP_K):
                    pltpu.make_async_copy(ys_ref.at[_tile_rows(idx_ref[k * COMBINE_TOKENS + r]), :],
                                          buf.at[s, k, _tile_rows(r), :],
                                          sems.at[s]).start(priority=k % 2)
            return carry

        lax.fori_loop(0, COMBINE_TOKENS // ROWS_PER_ITER, row_body, 0)

    @pl.when(i == 0)
    def _():
        gather(dest_ref, 0)

    @pl.when(i + 1 < n_steps)
    def _():
        gather(dest_next_ref, 1 - slot)

    for k in range(TOP_K):
        pltpu.make_async_copy(ys_ref.at[pl.ds(0, COMBINE_TOKENS * ROW_TILE), :],
                              buf.at[slot, k], sems.at[slot]).wait()

    route = route_ref[...].T
    acc = x1_ref[...]
    for k in range(TOP_K):
        acc = acc + route[:, TOP_K + k:TOP_K + k + 1] * _from_row_tiles(
            buf.at[slot, k], COMBINE_TOKENS)
    out_ref[...] = _rms(acc, gf_ref[...]) if final else acc


def _combine(dest, ys, x1, route, gf, final):
    n = x1.shape[0]
    n_steps = n // COMBINE_TOKENS
    idx_block = (COMBINE_TOKENS * TOP_K,)
    return pl.pallas_call(
        functools.partial(_combine_kernel, final),
        grid=(n_steps,),
        in_specs=[
            pl.BlockSpec(idx_block, lambda i: (i,), memory_space=pltpu.SMEM),
            pl.BlockSpec(idx_block, lambda i: (jnp.minimum(i + 1, n_steps - 1),),
                         memory_space=pltpu.SMEM),
            pl.BlockSpec(memory_space=pl.ANY),
            pl.BlockSpec((COMBINE_TOKENS, D_MODEL), lambda i: (i, 0)),
            pl.BlockSpec((ROUTE_ROWS, COMBINE_TOKENS), lambda i: (0, i)),
            pl.BlockSpec((1, D_MODEL), lambda i: (0, 0)),
        ],
        out_specs=pl.BlockSpec((COMBINE_TOKENS, D_MODEL), lambda i: (i, 0)),
        out_shape=jax.ShapeDtypeStruct((n, D_MODEL), F32),
        scratch_shapes=[
            pltpu.VMEM((2, TOP_K, COMBINE_TOKENS * ROW_TILE, LANES), F32),
            pltpu.SemaphoreType.DMA((2,)),
        ],
        compiler_params=pltpu.CompilerParams(
            dimension_semantics=("arbitrary",),
            vmem_limit_bytes=VMEM_LIMIT_BYTES),
        name="combine",
    )(dest, dest, ys, x1, route, gf)


def kernel(x, norm1_g, w_in, hgrn_lb, hgrn_norm_g, pool_w, pool_scale, w_out, norm2_g,
           router_w, router_b, w_gate, b_gate, w_up, b_up, w_down, b_down, norm_f_g):
    depth = w_in.shape[0]
    batch, seq, _ = x.shape
    n = batch * seq
    n_pages = n * TOP_K // ROW_BLOCK + N_EXPERTS
    lb_all = jnp.cumsum(jax.nn.softmax(hgrn_lb.astype(F32), axis=0), axis=0)

    xt = x.reshape(n, D_MODEL)
    for l in range(depth):
        wr = router_w[l].T
        wr_hi = wr.astype(BF16)
        wr_lo = (wr - wr_hi.astype(F32)).astype(BF16)
        br = router_b[l][:, None]
        x1, route, meta, xs = _mixer(
            xt, norm1_g[l][None, :], w_in[l].astype(BF16), lb_all[l][None, :],
            hgrn_norm_g[l][None, :], pool_w[l].astype(BF16), pool_scale[l][None, :],
            w_out[l].astype(BF16), norm2_g[l][None, :], wr_hi, wr_lo, br, seq, n_pages)

        page_expert = meta[0, :n_pages].astype(jnp.int32)
        n_used = meta[1, 0:1].astype(jnp.int32)
        page_ids = jnp.arange(n_pages, dtype=jnp.int32)
        key = jnp.where(page_expert < 0, N_EXPERTS, page_expert) * PAGE_PAD + page_ids
        rank = jnp.sum((key[None, :] < key[:, None]).astype(jnp.int32), axis=1)
        at_step = rank[None, :] == page_ids[:, None]
        blk_page = jnp.sum(jnp.where(at_step, page_ids[None, :], 0), axis=1)
        blk_e = jnp.clip(jnp.sum(jnp.where(at_step, page_expert[None, :], 0), axis=1),
                         0, N_EXPERTS - 1)
        dest = route[0:TOP_K].astype(jnp.int32).reshape(TOP_K, n // COMBINE_TOKENS, COMBINE_TOKENS)
        dest = dest.transpose(1, 0, 2).reshape(n * TOP_K)

        ys = _experts(blk_e, n_used, blk_page, xs, n_pages, w_gate[l], b_gate[l], w_up[l],
                      b_up[l], w_down[l], b_down[l])
        xt = _combine(dest, ys, x1, route, norm_f_g[None, :], l == depth - 1)
    return xt.reshape(batch, seq, D_MODEL)
```

```python
import functools
import math

import numpy as np
import jax
import jax.numpy as jnp
from jax import lax
from jax.experimental import pallas as pl
from jax.experimental.pallas import tpu as pltpu

F32 = jnp.float32
BF16 = jnp.bfloat16

D_MODEL = 1024
HGRN_WIDTH = 512
HEAD_DIM = 128
N_HEADS = HGRN_WIDTH // HEAD_DIM
POOL_WIDTH = 512
POOL_WINDOWS = (2, 4, 8, 16)
POOL_GW = POOL_WIDTH // len(POOL_WINDOWS)
IN_COLS = 4 * HGRN_WIDTH + POOL_WIDTH
N_EXPERTS = 32
TOP_K = 4
SWIGLU_LIMIT = 7.0
SWIGLU_ALPHA = 1.702
NORM_EPS = 1e-6

LANES = 128
ROW_TILE = D_MODEL // LANES
VMEM_LIMIT_BYTES = 56 * 1024 * 1024

MIX_TOKENS = 512
CHUNK = 64
N_LEVELS = int(math.log2(CHUNK))
HALO = max(POOL_WINDOWS)
FIRST_COPY_TOKENS = 320
ROW_BLOCK = 512
ZERO_ROWS = ROW_BLOCK // 2
PAGE_PAD = 384
COMBINE_TOKENS = 256
ROWS_PER_ITER = 32
ROUTE_ROWS = 16
LOG2_E = 1.4426950408889634


def _dot(a, b):
    return jnp.dot(a, b, preferred_element_type=F32)


def _dot_nt(a, b):
    return lax.dot_general(a, b, (((1,), (1,)), ((), ())), preferred_element_type=F32)


def _dot_tn(a, b):
    return lax.dot_general(a, b, (((0,), (0,)), ((), ())), preferred_element_type=F32)


def _sigmoid(x):
    return 1.0 / (1.0 + jnp.exp2(x * (-LOG2_E)))


def _rms(x, g):
    return x * lax.rsqrt(jnp.mean(x * x, axis=-1, keepdims=True) + NORM_EPS) * g


def _split_bf16(x):
    hi = x.astype(BF16)
    lo = (x - hi.astype(F32)).astype(BF16)
    return hi, lo


def _to_row_tiles(ref, value):
    t = value.shape[0]
    for j in range(ROW_TILE):
        ref[pl.ds(j, t, stride=ROW_TILE), :] = value[:, j * LANES:(j + 1) * LANES]


def _from_row_tiles(ref, t):
    return jnp.concatenate(
        [ref[pl.ds(j, t, stride=ROW_TILE), :] for j in range(ROW_TILE)], axis=1)


def _tile_rows(row):
    return pl.ds(pl.multiple_of(row * ROW_TILE, ROW_TILE), ROW_TILE)


def _decay_tables(ch):
    t = np.arange(ch)[:, None]
    u = np.arange(ch)[None, :]
    mats = [u <= t, u > t]
    masks = []
    m = 1
    while m < ch:
        right = (t // m) % 2 == 1
        mid = (t // (2 * m)) * (2 * m) + m
        mats.append(np.where(right, (u >= mid) & (u <= t), (u > t) & (u <= mid - 1)))
        masks.append(right & ((u // m) % 2 == 0) & ((u // (2 * m)) == (t // (2 * m))))
        m *= 2
    return (np.concatenate(mats, 0).astype(np.float32),
            np.stack(masks).astype(np.float32))


def _hgrn_chunk(qp, fp, v, gp, lb, hg, tabs, masks_ref, st_ref):
    heads = [slice(h * HEAD_DIM, (h + 1) * HEAD_DIM) for h in range(N_HEADS)]
    q = qp * _sigmoid(qp)
    f = lb + (1.0 - lb) * _sigmoid(fp)
    lf2 = jnp.log(f) * LOG2_E
    k = 1.0 - f
    x = jnp.exp2(_dot(tabs, jnp.concatenate(_split_bf16(lf2), axis=0)))
    eb = x[0:CHUNK]
    esuf = x[CHUNK:2 * CHUNK]
    vb = v.astype(BF16)
    q_in = (q * eb).astype(BF16)
    k_end = (k * esuf).astype(BF16)
    qk = q * k
    st = [st_ref[h] for h in range(N_HEADS)]
    o = [_dot_nt(q_in[:, hs], st[h].astype(BF16)) for h, hs in enumerate(heads)]
    upd = [_dot_tn(vb[:, hs], k_end[:, hs]) for hs in heads]
    s = [jnp.zeros((CHUNK, CHUNK), F32) for _ in heads]
    row = lax.broadcasted_iota(jnp.int32, (CHUNK, HGRN_WIDTH), 0)
    for lvl in range(N_LEVELS):
        m = 1 << lvl
        if m % 8 == 0:
            qk_rows = jnp.concatenate(
                [(q if (r0 // m) % 2 else k)[r0:r0 + m] for r0 in range(0, CHUNK, m)], axis=0)
        else:
            qk_rows = jnp.where((row & m) != 0, q, k)
        z = (qk_rows * x[(2 + lvl) * CHUNK:(3 + lvl) * CHUNK]).astype(BF16)
        mask = masks_ref[lvl] != 0.0
        for h, hs in enumerate(heads):
            s[h] = jnp.where(mask, _dot_nt(z[:, hs], z[:, hs]), s[h])
    outs = []
    for h, hs in enumerate(heads):
        oh = o[h] + jnp.sum(qk[:, hs], axis=-1, keepdims=True) * v[:, hs]
        oh = oh + _dot(s[h].astype(BF16), vb[:, hs])
        st_ref[h] = eb[CHUNK - 1:CHUNK, hs] * st[h] + upd[h]
        oh = oh * lax.rsqrt(jnp.mean(oh * oh, axis=-1, keepdims=True) + NORM_EPS) * hg
        outs.append(oh)
    return jnp.concatenate(outs, axis=1) * (gp * _sigmoid(gp))


def _row_copy(src, src_row, dst, dst_row, sem):
    return pltpu.make_async_copy(src.at[_tile_rows(src_row), :], dst.at[_tile_rows(dst_row), :], sem)


def _mixer_kernel(n_pages,
                  x_ref, g1_ref, win_ref, lb_ref, hg_ref, pw_ref, ps_ref, wout_ref, g2_ref,
                  wrh_ref, wrl_ref, br_ref, tabs_ref, masks_ref, tri_ref, trie_ref,
                  x1_ref, route_ref, meta_ref, xs_ref,
                  proj_scr, mix_scr, st_scr, ubuf_scr, cnt_scr, page_scr, nfree_scr, pexp_scr,
                  h2buf, dst_vmem, dst_smem, tail_vmem, tail_smem, zblk, row_sem, idx_sem, zsem):
    b_i = pl.program_id(0)
    s_i = pl.program_id(1)
    step = b_i * pl.num_programs(1) + s_i
    n_steps = pl.num_programs(0) * pl.num_programs(1)
    slot = lax.rem(step, 2)
    prev = 1 - slot

    @pl.when(s_i == 0)
    def _():
        st_scr[...] = jnp.zeros_like(st_scr)
        ubuf_scr[0:HALO, :] = jnp.zeros((HALO, POOL_WIDTH), F32)

    @pl.when(step == 0)
    def _():
        cnt_scr[...] = jnp.zeros_like(cnt_scr)
        page_scr[...] = jnp.zeros_like(page_scr)
        nfree_scr[...] = jnp.zeros_like(nfree_scr)
        pexp_scr[...] = jnp.full(pexp_scr.shape, -1.0, F32)
        zblk[...] = jnp.zeros_like(zblk)
        h2buf[1] = jnp.zeros(h2buf.shape[1:], F32)

        def spare_body(r, carry):
            for k in range(TOP_K):
                dst_smem[k, r] = n_pages * ROW_BLOCK + r * TOP_K + k
            return carry

        lax.fori_loop(0, MIX_TOKENS, spare_body, 0)

    @pl.when(step > 0)
    def _():
        pltpu.make_async_copy(dst_vmem, dst_smem, idx_sem).wait()
        for k in range(TOP_K):
            pltpu.make_async_copy(h2buf.at[slot], xs_ref.at[pl.ds(0, MIX_TOKENS * ROW_TILE), :],
                                  row_sem).wait()

    @pl.when(step >= 0)
    def _():
        for r in range(FIRST_COPY_TOKENS):
            for k in range(TOP_K):
                _row_copy(h2buf.at[prev], r, xs_ref, dst_smem[k, r], row_sem).start(priority=k % 2)
        h = _rms(x_ref[...], g1_ref[...]).astype(BF16)
        proj_scr[...] = _dot(h, win_ref[...])

    tabs = tabs_ref[...]

    for c in range(MIX_TOKENS // CHUNK):
        rows = slice(c * CHUNK, (c + 1) * CHUNK)
        out = _hgrn_chunk(
            proj_scr[rows, 0:HGRN_WIDTH],
            proj_scr[rows, HGRN_WIDTH:2 * HGRN_WIDTH],
            proj_scr[rows, 2 * HGRN_WIDTH:3 * HGRN_WIDTH],
            proj_scr[rows, 3 * HGRN_WIDTH:4 * HGRN_WIDTH],
            lb_ref[...], hg_ref[...], tabs, masks_ref, st_scr)
        mix_scr[rows, 0:HGRN_WIDTH] = out.astype(BF16)

    ubuf_scr[HALO:HALO + MIX_TOKENS, :] = proj_scr[:, 4 * HGRN_WIDTH:IN_COLS]
    pos = (s_i * MIX_TOKENS + 1
           + lax.broadcasted_iota(jnp.int32, (MIX_TOKENS, 1), 0)).astype(F32)
    for g, win in enumerate(POOL_WINDOWS):
        lo = g * POOL_GW
        ext = ubuf_scr[:, lo:lo + POOL_GW]
        acc = ext
        span = 1
        while span < win:
            acc = acc + pltpu.roll(acc, span, axis=0)
            span *= 2
        u = ext[HALO:]
        d = acc[HALO:] / jnp.minimum(pos, float(win)) - u
        y = _dot(d.astype(BF16), pw_ref[g]) * ps_ref[:, lo:lo + POOL_GW]
        mix_scr[:, HGRN_WIDTH + lo:HGRN_WIDTH + lo + POOL_GW] = y.astype(BF16)
    ubuf_scr[0:HALO, :] = ubuf_scr[MIX_TOKENS:MIX_TOKENS + HALO, :]

    @pl.when(step >= 0)
    def _():
        for r in range(FIRST_COPY_TOKENS, MIX_TOKENS):
            for k in range(TOP_K):
                _row_copy(h2buf.at[prev], r, xs_ref, dst_smem[k, r], row_sem).start(priority=k % 2)
        x1_ref[...] = x_ref[...] + _dot(mix_scr[...], wout_ref[...])

    x1 = x1_ref[...]
    h2 = _rms(x1, g2_ref[...])

    h_hi, h_lo = _split_bf16(h2)
    wrh = wrh_ref[...]
    logits = (_dot_nt(wrh, h_hi) + _dot_nt(wrh, h_lo) + _dot_nt(wrl_ref[...], h_hi)
              + br_ref[...])
    eidx = lax.broadcasted_iota(jnp.int32, (N_EXPERTS, MIX_TOKENS), 0)
    work = logits
    tops, sels = [], []
    for _ in range(TOP_K):
        m = jnp.max(work, axis=0, keepdims=True)
        idx = jnp.min(jnp.where(work == m, eidx, N_EXPERTS), axis=0, keepdims=True)
        sel = eidx == idx
        work = jnp.where(sel, -jnp.inf, work)
        tops.append(m)
        sels.append(sel)
    exps = [jnp.exp(m - tops[0]) for m in tops]
    denom = exps[0] + exps[1] + exps[2] + exps[3]
    gates = [e / denom for e in exps]

    sel_any = jnp.zeros((N_EXPERTS, MIX_TOKENS), F32)
    for sel in sels:
        sel_any = jnp.where(sel, 1.0, sel_any)
    cnt = cnt_scr[...]
    before = _dot(sel_any.astype(BF16), tri_ref[...]) + cnt
    cnt_new = cnt + jnp.sum(sel_any, axis=1, keepdims=True)
    cnt_scr[...] = cnt_new

    shift = int(math.log2(ROW_BLOCK))
    before_i = before.astype(jnp.int32)
    cnt_i = cnt.astype(jnp.int32)
    page_lo = cnt_i >> shift
    starts_page = (cnt_i & (ROW_BLOCK - 1)) == 0
    page_hi = (cnt_new.astype(jnp.int32) - 1) >> shift
    need = (cnt_new > cnt) & (starts_page | (page_hi > page_lo))
    need_f = jnp.where(need, 1.0, 0.0)
    earlier = _dot(trie_ref[...], jnp.broadcast_to(need_f, (N_EXPERTS, LANES)).astype(BF16))
    nfree = nfree_scr[...]
    new_id = nfree + earlier[:, 0:1]
    new_idx = jnp.where(starts_page, page_lo, page_lo + 1)
    cur_page = page_scr[...]
    page = jnp.where(need & ((before_i >> shift) == new_idx), new_id, cur_page)
    dest_all = page * float(ROW_BLOCK) + (before_i & (ROW_BLOCK - 1)).astype(F32)
    page_scr[...] = jnp.where(need, new_id, cur_page)
    nfree_new = nfree + jnp.sum(need_f, axis=0, keepdims=True)
    nfree_scr[...] = nfree_new
    page_lane = lax.broadcasted_iota(jnp.int32, (N_EXPERTS, PAGE_PAD), 1).astype(F32)
    owner = jnp.where(need & (page_lane == new_id), eidx[:, 0:1].astype(F32), -1.0)
    pexp = jnp.maximum(pexp_scr[...], jnp.max(owner, axis=0, keepdims=True))
    pexp_scr[...] = pexp

    dests = []
    for k in range(TOP_K):
        dest = jnp.sum(jnp.where(sels[k], dest_all, 0.0), axis=0, keepdims=True)
        dests.append(dest)
        route_ref[k:k + 1, :] = dest
        route_ref[TOP_K + k:TOP_K + k + 1, :] = gates[k]
    route_ref[2 * TOP_K:, :] = jnp.zeros((ROUTE_ROWS - 2 * TOP_K, MIX_TOKENS), F32)
    meta_ref[0:1, :] = pexp
    meta_ref[1:2, :] = jnp.broadcast_to(nfree_new[0:1, :], (1, PAGE_PAD))
    meta_ref[2:, :] = jnp.zeros((6, PAGE_PAD), F32)

    _to_row_tiles(h2buf.at[slot], h2)
    dst_vmem[...] = jnp.concatenate(
        dests + [jnp.zeros((8 - TOP_K, MIX_TOKENS), F32)], axis=0).astype(jnp.int32)
    pltpu.make_async_copy(dst_vmem, dst_smem, idx_sem).start()

    @pl.when(step == n_steps - 1)
    def _():
        pltpu.make_async_copy(dst_vmem, dst_smem, idx_sem).wait()
        for k in range(TOP_K):
            pltpu.make_async_copy(h2buf.at[prev], xs_ref.at[pl.ds(0, MIX_TOKENS * ROW_TILE), :],
                                  row_sem).wait()

        def row_body(it, carry):
            for j in range(ROWS_PER_ITER):
                r = it * ROWS_PER_ITER + j
                for k in range(TOP_K):
                    _row_copy(h2buf.at[slot], r, xs_ref, dst_smem[k, r],
                              row_sem).start(priority=k % 2)
            return carry

        lax.fori_loop(0, MIX_TOKENS // ROWS_PER_ITER, row_body, 0)
        for k in range(TOP_K):
            pltpu.make_async_copy(h2buf.at[slot], xs_ref.at[pl.ds(0, MIX_TOKENS * ROW_TILE), :],
                                  row_sem).wait()

        tail_vmem[0] = jnp.broadcast_to(cnt_new, (N_EXPERTS, LANES)).astype(jnp.int32)
        tail_vmem[1] = jnp.broadcast_to(page_scr[...], (N_EXPERTS, LANES)).astype(jnp.int32)
        tail_vmem[2] = jnp.broadcast_to(nfree_new, (N_EXPERTS, LANES)).astype(jnp.int32)
        cp = pltpu.make_async_copy(tail_vmem, tail_smem, idx_sem)
        cp.start()
        cp.wait()

        def zero_copy(first_row, rows):
            return pltpu.make_async_copy(
                zblk.at[pl.ds(0, rows * ROW_TILE), :],
                xs_ref.at[pl.ds(pl.multiple_of(first_row * ROW_TILE, ROW_TILE), rows * ROW_TILE), :],
                zsem)

        def tail_pass(wait):
            def body(e, carry):
                used = tail_smem[0, e, 0] & (ROW_BLOCK - 1)
                first = tail_smem[1, e, 0] * ROW_BLOCK + used
                left = jnp.where(used == 0, 0, ROW_BLOCK - used)
                rows = ZERO_ROWS
                while rows >= 1:
                    @pl.when((left & rows) != 0)
                    def _():
                        cp = zero_copy(first, rows)
                        cp.wait() if wait else cp.start()
                    first = first + jnp.where((left & rows) != 0, rows, 0)
                    rows //= 2
                return carry

            lax.fori_loop(0, N_EXPERTS, body, 0)

        def page_pass(wait):
            def body(p, carry):
                for half in range(ROW_BLOCK // ZERO_ROWS):
                    cp = zero_copy(p * ROW_BLOCK + half * ZERO_ROWS, ZERO_ROWS)
                    cp.wait() if wait else cp.start()
                return carry

            lax.fori_loop(tail_smem[2, 0, 0], n_pages, body, 0)

        tail_pass(False)
        page_pass(False)
        tail_pass(True)
        page_pass(True)


def _mixer(x2d, g1, w_in, lb, hg, pool_w, pool_scale, w_out, g2, wr_hi, wr_lo, br, seq, n_pages):
    n = x2d.shape[0]
    steps_per_seq = seq // MIX_TOKENS
    assert MIX_TOKENS <= ROW_BLOCK and n_pages <= PAGE_PAD
    spare_pages = MIX_TOKENS * TOP_K // ROW_BLOCK
    tabs_np, masks_np = _decay_tables(CHUNK)
    tabs = jnp.asarray(np.concatenate([tabs_np, tabs_np], axis=1), BF16)
    masks = jnp.asarray(masks_np, F32)
    tri = jnp.asarray(np.triu(np.ones((MIX_TOKENS, MIX_TOKENS), np.float32), 1), BF16)
    tri_e = jnp.asarray(np.tril(np.ones((N_EXPERTS, N_EXPERTS), np.float32), -1), BF16)

    def tok(b, s):
        return (b * steps_per_seq + s, 0)

    def tok_col(b, s):
        return (0, b * steps_per_seq + s)

    def const2(b, s):
        return (0, 0)

    def const3(b, s):
        return (0, 0, 0)

    in_specs = [
        pl.BlockSpec((MIX_TOKENS, D_MODEL), tok),
        pl.BlockSpec((1, D_MODEL), const2),
        pl.BlockSpec((D_MODEL, IN_COLS), const2),
        pl.BlockSpec((1, HGRN_WIDTH), const2),
        pl.BlockSpec((1, HEAD_DIM), const2),
        pl.BlockSpec((len(POOL_WINDOWS), POOL_GW, POOL_GW), const3),
        pl.BlockSpec((1, POOL_WIDTH), const2),
        pl.BlockSpec((D_MODEL, D_MODEL), const2),
        pl.BlockSpec((1, D_MODEL), const2),
        pl.BlockSpec((N_EXPERTS, D_MODEL), const2),
        pl.BlockSpec((N_EXPERTS, D_MODEL), const2),
        pl.BlockSpec((N_EXPERTS, 1), const2),
        pl.BlockSpec(tabs.shape, const2),
        pl.BlockSpec(masks.shape, const3),
        pl.BlockSpec(tri.shape, const2),
        pl.BlockSpec(tri_e.shape, const2),
    ]
    out_specs = [
        pl.BlockSpec((MIX_TOKENS, D_MODEL), tok),
        pl.BlockSpec((ROUTE_ROWS, MIX_TOKENS), tok_col),
        pl.BlockSpec((8, PAGE_PAD), const2),
        pl.BlockSpec(memory_space=pl.ANY),
    ]
    out_shape = [
        jax.ShapeDtypeStruct((n, D_MODEL), F32),
        jax.ShapeDtypeStruct((ROUTE_ROWS, n), F32),
        jax.ShapeDtypeStruct((8, PAGE_PAD), F32),
        jax.ShapeDtypeStruct(((n_pages + spare_pages) * ROW_BLOCK * ROW_TILE, LANES), F32),
    ]
    scratch = [
        pltpu.VMEM((MIX_TOKENS, IN_COLS), F32),
        pltpu.VMEM((MIX_TOKENS, D_MODEL), BF16),
        pltpu.VMEM((N_HEADS, HEAD_DIM, HEAD_DIM), F32),
        pltpu.VMEM((HALO + MIX_TOKENS, POOL_WIDTH), F32),
        pltpu.VMEM((N_EXPERTS, 1), F32),
        pltpu.VMEM((N_EXPERTS, 1), F32),
        pltpu.VMEM((N_EXPERTS, 1), F32),
        pltpu.VMEM((1, PAGE_PAD), F32),
        pltpu.VMEM((2, MIX_TOKENS * ROW_TILE, LANES), F32),
        pltpu.VMEM((8, MIX_TOKENS), jnp.int32),
        pltpu.SMEM((8, MIX_TOKENS), jnp.int32),
        pltpu.VMEM((3, N_EXPERTS, LANES), jnp.int32),
        pltpu.SMEM((3, N_EXPERTS, LANES), jnp.int32),
        pltpu.VMEM((ZERO_ROWS * ROW_TILE, LANES), F32),
        pltpu.SemaphoreType.DMA,
        pltpu.SemaphoreType.DMA,
        pltpu.SemaphoreType.DMA,
    ]
    return pl.pallas_call(
        functools.partial(_mixer_kernel, n_pages),
        grid=(n // seq, steps_per_seq),
        in_specs=in_specs,
        out_specs=out_specs,
        out_shape=out_shape,
        scratch_shapes=scratch,
        compiler_params=pltpu.CompilerParams(
            dimension_semantics=("arbitrary", "arbitrary"),
            vmem_limit_bytes=VMEM_LIMIT_BYTES),
        name="mixer",
    )(x2d, g1, w_in, lb, hg, pool_w, pool_scale, w_out, g2, wr_hi, wr_lo, br, tabs, masks, tri,
      tri_e)


def _expert_kernel(blk_e_ref, nused_ref, blk_page_ref, xs_ref, wg_ref, bg_ref, wu_ref, bu_ref,
                   wd_ref, bd_ref, ys_ref, wg_s, wu_s, wd_s):
    i = pl.program_id(0)
    active = i < nused_ref[0]
    changed = (i == 0) | (blk_e_ref[i] != blk_e_ref[jnp.maximum(i - 1, 0)])

    @pl.when(active & changed)
    def _():
        wg_s[...] = wg_ref[0].astype(BF16)
        wu_s[...] = wu_ref[0].astype(BF16)
        wd_s[...] = wd_ref[0].astype(BF16)

    @pl.when(active)
    def _():
        xb = _from_row_tiles(xs_ref, ROW_BLOCK).astype(BF16)
        gt = jnp.minimum(_dot(xb, wg_s[...]) + bg_ref[0], SWIGLU_LIMIT)
        up = jnp.clip(_dot(xb, wu_s[...]) + bu_ref[0], -SWIGLU_LIMIT, SWIGLU_LIMIT)
        act = (up + 1.0) * (gt * _sigmoid(SWIGLU_ALPHA * gt))
        _to_row_tiles(ys_ref, _dot(act.astype(BF16), wd_s[...]) + bd_ref[0])

    @pl.when(jnp.logical_not(active))
    def _():
        ys_ref[...] = jnp.zeros_like(ys_ref)


def _experts(blk_e, n_used, blk_page, xs, n_blocks, w_gate, b_gate, w_up, b_up, w_down, b_down):
    rows_spec_shape = (ROW_BLOCK * ROW_TILE, LANES)

    def blk(i, be, nu):
        return jnp.minimum(i, nu[0] - 1)

    def row_map(i, be, nu, bp):
        return (bp[blk(i, be, nu)], 0)

    def w_map(i, be, nu, bp):
        return (be[blk(i, be, nu)], 0, 0)

    w_spec = pl.BlockSpec((1, D_MODEL, D_MODEL), w_map)
    b_spec = pl.BlockSpec((1, 1, D_MODEL), w_map)
    grid_spec = pltpu.PrefetchScalarGridSpec(
        num_scalar_prefetch=3,
        grid=(n_blocks,),
        in_specs=[pl.BlockSpec(rows_spec_shape, row_map),
                  w_spec, b_spec, w_spec, b_spec, w_spec, b_spec],
        out_specs=pl.BlockSpec(rows_spec_shape, lambda i, be, nu, bp: (bp[i], 0)),
        scratch_shapes=[pltpu.VMEM((D_MODEL, D_MODEL), BF16)] * 3,
    )
    return pl.pallas_call(
        _expert_kernel,
        grid_spec=grid_spec,
        out_shape=jax.ShapeDtypeStruct((n_blocks * ROW_BLOCK * ROW_TILE, LANES), F32),
        compiler_params=pltpu.CompilerParams(
            dimension_semantics=("arbitrary",),
            vmem_limit_bytes=VMEM_LIMIT_BYTES),
        name="experts",
    )(blk_e, n_used, blk_page, xs, w_gate, b_gate[:, None, :], w_up, b_up[:, None, :],
      w_down, b_down[:, None, :])


def _combine_kernel(final, dest_ref, dest_next_ref, ys_ref, x1_ref, route_ref, gf_ref, out_ref,
                    buf, sems):
    i = pl.program_id(0)
    n_steps = pl.num_programs(0)

    def gather_copy(idx_ref, s, k, r):
        return pltpu.make_async_copy(ys_ref.at[_tile_rows(idx_ref[k * COMBINE_TOKENS + r]), :],
                                     buf.at[s, k, _tile_rows(r), :], sems.at[s])

    def wait_tile(s):
        for k in range(TOP_K):
            pltpu.make_async_copy(ys_ref.at[pl.ds(0, COMBINE_TOKENS * ROW_TILE), :],
                                  buf.at[s, k], sems.at[s]).wait()

    @pl.when(i == 0)
    def _():
        def row_body(it, carry):
            for j in range(ROWS_PER_ITER):
                r = it * ROWS_PER_ITER + j
                for k in range(TOP_K):
                    gather_copy(dest_ref, 0, k, r).start(priority=k % 2)
            return carry

        lax.fori_loop(0, COMBINE_TOKENS // ROWS_PER_ITER, row_body, 0)

    def step_body(s):
        wait_tile(s)
        for r in range(COMBINE_TOKENS):
            for k in range(TOP_K):
                gather_copy(dest_next_ref, 1 - s, k, r).start(priority=k % 2)
        route = route_ref[...].T
        acc = x1_ref[...]
        for k in range(TOP_K):
            acc = acc + route[:, TOP_K + k:TOP_K + k + 1] * _from_row_tiles(
                buf.at[s, k], COMBINE_TOKENS)
        out_ref[...] = _rms(acc, gf_ref[...]) if final else acc

        @pl.when(i == n_steps - 1)
        def _():
            wait_tile(1 - s)

    for s in range(2):
        @pl.when(lax.rem(i, 2) == s)
        def _():
            step_body(s)


def _combine(dest, ys, x1, route, gf, final):
    n = x1.shape[0]
    n_steps = n // COMBINE_TOKENS
    idx_block = (COMBINE_TOKENS * TOP_K,)
    return pl.pallas_call(
        functools.partial(_combine_kernel, final),
        grid=(n_steps,),
        in_specs=[
            pl.BlockSpec(idx_block, lambda i: (i,), memory_space=pltpu.SMEM),
            pl.BlockSpec(idx_block, lambda i: (jnp.minimum(i + 1, n_steps - 1),),
                         memory_space=pltpu.SMEM),
            pl.BlockSpec(memory_space=pl.ANY),
            pl.BlockSpec((COMBINE_TOKENS, D_MODEL), lambda i: (i, 0)),
            pl.BlockSpec((ROUTE_ROWS, COMBINE_TOKENS), lambda i: (0, i)),
            pl.BlockSpec((1, D_MODEL), lambda i: (0, 0)),
        ],
        out_specs=pl.BlockSpec((COMBINE_TOKENS, D_MODEL), lambda i: (i, 0)),
        out_shape=jax.ShapeDtypeStruct((n, D_MODEL), F32),
        scratch_shapes=[
            pltpu.VMEM((2, TOP_K, COMBINE_TOKENS * ROW_TILE, LANES), F32),
            pltpu.SemaphoreType.DMA((2,)),
        ],
        compiler_params=pltpu.CompilerParams(
            dimension_semantics=("arbitrary",),
            vmem_limit_bytes=VMEM_LIMIT_BYTES),
        name="combine",
    )(dest, dest, ys, x1, route, gf)


def kernel(x, norm1_g, w_in, hgrn_lb, hgrn_norm_g, pool_w, pool_scale, w_out, norm2_g,
           router_w, router_b, w_gate, b_gate, w_up, b_up, w_down, b_down, norm_f_g):
    depth = w_in.shape[0]
    batch, seq, _ = x.shape
    n = batch * seq
    n_pages = n * TOP_K // ROW_BLOCK + N_EXPERTS
    lb_all = jnp.cumsum(jax.nn.softmax(hgrn_lb.astype(F32), axis=0), axis=0)

    xt = x.reshape(n, D_MODEL)
    for l in range(depth):
        wr = router_w[l].T
        wr_hi = wr.astype(BF16)
        wr_lo = (wr - wr_hi.astype(F32)).astype(BF16)
        br = router_b[l][:, None]
        x1, route, meta, xs = _mixer(
            xt, norm1_g[l][None, :], w_in[l].astype(BF16), lb_all[l][None, :],
            hgrn_norm_g[l][None, :], pool_w[l].astype(BF16), pool_scale[l][None, :],
            w_out[l].astype(BF16), norm2_g[l][None, :], wr_hi, wr_lo, br, seq, n_pages)

        page_expert = meta[0, :n_pages].astype(jnp.int32)
        n_used = meta[1, 0:1].astype(jnp.int32)
        page_ids = jnp.arange(n_pages, dtype=jnp.int32)
        key = jnp.where(page_expert < 0, N_EXPERTS, page_expert) * PAGE_PAD + page_ids
        rank = jnp.sum((key[None, :] < key[:, None]).astype(jnp.int32), axis=1)
        at_step = rank[None, :] == page_ids[:, None]
        blk_page = jnp.sum(jnp.where(at_step, page_ids[None, :], 0), axis=1)
        blk_e = jnp.clip(jnp.sum(jnp.where(at_step, page_expert[None, :], 0), axis=1),
                         0, N_EXPERTS - 1)
        dest = route[0:TOP_K].astype(jnp.int32).reshape(TOP_K, n // COMBINE_TOKENS, COMBINE_TOKENS)
        dest = dest.transpose(1, 0, 2).reshape(n * TOP_K)

        ys = _experts(blk_e, n_used, blk_page, xs, n_pages, w_gate[l], b_gate[l], w_up[l],
                      b_up[l], w_down[l], b_down[l])
        xt = _combine(dest, ys, x1, route, norm_f_g[None, :], l == depth - 1)
    return xt.reshape(batch, seq, D_MODEL)
```

```python
import functools
import math

import numpy as np
import jax
import jax.numpy as jnp
from jax import lax
from jax.experimental import pallas as pl
from jax.experimental.pallas import tpu as pltpu

F32 = jnp.float32
BF16 = jnp.bfloat16

D_MODEL = 1024
HGRN_WIDTH = 512
HEAD_DIM = 128
N_HEADS = HGRN_WIDTH // HEAD_DIM
POOL_WIDTH = 512
POOL_WINDOWS = (2, 4, 8, 16)
POOL_GW = POOL_WIDTH // len(POOL_WINDOWS)
IN_COLS = 4 * HGRN_WIDTH + POOL_WIDTH
N_EXPERTS = 32
TOP_K = 4
SWIGLU_LIMIT = 7.0
SWIGLU_ALPHA = 1.702
NORM_EPS = 1e-6

LANES = 128
ROW_TILE = D_MODEL // LANES
VMEM_LIMIT_BYTES = 56 * 1024 * 1024

MIX_TOKENS = 512
CHUNK = 64
N_LEVELS = int(math.log2(CHUNK))
HALO = max(POOL_WINDOWS)
FIRST_COPY_TOKENS = 320
ROW_BLOCK = 512
ZERO_ROWS = ROW_BLOCK // 2
PAGE_PAD = 384
COMBINE_TOKENS = 256
ROWS_PER_ITER = 32
ROUTE_ROWS = 16
LOG2_E = 1.4426950408889634


def _dot(a, b):
    return jnp.dot(a, b, preferred_element_type=F32)


def _dot_nt(a, b):
    return lax.dot_general(a, b, (((1,), (1,)), ((), ())), preferred_element_type=F32)


def _dot_tn(a, b):
    return lax.dot_general(a, b, (((0,), (0,)), ((), ())), preferred_element_type=F32)


def _sigmoid(x):
    return 1.0 / (1.0 + jnp.exp2(x * (-LOG2_E)))


def _rms(x, g):
    return x * lax.rsqrt(jnp.mean(x * x, axis=-1, keepdims=True) + NORM_EPS) * g


def _split_bf16(x):
    hi = x.astype(BF16)
    lo = (x - hi.astype(F32)).astype(BF16)
    return hi, lo


def _to_row_tiles(ref, value):
    t = value.shape[0]
    for j in range(ROW_TILE):
        ref[pl.ds(j, t, stride=ROW_TILE), :] = value[:, j * LANES:(j + 1) * LANES]


def _from_row_tiles(ref, t):
    return jnp.concatenate(
        [ref[pl.ds(j, t, stride=ROW_TILE), :] for j in range(ROW_TILE)], axis=1)


def _tile_rows(row):
    return pl.ds(pl.multiple_of(row * ROW_TILE, ROW_TILE), ROW_TILE)


def _decay_tables(ch):
    t = np.arange(ch)[:, None]
    u = np.arange(ch)[None, :]
    mats = [u <= t, u > t]
    masks = []
    m = 1
    while m < ch:
        right = (t // m) % 2 == 1
        mid = (t // (2 * m)) * (2 * m) + m
        mats.append(np.where(right, (u >= mid) & (u <= t), (u > t) & (u <= mid - 1)))
        masks.append(right & ((u // m) % 2 == 0) & ((u // (2 * m)) == (t // (2 * m))))
        m *= 2
    return (np.concatenate(mats, 0).astype(np.float32),
            np.stack(masks).astype(np.float32))


def _hgrn_chunk(qp, fp, v, gp, lb, hg, tabs, masks_ref, st_ref):
    heads = [slice(h * HEAD_DIM, (h + 1) * HEAD_DIM) for h in range(N_HEADS)]
    q = qp * _sigmoid(qp)
    f = lb + (1.0 - lb) * _sigmoid(fp)
    lf2 = jnp.log(f) * LOG2_E
    k = 1.0 - f
    x = jnp.exp2(_dot(tabs, jnp.concatenate(_split_bf16(lf2), axis=0)))
    eb = x[0:CHUNK]
    esuf = x[CHUNK:2 * CHUNK]
    vb = v.astype(BF16)
    q_in = (q * eb).astype(BF16)
    k_end = (k * esuf).astype(BF16)
    qk = q * k
    st = [st_ref[h] for h in range(N_HEADS)]
    o = [_dot_nt(q_in[:, hs], st[h].astype(BF16)) for h, hs in enumerate(heads)]
    upd = [_dot_tn(vb[:, hs], k_end[:, hs]) for hs in heads]
    s = [jnp.zeros((CHUNK, CHUNK), F32) for _ in heads]
    row = lax.broadcasted_iota(jnp.int32, (CHUNK, HGRN_WIDTH), 0)
    for lvl in range(N_LEVELS):
        m = 1 << lvl
        if m % 8 == 0:
            qk_rows = jnp.concatenate(
                [(q if (r0 // m) % 2 else k)[r0:r0 + m] for r0 in range(0, CHUNK, m)], axis=0)
        else:
            qk_rows = jnp.where((row & m) != 0, q, k)
        z = (qk_rows * x[(2 + lvl) * CHUNK:(3 + lvl) * CHUNK]).astype(BF16)
        mask = masks_ref[lvl] != 0.0
        for h, hs in enumerate(heads):
            s[h] = jnp.where(mask, _dot_nt(z[:, hs], z[:, hs]), s[h])
    outs = []
    for h, hs in enumerate(heads):
        oh = o[h] + jnp.sum(qk[:, hs], axis=-1, keepdims=True) * v[:, hs]
        oh = oh + _dot(s[h].astype(BF16), vb[:, hs])
        st_ref[h] = eb[CHUNK - 1:CHUNK, hs] * st[h] + upd[h]
        oh = oh * lax.rsqrt(jnp.mean(oh * oh, axis=-1, keepdims=True) + NORM_EPS) * hg
        outs.append(oh)
    return jnp.concatenate(outs, axis=1) * (gp * _sigmoid(gp))


def _row_copy(src, src_row, dst, dst_row, sem):
    return pltpu.make_async_copy(src.at[_tile_rows(src_row), :], dst.at[_tile_rows(dst_row), :], sem)


def _mixer_kernel(n_pages,
                  x_ref, g1_ref, win_ref, lb_ref, hg_ref, pw_ref, ps_ref, wout_ref, g2_ref,
                  wrh_ref, wrl_ref, br_ref, tabs_ref, masks_ref, tri_ref, trie_ref,
                  x1_ref, route_ref, meta_ref, xs_ref,
                  proj_scr, mix_scr, st_scr, ubuf_scr, cnt_scr, page_scr, nfree_scr, pexp_scr,
                  h2buf, dst_vmem, dst_smem, tail_vmem, tail_smem, zblk, row_sem, idx_sem, zsem):
    b_i = pl.program_id(0)
    s_i = pl.program_id(1)
    step = b_i * pl.num_programs(1) + s_i
    n_steps = pl.num_programs(0) * pl.num_programs(1)
    slot = lax.rem(step, 2)
    prev = 1 - slot

    @pl.when(s_i == 0)
    def _():
        st_scr[...] = jnp.zeros_like(st_scr)
        ubuf_scr[0:HALO, :] = jnp.zeros((HALO, POOL_WIDTH), F32)

    @pl.when(step == 0)
    def _():
        cnt_scr[...] = jnp.zeros_like(cnt_scr)
        page_scr[...] = jnp.zeros_like(page_scr)
        nfree_scr[...] = jnp.zeros_like(nfree_scr)
        pexp_scr[...] = jnp.full(pexp_scr.shape, -1.0, F32)
        zblk[...] = jnp.zeros_like(zblk)
        h2buf[1] = jnp.zeros(h2buf.shape[1:], F32)

        def spare_body(r, carry):
            for k in range(TOP_K):
                dst_smem[k, r] = n_pages * ROW_BLOCK + r * TOP_K + k
            return carry

        lax.fori_loop(0, MIX_TOKENS, spare_body, 0)

    @pl.when(step > 0)
    def _():
        pltpu.make_async_copy(dst_vmem, dst_smem, idx_sem).wait()
        for k in range(TOP_K):
            pltpu.make_async_copy(h2buf.at[slot], xs_ref.at[pl.ds(0, MIX_TOKENS * ROW_TILE), :],
                                  row_sem).wait()

    @pl.when(step >= 0)
    def _():
        for r in range(FIRST_COPY_TOKENS):
            for k in range(TOP_K):
                _row_copy(h2buf.at[prev], r, xs_ref, dst_smem[k, r], row_sem).start(priority=k % 2)
        h = _rms(x_ref[...], g1_ref[...]).astype(BF16)
        proj_scr[...] = _dot(h, win_ref[...])

    tabs = tabs_ref[...]

    for c in range(MIX_TOKENS // CHUNK):
        rows = slice(c * CHUNK, (c + 1) * CHUNK)
        out = _hgrn_chunk(
            proj_scr[rows, 0:HGRN_WIDTH],
            proj_scr[rows, HGRN_WIDTH:2 * HGRN_WIDTH],
            proj_scr[rows, 2 * HGRN_WIDTH:3 * HGRN_WIDTH],
            proj_scr[rows, 3 * HGRN_WIDTH:4 * HGRN_WIDTH],
            lb_ref[...], hg_ref[...], tabs, masks_ref, st_scr)
        mix_scr[rows, 0:HGRN_WIDTH] = out.astype(BF16)

    ubuf_scr[HALO:HALO + MIX_TOKENS, :] = proj_scr[:, 4 * HGRN_WIDTH:IN_COLS]
    pos = (s_i * MIX_TOKENS + 1
           + lax.broadcasted_iota(jnp.int32, (MIX_TOKENS, 1), 0)).astype(F32)
    for g, win in enumerate(POOL_WINDOWS):
        lo = g * POOL_GW
        ext = ubuf_scr[:, lo:lo + POOL_GW]
        acc = ext
        span = 1
        while span < win:
            acc = acc + pltpu.roll(acc, span, axis=0)
            span *= 2
        u = ext[HALO:]
        d = acc[HALO:] / jnp.minimum(pos, float(win)) - u
        y = _dot(d.astype(BF16), pw_ref[g]) * ps_ref[:, lo:lo + POOL_GW]
        mix_scr[:, HGRN_WIDTH + lo:HGRN_WIDTH + lo + POOL_GW] = y.astype(BF16)
    ubuf_scr[0:HALO, :] = ubuf_scr[MIX_TOKENS:MIX_TOKENS + HALO, :]

    @pl.when(step >= 0)
    def _():
        for r in range(FIRST_COPY_TOKENS, MIX_TOKENS):
            for k in range(TOP_K):
                _row_copy(h2buf.at[prev], r, xs_ref, dst_smem[k, r], row_sem).start(priority=k % 2)
        x1_ref[...] = x_ref[...] + _dot(mix_scr[...], wout_ref[...])

    x1 = x1_ref[...]
    h2 = _rms(x1, g2_ref[...])

    h_hi, h_lo = _split_bf16(h2)
    wrh = wrh_ref[...]
    logits = (_dot_nt(wrh, h_hi) + _dot_nt(wrh, h_lo) + _dot_nt(wrl_ref[...], h_hi)
              + br_ref[...])
    eidx = lax.broadcasted_iota(jnp.int32, (N_EXPERTS, MIX_TOKENS), 0)
    work = logits
    tops, sels = [], []
    for _ in range(TOP_K):
        m = jnp.max(work, axis=0, keepdims=True)
        idx = jnp.min(jnp.where(work == m, eidx, N_EXPERTS), axis=0, keepdims=True)
        sel = eidx == idx
        work = jnp.where(sel, -jnp.inf, work)
        tops.append(m)
        sels.append(sel)
    exps = [jnp.exp(m - tops[0]) for m in tops]
    denom = exps[0] + exps[1] + exps[2] + exps[3]
    gates = [e / denom for e in exps]

    sel_any = jnp.zeros((N_EXPERTS, MIX_TOKENS), F32)
    for sel in sels:
        sel_any = jnp.where(sel, 1.0, sel_any)
    cnt = cnt_scr[...]
    before = _dot(sel_any.astype(BF16), tri_ref[...]) + cnt
    cnt_new = cnt + jnp.sum(sel_any, axis=1, keepdims=True)
    cnt_scr[...] = cnt_new

    shift = int(math.log2(ROW_BLOCK))
    before_i = before.astype(jnp.int32)
    cnt_i = cnt.astype(jnp.int32)
    page_lo = cnt_i >> shift
    starts_page = (cnt_i & (ROW_BLOCK - 1)) == 0
    page_hi = (cnt_new.astype(jnp.int32) - 1) >> shift
    need = (cnt_new > cnt) & (starts_page | (page_hi > page_lo))
    need_f = jnp.where(need, 1.0, 0.0)
    earlier = _dot(trie_ref[...], jnp.broadcast_to(need_f, (N_EXPERTS, LANES)).astype(BF16))
    nfree = nfree_scr[...]
    new_id = nfree + earlier[:, 0:1]
    new_idx = jnp.where(starts_page, page_lo, page_lo + 1)
    cur_page = page_scr[...]
    page = jnp.where(need & ((before_i >> shift) == new_idx), new_id, cur_page)
    dest_all = page * float(ROW_BLOCK) + (before_i & (ROW_BLOCK - 1)).astype(F32)
    page_scr[...] = jnp.where(need, new_id, cur_page)
    nfree_new = nfree + jnp.sum(need_f, axis=0, keepdims=True)
    nfree_scr[...] = nfree_new
    page_lane = lax.broadcasted_iota(jnp.int32, (N_EXPERTS, PAGE_PAD), 1).astype(F32)
    owner = jnp.where(need & (page_lane == new_id), eidx[:, 0:1].astype(F32), -1.0)
    pexp = jnp.maximum(pexp_scr[...], jnp.max(owner, axis=0, keepdims=True))
    pexp_scr[...] = pexp

    dests = []
    for k in range(TOP_K):
        dest = jnp.sum(jnp.where(sels[k], dest_all, 0.0), axis=0, keepdims=True)
        dests.append(dest)
        route_ref[k:k + 1, :] = dest
        route_ref[TOP_K + k:TOP_K + k + 1, :] = gates[k]
    route_ref[2 * TOP_K:, :] = jnp.zeros((ROUTE_ROWS - 2 * TOP_K, MIX_TOKENS), F32)
    used = (cnt_new.astype(jnp.int32) - 1) & (ROW_BLOCK - 1)
    last_page = (cnt_new > 0.0) & (page_lane == page_scr[...])
    tail_rows = jnp.max(jnp.where(last_page, (used + 1).astype(F32), 0.0), axis=0, keepdims=True)
    meta_ref[0:1, :] = pexp
    meta_ref[1:2, :] = jnp.broadcast_to(nfree_new[0:1, :], (1, PAGE_PAD))
    meta_ref[2:3, :] = tail_rows
    meta_ref[3:, :] = jnp.zeros((5, PAGE_PAD), F32)

    _to_row_tiles(h2buf.at[slot], h2)
    dst_vmem[...] = jnp.concatenate(
        dests + [jnp.zeros((8 - TOP_K, MIX_TOKENS), F32)], axis=0).astype(jnp.int32)
    pltpu.make_async_copy(dst_vmem, dst_smem, idx_sem).start()

    @pl.when(step == n_steps - 1)
    def _():
        pltpu.make_async_copy(dst_vmem, dst_smem, idx_sem).wait()
        for k in range(TOP_K):
            pltpu.make_async_copy(h2buf.at[prev], xs_ref.at[pl.ds(0, MIX_TOKENS * ROW_TILE), :],
                                  row_sem).wait()

        def row_body(it, carry):
            for j in range(ROWS_PER_ITER):
                r = it * ROWS_PER_ITER + j
                for k in range(TOP_K):
                    _row_copy(h2buf.at[slot], r, xs_ref, dst_smem[k, r],
                              row_sem).start(priority=k % 2)
            return carry

        lax.fori_loop(0, MIX_TOKENS // ROWS_PER_ITER, row_body, 0)
        for k in range(TOP_K):
            pltpu.make_async_copy(h2buf.at[slot], xs_ref.at[pl.ds(0, MIX_TOKENS * ROW_TILE), :],
                                  row_sem).wait()

        tail_vmem[0] = jnp.broadcast_to(cnt_new, (N_EXPERTS, LANES)).astype(jnp.int32)
        tail_vmem[1] = jnp.broadcast_to(page_scr[...], (N_EXPERTS, LANES)).astype(jnp.int32)
        tail_vmem[2] = jnp.broadcast_to(nfree_new, (N_EXPERTS, LANES)).astype(jnp.int32)
        cp = pltpu.make_async_copy(tail_vmem, tail_smem, idx_sem)
        cp.start()
        cp.wait()

        def zero_copy(first_row, rows):
            return pltpu.make_async_copy(
                zblk.at[pl.ds(0, rows * ROW_TILE), :],
                xs_ref.at[pl.ds(pl.multiple_of(first_row * ROW_TILE, ROW_TILE), rows * ROW_TILE), :],
                zsem)

        def tail_pass(wait):
            def body(e, carry):
                used = tail_smem[0, e, 0] & (ROW_BLOCK - 1)
                first = tail_smem[1, e, 0] * ROW_BLOCK + used
                left = jnp.where(used == 0, 0, ROW_BLOCK - used)
                rows = ZERO_ROWS
                while rows >= 1:
                    @pl.when((left & rows) != 0)
                    def _():
                        cp = zero_copy(first, rows)
                        cp.wait() if wait else cp.start()
                    first = first + jnp.where((left & rows) != 0, rows, 0)
                    rows //= 2
                return carry

            lax.fori_loop(0, N_EXPERTS, body, 0)

        def page_pass(wait):
            def body(p, carry):
                for half in range(ROW_BLOCK // ZERO_ROWS):
                    cp = zero_copy(p * ROW_BLOCK + half * ZERO_ROWS, ZERO_ROWS)
                    cp.wait() if wait else cp.start()
                return carry

            lax.fori_loop(tail_smem[2, 0, 0], n_pages, body, 0)

        tail_pass(False)
        page_pass(False)
        tail_pass(True)
        page_pass(True)


def _mixer(x2d, g1, w_in, lb, hg, pool_w, pool_scale, w_out, g2, wr_hi, wr_lo, br, seq, n_pages):
    n = x2d.shape[0]
    steps_per_seq = seq // MIX_TOKENS
    assert MIX_TOKENS <= ROW_BLOCK and n_pages <= PAGE_PAD
    spare_pages = MIX_TOKENS * TOP_K // ROW_BLOCK
    tabs_np, masks_np = _decay_tables(CHUNK)
    tabs = jnp.asarray(np.concatenate([tabs_np, tabs_np], axis=1), BF16)
    masks = jnp.asarray(masks_np, F32)
    tri = jnp.asarray(np.triu(np.ones((MIX_TOKENS, MIX_TOKENS), np.float32), 1), BF16)
    tri_e = jnp.asarray(np.tril(np.ones((N_EXPERTS, N_EXPERTS), np.float32), -1), BF16)

    def tok(b, s):
        return (b * steps_per_seq + s, 0)

    def tok_col(b, s):
        return (0, b * steps_per_seq + s)

    def const2(b, s):
        return (0, 0)

    def const3(b, s):
        return (0, 0, 0)

    in_specs = [
        pl.BlockSpec((MIX_TOKENS, D_MODEL), tok),
        pl.BlockSpec((1, D_MODEL), const2),
        pl.BlockSpec((D_MODEL, IN_COLS), const2),
        pl.BlockSpec((1, HGRN_WIDTH), const2),
        pl.BlockSpec((1, HEAD_DIM), const2),
        pl.BlockSpec((len(POOL_WINDOWS), POOL_GW, POOL_GW), const3),
        pl.BlockSpec((1, POOL_WIDTH), const2),
        pl.BlockSpec((D_MODEL, D_MODEL), const2),
        pl.BlockSpec((1, D_MODEL), const2),
        pl.BlockSpec((N_EXPERTS, D_MODEL), const2),
        pl.BlockSpec((N_EXPERTS, D_MODEL), const2),
        pl.BlockSpec((N_EXPERTS, 1), const2),
        pl.BlockSpec(tabs.shape, const2),
        pl.BlockSpec(masks.shape, const3),
        pl.BlockSpec(tri.shape, const2),
        pl.BlockSpec(tri_e.shape, const2),
    ]
    out_specs = [
        pl.BlockSpec((MIX_TOKENS, D_MODEL), tok),
        pl.BlockSpec((ROUTE_ROWS, MIX_TOKENS), tok_col),
        pl.BlockSpec((8, PAGE_PAD), const2),
        pl.BlockSpec(memory_space=pl.ANY),
    ]
    out_shape = [
        jax.ShapeDtypeStruct((n, D_MODEL), F32),
        jax.ShapeDtypeStruct((ROUTE_ROWS, n), F32),
        jax.ShapeDtypeStruct((8, PAGE_PAD), F32),
        jax.ShapeDtypeStruct(((n_pages + spare_pages) * ROW_BLOCK * ROW_TILE, LANES), F32),
    ]
    scratch = [
        pltpu.VMEM((MIX_TOKENS, IN_COLS), F32),
        pltpu.VMEM((MIX_TOKENS, D_MODEL), BF16),
        pltpu.VMEM((N_HEADS, HEAD_DIM, HEAD_DIM), F32),
        pltpu.VMEM((HALO + MIX_TOKENS, POOL_WIDTH), F32),
        pltpu.VMEM((N_EXPERTS, 1), F32),
        pltpu.VMEM((N_EXPERTS, 1), F32),
        pltpu.VMEM((N_EXPERTS, 1), F32),
        pltpu.VMEM((1, PAGE_PAD), F32),
        pltpu.VMEM((2, MIX_TOKENS * ROW_TILE, LANES), F32),
        pltpu.VMEM((8, MIX_TOKENS), jnp.int32),
        pltpu.SMEM((8, MIX_TOKENS), jnp.int32),
        pltpu.VMEM((3, N_EXPERTS, LANES), jnp.int32),
        pltpu.SMEM((3, N_EXPERTS, LANES), jnp.int32),
        pltpu.VMEM((ZERO_ROWS * ROW_TILE, LANES), F32),
        pltpu.SemaphoreType.DMA,
        pltpu.SemaphoreType.DMA,
        pltpu.SemaphoreType.DMA,
    ]
    return pl.pallas_call(
        functools.partial(_mixer_kernel, n_pages),
        grid=(n // seq, steps_per_seq),
        in_specs=in_specs,
        out_specs=out_specs,
        out_shape=out_shape,
        scratch_shapes=scratch,
        compiler_params=pltpu.CompilerParams(
            dimension_semantics=("arbitrary", "arbitrary"),
            vmem_limit_bytes=VMEM_LIMIT_BYTES),
        name="mixer",
    )(x2d, g1, w_in, lb, hg, pool_w, pool_scale, w_out, g2, wr_hi, wr_lo, br, tabs, masks, tri,
      tri_e)


def _expert_kernel(blk_e_ref, nused_ref, blk_page_ref, blk_rows_ref, xs_ref, wg_ref, bg_ref,
                   wu_ref, bu_ref, wd_ref, bd_ref, ys_ref, wg_s, wu_s, wd_s):
    i = pl.program_id(0)
    active = i < nused_ref[0]
    half_page = blk_rows_ref[i] <= ROW_BLOCK // 2
    changed = (i == 0) | (blk_e_ref[i] != blk_e_ref[jnp.maximum(i - 1, 0)])

    @pl.when(active & changed)
    def _():
        wg_s[...] = wg_ref[0].astype(BF16)
        wu_s[...] = wu_ref[0].astype(BF16)
        wd_s[...] = wd_ref[0].astype(BF16)

    def ffn(rows):
        x_rows = xs_ref.at[pl.ds(0, rows * ROW_TILE), :]
        xb = _from_row_tiles(x_rows, rows).astype(BF16)
        gt = jnp.minimum(_dot(xb, wg_s[...]) + bg_ref[0], SWIGLU_LIMIT)
        up = jnp.clip(_dot(xb, wu_s[...]) + bu_ref[0], -SWIGLU_LIMIT, SWIGLU_LIMIT)
        act = (up + 1.0) * (gt * _sigmoid(SWIGLU_ALPHA * gt))
        _to_row_tiles(ys_ref.at[pl.ds(0, rows * ROW_TILE), :],
                      _dot(act.astype(BF16), wd_s[...]) + bd_ref[0])

    @pl.when(active & jnp.logical_not(half_page))
    def _():
        ffn(ROW_BLOCK)

    @pl.when(active & half_page)
    def _():
        ffn(ROW_BLOCK // 2)
        ys_ref[pl.ds(ROW_BLOCK // 2 * ROW_TILE, ROW_BLOCK // 2 * ROW_TILE), :] = jnp.zeros(
            (ROW_BLOCK // 2 * ROW_TILE, LANES), F32)

    @pl.when(jnp.logical_not(active))
    def _():
        ys_ref[...] = jnp.zeros_like(ys_ref)


def _experts(blk_e, n_used, blk_page, blk_rows, xs, n_blocks, w_gate, b_gate, w_up, b_up, w_down,
             b_down):
    rows_spec_shape = (ROW_BLOCK * ROW_TILE, LANES)

    def blk(i, be, nu):
        return jnp.minimum(i, nu[0] - 1)

    def row_map(i, be, nu, bp, br):
        return (bp[blk(i, be, nu)], 0)

    def w_map(i, be, nu, bp, br):
        return (be[blk(i, be, nu)], 0, 0)

    w_spec = pl.BlockSpec((1, D_MODEL, D_MODEL), w_map)
    b_spec = pl.BlockSpec((1, 1, D_MODEL), w_map)
    grid_spec = pltpu.PrefetchScalarGridSpec(
        num_scalar_prefetch=4,
        grid=(n_blocks,),
        in_specs=[pl.BlockSpec(rows_spec_shape, row_map),
                  w_spec, b_spec, w_spec, b_spec, w_spec, b_spec],
        out_specs=pl.BlockSpec(rows_spec_shape, lambda i, be, nu, bp, br: (bp[i], 0)),
        scratch_shapes=[pltpu.VMEM((D_MODEL, D_MODEL), BF16)] * 3,
    )
    return pl.pallas_call(
        _expert_kernel,
        grid_spec=grid_spec,
        out_shape=jax.ShapeDtypeStruct((n_blocks * ROW_BLOCK * ROW_TILE, LANES), F32),
        compiler_params=pltpu.CompilerParams(
            dimension_semantics=("arbitrary",),
            vmem_limit_bytes=VMEM_LIMIT_BYTES),
        name="experts",
    )(blk_e, n_used, blk_page, blk_rows, xs, w_gate, b_gate[:, None, :], w_up, b_up[:, None, :],
      w_down, b_down[:, None, :])


def _combine_kernel(final, dest_ref, dest_next_ref, ys_ref, x1_ref, route_ref, gf_ref, out_ref,
                    buf, sems):
    i = pl.program_id(0)
    n_steps = pl.num_programs(0)

    def gather_copy(idx_ref, s, k, r):
        return pltpu.make_async_copy(ys_ref.at[_tile_rows(idx_ref[k * COMBINE_TOKENS + r]), :],
                                     buf.at[s, k, _tile_rows(r), :], sems.at[s])

    def wait_tile(s):
        for k in range(TOP_K):
            pltpu.make_async_copy(ys_ref.at[pl.ds(0, COMBINE_TOKENS * ROW_TILE), :],
                                  buf.at[s, k], sems.at[s]).wait()

    @pl.when(i == 0)
    def _():
        def row_body(it, carry):
            for j in range(ROWS_PER_ITER):
                r = it * ROWS_PER_ITER + j
                for k in range(TOP_K):
                    gather_copy(dest_ref, 0, k, r).start(priority=k % 2)
            return carry

        lax.fori_loop(0, COMBINE_TOKENS // ROWS_PER_ITER, row_body, 0)

    def step_body(s):
        wait_tile(s)
        for r in range(COMBINE_TOKENS):
            for k in range(TOP_K):
                gather_copy(dest_next_ref, 1 - s, k, r).start(priority=k % 2)
        route = route_ref[...].T
        acc = x1_ref[...]
        for k in range(TOP_K):
            acc = acc + route[:, TOP_K + k:TOP_K + k + 1] * _from_row_tiles(
                buf.at[s, k], COMBINE_TOKENS)
        out_ref[...] = _rms(acc, gf_ref[...]) if final else acc

        @pl.when(i == n_steps - 1)
        def _():
            wait_tile(1 - s)

    for s in range(2):
        @pl.when(lax.rem(i, 2) == s)
        def _():
            step_body(s)


def _combine(dest, ys, x1, route, gf, final):
    n = x1.shape[0]
    n_steps = n // COMBINE_TOKENS
    idx_block = (COMBINE_TOKENS * TOP_K,)
    return pl.pallas_call(
        functools.partial(_combine_kernel, final),
        grid=(n_steps,),
        in_specs=[
            pl.BlockSpec(idx_block, lambda i: (i,), memory_space=pltpu.SMEM),
            pl.BlockSpec(idx_block, lambda i: (jnp.minimum(i + 1, n_steps - 1),),
                         memory_space=pltpu.SMEM),
            pl.BlockSpec(memory_space=pl.ANY),
            pl.BlockSpec((COMBINE_TOKENS, D_MODEL), lambda i: (i, 0)),
            pl.BlockSpec((ROUTE_ROWS, COMBINE_TOKENS), lambda i: (0, i)),
            pl.BlockSpec((1, D_MODEL), lambda i: (0, 0)),
        ],
        out_specs=pl.BlockSpec((COMBINE_TOKENS, D_MODEL), lambda i: (i, 0)),
        out_shape=jax.ShapeDtypeStruct((n, D_MODEL), F32),
        scratch_shapes=[
            pltpu.VMEM((2, TOP_K, COMBINE_TOKENS * ROW_TILE, LANES), F32),
            pltpu.SemaphoreType.DMA((2,)),
        ],
        compiler_params=pltpu.CompilerParams(
            dimension_semantics=("arbitrary",),
            vmem_limit_bytes=VMEM_LIMIT_BYTES),
        name="combine",
    )(dest, dest, ys, x1, route, gf)


def kernel(x, norm1_g, w_in, hgrn_lb, hgrn_norm_g, pool_w, pool_scale, w_out, norm2_g,
           router_w, router_b, w_gate, b_gate, w_up, b_up, w_down, b_down, norm_f_g):
    depth = w_in.shape[0]
    batch, seq, _ = x.shape
    n = batch * seq
    n_pages = n * TOP_K // ROW_BLOCK + N_EXPERTS
    lb_all = jnp.cumsum(jax.nn.softmax(hgrn_lb.astype(F32), axis=0), axis=0)

    xt = x.reshape(n, D_MODEL)
    for l in range(depth):
        wr = router_w[l].T
        wr_hi = wr.astype(BF16)
        wr_lo = (wr - wr_hi.astype(F32)).astype(BF16)
        br = router_b[l][:, None]
        x1, route, meta, xs = _mixer(
            xt, norm1_g[l][None, :], w_in[l].astype(BF16), lb_all[l][None, :],
            hgrn_norm_g[l][None, :], pool_w[l].astype(BF16), pool_scale[l][None, :],
            w_out[l].astype(BF16), norm2_g[l][None, :], wr_hi, wr_lo, br, seq, n_pages)

        page_expert = meta[0, :n_pages].astype(jnp.int32)
        n_used = meta[1, 0:1].astype(jnp.int32)
        page_ids = jnp.arange(n_pages, dtype=jnp.int32)
        key = jnp.where(page_expert < 0, N_EXPERTS, page_expert) * PAGE_PAD + page_ids
        rank = jnp.sum((key[None, :] < key[:, None]).astype(jnp.int32), axis=1)
        at_step = rank[None, :] == page_ids[:, None]
        blk_page = jnp.sum(jnp.where(at_step, page_ids[None, :], 0), axis=1)
        blk_e = jnp.clip(jnp.sum(jnp.where(at_step, page_expert[None, :], 0), axis=1),
                         0, N_EXPERTS - 1)
        tail_rows = meta[2, :n_pages].astype(jnp.int32)
        page_rows = jnp.where(tail_rows > 0, tail_rows, ROW_BLOCK)
        blk_rows = jnp.sum(jnp.where(at_step, page_rows[None, :], 0), axis=1)
        dest = route[0:TOP_K].astype(jnp.int32).reshape(TOP_K, n // COMBINE_TOKENS, COMBINE_TOKENS)
        dest = dest.transpose(1, 0, 2).reshape(n * TOP_K)

        ys = _experts(blk_e, n_used, blk_page, blk_rows, xs, n_pages, w_gate[l], b_gate[l],
                      w_up[l], b_up[l], w_down[l], b_down[l])
        xt = _combine(dest, ys, x1, route, norm_f_g[None, :], l == depth - 1)
    return xt.reshape(batch, seq, D_MODEL)
```

```python
import functools
import math

import numpy as np
import jax
import jax.numpy as jnp
from jax import lax
from jax.experimental import pallas as pl
from jax.experimental.pallas import tpu as pltpu

F32 = jnp.float32
BF16 = jnp.bfloat16

D_MODEL = 1024
HGRN_WIDTH = 512
HEAD_DIM = 128
N_HEADS = HGRN_WIDTH // HEAD_DIM
POOL_WIDTH = 512
POOL_WINDOWS = (2, 4, 8, 16)
POOL_GW = POOL_WIDTH // len(POOL_WINDOWS)
IN_COLS = 4 * HGRN_WIDTH + POOL_WIDTH
N_EXPERTS = 32
TOP_K = 4
SWIGLU_LIMIT = 7.0
SWIGLU_ALPHA = 1.702
NORM_EPS = 1e-6

LANES = 128
ROW_TILE = D_MODEL // LANES
VMEM_LIMIT_BYTES = 56 * 1024 * 1024

MIX_TOKENS = 512
CHUNK = 64
N_LEVELS = int(math.log2(CHUNK))
HALO = max(POOL_WINDOWS)
FIRST_COPY_TOKENS = 320
ROW_BLOCK = 512
ZERO_ROWS = ROW_BLOCK // 2
PAGE_PAD = 384
COMBINE_TOKENS = 256
ROWS_PER_ITER = 32
ROUTE_ROWS = 16
LOG2_E = 1.4426950408889634


def _dot(a, b):
    return jnp.dot(a, b, preferred_element_type=F32)


def _dot_nt(a, b):
    return lax.dot_general(a, b, (((1,), (1,)), ((), ())), preferred_element_type=F32)


def _dot_tn(a, b):
    return lax.dot_general(a, b, (((0,), (0,)), ((), ())), preferred_element_type=F32)


def _sigmoid(x):
    return 1.0 / (1.0 + jnp.exp2(x * (-LOG2_E)))


def _rms(x, g):
    return x * lax.rsqrt(jnp.mean(x * x, axis=-1, keepdims=True) + NORM_EPS) * g


def _split_bf16(x):
    hi = x.astype(BF16)
    lo = (x - hi.astype(F32)).astype(BF16)
    return hi, lo


def _to_row_tiles(ref, value):
    t = value.shape[0]
    for j in range(ROW_TILE):
        ref[pl.ds(j, t, stride=ROW_TILE), :] = value[:, j * LANES:(j + 1) * LANES]


def _from_row_tiles(ref, t):
    return jnp.concatenate(
        [ref[pl.ds(j, t, stride=ROW_TILE), :] for j in range(ROW_TILE)], axis=1)


def _tile_rows(row):
    return pl.ds(pl.multiple_of(row * ROW_TILE, ROW_TILE), ROW_TILE)


def _decay_tables(ch):
    t = np.arange(ch)[:, None]
    u = np.arange(ch)[None, :]
    mats = [u <= t, u > t]
    masks = []
    m = 1
    while m < ch:
        right = (t // m) % 2 == 1
        mid = (t // (2 * m)) * (2 * m) + m
        mats.append(np.where(right, (u >= mid) & (u <= t), (u > t) & (u <= mid - 1)))
        masks.append(right & ((u // m) % 2 == 0) & ((u // (2 * m)) == (t // (2 * m))))
        m *= 2
    return (np.concatenate(mats, 0).astype(np.float32),
            np.stack(masks).astype(np.float32))


def _hgrn_chunk(qp, fp, v, gp, lb, hg, tabs, masks_ref, st_ref):
    heads = [slice(h * HEAD_DIM, (h + 1) * HEAD_DIM) for h in range(N_HEADS)]
    q = qp * _sigmoid(qp)
    f = lb + (1.0 - lb) * _sigmoid(fp)
    lf2 = jnp.log(f) * LOG2_E
    k = 1.0 - f
    x = jnp.exp2(_dot(tabs, jnp.concatenate(_split_bf16(lf2), axis=0)))
    eb = x[0:CHUNK]
    esuf = x[CHUNK:2 * CHUNK]
    vb = v.astype(BF16)
    q_in = (q * eb).astype(BF16)
    k_end = (k * esuf).astype(BF16)
    qk = q * k
    st = [st_ref[h] for h in range(N_HEADS)]
    o = [_dot_nt(q_in[:, hs], st[h].astype(BF16)) for h, hs in enumerate(heads)]
    upd = [_dot_tn(vb[:, hs], k_end[:, hs]) for hs in heads]
    s = [jnp.zeros((CHUNK, CHUNK), F32) for _ in heads]
    row = lax.broadcasted_iota(jnp.int32, (CHUNK, HGRN_WIDTH), 0)
    for lvl in range(N_LEVELS):
        m = 1 << lvl
        if m % 8 == 0:
            qk_rows = jnp.concatenate(
                [(q if (r0 // m) % 2 else k)[r0:r0 + m] for r0 in range(0, CHUNK, m)], axis=0)
        else:
            qk_rows = jnp.where((row & m) != 0, q, k)
        z = (qk_rows * x[(2 + lvl) * CHUNK:(3 + lvl) * CHUNK]).astype(BF16)
        mask = masks_ref[lvl] != 0.0
        for h, hs in enumerate(heads):
            s[h] = jnp.where(mask, _dot_nt(z[:, hs], z[:, hs]), s[h])
    outs = []
    for h, hs in enumerate(heads):
        oh = o[h] + jnp.sum(qk[:, hs], axis=-1, keepdims=True) * v[:, hs]
        oh = oh + _dot(s[h].astype(BF16), vb[:, hs])
        st_ref[h] = eb[CHUNK - 1:CHUNK, hs] * st[h] + upd[h]
        oh = oh * lax.rsqrt(jnp.mean(oh * oh, axis=-1, keepdims=True) + NORM_EPS) * hg
        outs.append(oh)
    return jnp.concatenate(outs, axis=1) * (gp * _sigmoid(gp))


def _row_copy(src, src_row, dst, dst_row, sem):
    return pltpu.make_async_copy(src.at[_tile_rows(src_row), :], dst.at[_tile_rows(dst_row), :], sem)


def _mixer_kernel(n_pages,
                  x_ref, g1_ref, win_ref, lb_ref, hg_ref, pw_ref, ps_ref, wout_ref, g2_ref,
                  wrh_ref, wrl_ref, br_ref, tabs_ref, masks_ref, tri_ref, trie_ref,
                  x1_ref, route_ref, meta_ref, xs_ref,
                  proj_scr, mix_scr, st_scr, ubuf_scr, cnt_scr, page_scr, nfree_scr, pexp_scr,
                  h2buf, dst_vmem, dst_smem, tail_vmem, tail_smem, zblk, row_sem, idx_sem, zsem):
    b_i = pl.program_id(0)
    s_i = pl.program_id(1)
    step = b_i * pl.num_programs(1) + s_i
    n_steps = pl.num_programs(0) * pl.num_programs(1)
    slot = lax.rem(step, 2)
    prev = 1 - slot

    @pl.when(s_i == 0)
    def _():
        st_scr[...] = jnp.zeros_like(st_scr)
        ubuf_scr[0:HALO, :] = jnp.zeros((HALO, POOL_WIDTH), F32)

    @pl.when(step == 0)
    def _():
        cnt_scr[...] = jnp.zeros_like(cnt_scr)
        page_scr[...] = jnp.zeros_like(page_scr)
        nfree_scr[...] = jnp.zeros_like(nfree_scr)
        pexp_scr[...] = jnp.full(pexp_scr.shape, -1.0, F32)
        zblk[...] = jnp.zeros_like(zblk)
        h2buf[1] = jnp.zeros(h2buf.shape[1:], F32)

        def spare_body(r, carry):
            for k in range(TOP_K):
                dst_smem[k, r] = n_pages * ROW_BLOCK + r * TOP_K + k
            return carry

        lax.fori_loop(0, MIX_TOKENS, spare_body, 0)

    @pl.when(step > 0)
    def _():
        pltpu.make_async_copy(dst_vmem, dst_smem, idx_sem).wait()
        for k in range(TOP_K):
            pltpu.make_async_copy(h2buf.at[slot], xs_ref.at[pl.ds(0, MIX_TOKENS * ROW_TILE), :],
                                  row_sem).wait()

    @pl.when(step >= 0)
    def _():
        for r in range(FIRST_COPY_TOKENS):
            for k in range(TOP_K):
                _row_copy(h2buf.at[prev], r, xs_ref, dst_smem[k, r], row_sem).start(priority=k % 2)
        h = _rms(x_ref[...], g1_ref[...]).astype(BF16)
        proj_scr[...] = _dot(h, win_ref[...])

    tabs = tabs_ref[...]

    for c in range(MIX_TOKENS // CHUNK):
        rows = slice(c * CHUNK, (c + 1) * CHUNK)
        out = _hgrn_chunk(
            proj_scr[rows, 0:HGRN_WIDTH],
            proj_scr[rows, HGRN_WIDTH:2 * HGRN_WIDTH],
            proj_scr[rows, 2 * HGRN_WIDTH:3 * HGRN_WIDTH],
            proj_scr[rows, 3 * HGRN_WIDTH:4 * HGRN_WIDTH],
            lb_ref[...], hg_ref[...], tabs, masks_ref, st_scr)
        mix_scr[rows, 0:HGRN_WIDTH] = out.astype(BF16)

    ubuf_scr[HALO:HALO + MIX_TOKENS, :] = proj_scr[:, 4 * HGRN_WIDTH:IN_COLS]
    pos = (s_i * MIX_TOKENS + 1
           + lax.broadcasted_iota(jnp.int32, (MIX_TOKENS, 1), 0)).astype(F32)
    for g, win in enumerate(POOL_WINDOWS):
        lo = g * POOL_GW
        ext = ubuf_scr[:, lo:lo + POOL_GW]
        acc = ext
        span = 1
        while span < win:
            acc = acc + pltpu.roll(acc, span, axis=0)
            span *= 2
        u = ext[HALO:]
        d = acc[HALO:] / jnp.minimum(pos, float(win)) - u
        y = _dot(d.astype(BF16), pw_ref[g]) * ps_ref[:, lo:lo + POOL_GW]
        mix_scr[:, HGRN_WIDTH + lo:HGRN_WIDTH + lo + POOL_GW] = y.astype(BF16)
    ubuf_scr[0:HALO, :] = ubuf_scr[MIX_TOKENS:MIX_TOKENS + HALO, :]

    @pl.when(step >= 0)
    def _():
        for r in range(FIRST_COPY_TOKENS, MIX_TOKENS):
            for k in range(TOP_K):
                _row_copy(h2buf.at[prev], r, xs_ref, dst_smem[k, r], row_sem).start(priority=k % 2)
        x1_ref[...] = x_ref[...] + _dot(mix_scr[...], wout_ref[...])

    x1 = x1_ref[...]
    h2 = _rms(x1, g2_ref[...])

    h_hi, h_lo = _split_bf16(h2)
    wrh = wrh_ref[...]
    logits = (_dot_nt(wrh, h_hi) + _dot_nt(wrh, h_lo) + _dot_nt(wrl_ref[...], h_hi)
              + br_ref[...])
    eidx = lax.broadcasted_iota(jnp.int32, (N_EXPERTS, MIX_TOKENS), 0)
    work = logits
    tops, sels = [], []
    for _ in range(TOP_K):
        m = jnp.max(work, axis=0, keepdims=True)
        idx = jnp.min(jnp.where(work == m, eidx, N_EXPERTS), axis=0, keepdims=True)
        sel = eidx == idx
        work = jnp.where(sel, -jnp.inf, work)
        tops.append(m)
        sels.append(sel)
    exps = [jnp.exp(m - tops[0]) for m in tops]
    denom = exps[0] + exps[1] + exps[2] + exps[3]
    gates = [e / denom for e in exps]

    sel_any = jnp.zeros((N_EXPERTS, MIX_TOKENS), F32)
    for sel in sels:
        sel_any = jnp.where(sel, 1.0, sel_any)
    cnt = cnt_scr[...]
    before = _dot(sel_any.astype(BF16), tri_ref[...]) + cnt
    cnt_new = cnt + jnp.sum(sel_any, axis=1, keepdims=True)
    cnt_scr[...] = cnt_new

    shift = int(math.log2(ROW_BLOCK))
    before_i = before.astype(jnp.int32)
    cnt_i = cnt.astype(jnp.int32)
    page_lo = cnt_i >> shift
    starts_page = (cnt_i & (ROW_BLOCK - 1)) == 0
    page_hi = (cnt_new.astype(jnp.int32) - 1) >> shift
    need = (cnt_new > cnt) & (starts_page | (page_hi > page_lo))
    need_f = jnp.where(need, 1.0, 0.0)
    earlier = _dot(trie_ref[...], jnp.broadcast_to(need_f, (N_EXPERTS, LANES)).astype(BF16))
    nfree = nfree_scr[...]
    new_id = nfree + earlier[:, 0:1]
    new_idx = jnp.where(starts_page, page_lo, page_lo + 1)
    cur_page = page_scr[...]
    page = jnp.where(need & ((before_i >> shift) == new_idx), new_id, cur_page)
    dest_all = page * float(ROW_BLOCK) + (before_i & (ROW_BLOCK - 1)).astype(F32)
    page_scr[...] = jnp.where(need, new_id, cur_page)
    nfree_new = nfree + jnp.sum(need_f, axis=0, keepdims=True)
    nfree_scr[...] = nfree_new
    page_lane = lax.broadcasted_iota(jnp.int32, (N_EXPERTS, PAGE_PAD), 1).astype(F32)
    owner = jnp.where(need & (page_lane == new_id), eidx[:, 0:1].astype(F32), -1.0)
    pexp = jnp.maximum(pexp_scr[...], jnp.max(owner, axis=0, keepdims=True))
    pexp_scr[...] = pexp

    dests = []
    for k in range(TOP_K):
        dest = jnp.sum(jnp.where(sels[k], dest_all, 0.0), axis=0, keepdims=True)
        dests.append(dest)
        route_ref[k:k + 1, :] = dest
        route_ref[TOP_K + k:TOP_K + k + 1, :] = gates[k]
    route_ref[2 * TOP_K:, :] = jnp.zeros((ROUTE_ROWS - 2 * TOP_K, MIX_TOKENS), F32)
    used = (cnt_new.astype(jnp.int32) - 1) & (ROW_BLOCK - 1)
    last_page = (cnt_new > 0.0) & (page_lane == page_scr[...])
    tail_rows = jnp.max(jnp.where(last_page, (used + 1).astype(F32), 0.0), axis=0, keepdims=True)
    meta_ref[0:1, :] = pexp
    meta_ref[1:2, :] = jnp.broadcast_to(nfree_new[0:1, :], (1, PAGE_PAD))
    meta_ref[2:3, :] = tail_rows
    meta_ref[3:, :] = jnp.zeros((5, PAGE_PAD), F32)

    _to_row_tiles(h2buf.at[slot], h2)
    dst_vmem[...] = jnp.concatenate(
        dests + [jnp.zeros((8 - TOP_K, MIX_TOKENS), F32)], axis=0).astype(jnp.int32)
    pltpu.make_async_copy(dst_vmem, dst_smem, idx_sem).start()

    @pl.when(step == n_steps - 1)
    def _():
        pltpu.make_async_copy(dst_vmem, dst_smem, idx_sem).wait()
        for k in range(TOP_K):
            pltpu.make_async_copy(h2buf.at[prev], xs_ref.at[pl.ds(0, MIX_TOKENS * ROW_TILE), :],
                                  row_sem).wait()

        def row_body(it, carry):
            for j in range(ROWS_PER_ITER):
                r = it * ROWS_PER_ITER + j
                for k in range(TOP_K):
                    _row_copy(h2buf.at[slot], r, xs_ref, dst_smem[k, r],
                              row_sem).start(priority=k % 2)
            return carry

        lax.fori_loop(0, MIX_TOKENS // ROWS_PER_ITER, row_body, 0)
        for k in range(TOP_K):
            pltpu.make_async_copy(h2buf.at[slot], xs_ref.at[pl.ds(0, MIX_TOKENS * ROW_TILE), :],
                                  row_sem).wait()

        tail_vmem[0] = jnp.broadcast_to(cnt_new, (N_EXPERTS, LANES)).astype(jnp.int32)
        tail_vmem[1] = jnp.broadcast_to(page_scr[...], (N_EXPERTS, LANES)).astype(jnp.int32)
        tail_vmem[2] = jnp.broadcast_to(nfree_new, (N_EXPERTS, LANES)).astype(jnp.int32)
        cp = pltpu.make_async_copy(tail_vmem, tail_smem, idx_sem)
        cp.start()
        cp.wait()

        def zero_copy(first_row, rows):
            return pltpu.make_async_copy(
                zblk.at[pl.ds(0, rows * ROW_TILE), :],
                xs_ref.at[pl.ds(pl.multiple_of(first_row * ROW_TILE, ROW_TILE), rows * ROW_TILE), :],
                zsem)

        def tail_pass(wait):
            def body(e, carry):
                used = tail_smem[0, e, 0] & (ROW_BLOCK - 1)
                first = tail_smem[1, e, 0] * ROW_BLOCK + used
                left = jnp.where(used == 0, 0, ROW_BLOCK - used)
                rows = ZERO_ROWS
                while rows >= 1:
                    @pl.when((left & rows) != 0)
                    def _():
                        cp = zero_copy(first, rows)
                        cp.wait() if wait else cp.start()
                    first = first + jnp.where((left & rows) != 0, rows, 0)
                    rows //= 2
                return carry

            lax.fori_loop(0, N_EXPERTS, body, 0)

        def page_pass(wait):
            def body(p, carry):
                for half in range(ROW_BLOCK // ZERO_ROWS):
                    cp = zero_copy(p * ROW_BLOCK + half * ZERO_ROWS, ZERO_ROWS)
                    cp.wait() if wait else cp.start()
                return carry

            lax.fori_loop(tail_smem[2, 0, 0], n_pages, body, 0)

        tail_pass(False)
        page_pass(False)
        tail_pass(True)
        page_pass(True)


def _mixer(x2d, g1, w_in, lb, hg, pool_w, pool_scale, w_out, g2, wr_hi, wr_lo, br, seq, n_pages):
    n = x2d.shape[0]
    steps_per_seq = seq // MIX_TOKENS
    assert MIX_TOKENS <= ROW_BLOCK and n_pages <= PAGE_PAD
    spare_pages = MIX_TOKENS * TOP_K // ROW_BLOCK
    tabs_np, masks_np = _decay_tables(CHUNK)
    tabs = jnp.asarray(np.concatenate([tabs_np, tabs_np], axis=1), BF16)
    masks = jnp.asarray(masks_np, F32)
    tri = jnp.asarray(np.triu(np.ones((MIX_TOKENS, MIX_TOKENS), np.float32), 1), BF16)
    tri_e = jnp.asarray(np.tril(np.ones((N_EXPERTS, N_EXPERTS), np.float32), -1), BF16)

    def tok(b, s):
        return (b * steps_per_seq + s, 0)

    def tok_col(b, s):
        return (0, b * steps_per_seq + s)

    def const2(b, s):
        return (0, 0)

    def const3(b, s):
        return (0, 0, 0)

    in_specs = [
        pl.BlockSpec((MIX_TOKENS, D_MODEL), tok),
        pl.BlockSpec((1, D_MODEL), const2),
        pl.BlockSpec((D_MODEL, IN_COLS), const2),
        pl.BlockSpec((1, HGRN_WIDTH), const2),
        pl.BlockSpec((1, HEAD_DIM), const2),
        pl.BlockSpec((len(POOL_WINDOWS), POOL_GW, POOL_GW), const3),
        pl.BlockSpec((1, POOL_WIDTH), const2),
        pl.BlockSpec((D_MODEL, D_MODEL), const2),
        pl.BlockSpec((1, D_MODEL), const2),
        pl.BlockSpec((N_EXPERTS, D_MODEL), const2),
        pl.BlockSpec((N_EXPERTS, D_MODEL), const2),
        pl.BlockSpec((N_EXPERTS, 1), const2),
        pl.BlockSpec(tabs.shape, const2),
        pl.BlockSpec(masks.shape, const3),
        pl.BlockSpec(tri.shape, const2),
        pl.BlockSpec(tri_e.shape, const2),
    ]
    out_specs = [
        pl.BlockSpec((MIX_TOKENS, D_MODEL), tok),
        pl.BlockSpec((ROUTE_ROWS, MIX_TOKENS), tok_col),
        pl.BlockSpec((8, PAGE_PAD), const2),
        pl.BlockSpec(memory_space=pl.ANY),
    ]
    out_shape = [
        jax.ShapeDtypeStruct((n, D_MODEL), F32),
        jax.ShapeDtypeStruct((ROUTE_ROWS, n), F32),
        jax.ShapeDtypeStruct((8, PAGE_PAD), F32),
        jax.ShapeDtypeStruct(((n_pages + spare_pages) * ROW_BLOCK * ROW_TILE, LANES), F32),
    ]
    scratch = [
        pltpu.VMEM((MIX_TOKENS, IN_COLS), F32),
        pltpu.VMEM((MIX_TOKENS, D_MODEL), BF16),
        pltpu.VMEM((N_HEADS, HEAD_DIM, HEAD_DIM), F32),
        pltpu.VMEM((HALO + MIX_TOKENS, POOL_WIDTH), F32),
        pltpu.VMEM((N_EXPERTS, 1), F32),
        pltpu.VMEM((N_EXPERTS, 1), F32),
        pltpu.VMEM((N_EXPERTS, 1), F32),
        pltpu.VMEM((1, PAGE_PAD), F32),
        pltpu.VMEM((2, MIX_TOKENS * ROW_TILE, LANES), F32),
        pltpu.VMEM((8, MIX_TOKENS), jnp.int32),
        pltpu.SMEM((8, MIX_TOKENS), jnp.int32),
        pltpu.VMEM((3, N_EXPERTS, LANES), jnp.int32),
        pltpu.SMEM((3, N_EXPERTS, LANES), jnp.int32),
        pltpu.VMEM((ZERO_ROWS * ROW_TILE, LANES), F32),
        pltpu.SemaphoreType.DMA,
        pltpu.SemaphoreType.DMA,
        pltpu.SemaphoreType.DMA,
    ]
    return pl.pallas_call(
        functools.partial(_mixer_kernel, n_pages),
        grid=(n // seq, steps_per_seq),
        in_specs=in_specs,
        out_specs=out_specs,
        out_shape=out_shape,
        scratch_shapes=scratch,
        compiler_params=pltpu.CompilerParams(
            dimension_semantics=("arbitrary", "arbitrary"),
            vmem_limit_bytes=VMEM_LIMIT_BYTES),
        name="mixer",
    )(x2d, g1, w_in, lb, hg, pool_w, pool_scale, w_out, g2, wr_hi, wr_lo, br, tabs, masks, tri,
      tri_e)


def _expert_kernel(blk_e_ref, nused_ref, blk_page_ref, blk_rows_ref, blk_w_ref, xs_ref, wg_ref,
                   bg_ref, wu_ref, bu_ref, wd_ref, bd_ref, ys_ref, wg_s, wu_s, wd_s):
    i = pl.program_id(0)
    active = i < nused_ref[0]
    half_page = blk_rows_ref[i] <= ROW_BLOCK // 2
    changed = (i == 0) | (blk_e_ref[i] != blk_e_ref[jnp.maximum(i - 1, 0)])

    @pl.when(active & changed)
    def _():
        wg_s[...] = wg_ref[0].astype(BF16)
        wu_s[...] = wu_ref[0].astype(BF16)
        wd_s[...] = wd_ref[0].astype(BF16)

    def ffn(rows):
        x_rows = xs_ref.at[pl.ds(0, rows * ROW_TILE), :]
        xb = _from_row_tiles(x_rows, rows).astype(BF16)
        gt = jnp.minimum(_dot(xb, wg_s[...]) + bg_ref[0], SWIGLU_LIMIT)
        up = jnp.clip(_dot(xb, wu_s[...]) + bu_ref[0], -SWIGLU_LIMIT, SWIGLU_LIMIT)
        act = (up + 1.0) * (gt * _sigmoid(SWIGLU_ALPHA * gt))
        _to_row_tiles(ys_ref.at[pl.ds(0, rows * ROW_TILE), :],
                      _dot(act.astype(BF16), wd_s[...]) + bd_ref[0])

    @pl.when(active & jnp.logical_not(half_page))
    def _():
        ffn(ROW_BLOCK)

    @pl.when(active & half_page)
    def _():
        ffn(ROW_BLOCK // 2)
        ys_ref[pl.ds(ROW_BLOCK // 2 * ROW_TILE, ROW_BLOCK // 2 * ROW_TILE), :] = jnp.zeros(
            (ROW_BLOCK // 2 * ROW_TILE, LANES), F32)

    @pl.when(jnp.logical_not(active))
    def _():
        ys_ref[...] = jnp.zeros_like(ys_ref)


def _experts(blk_e, n_used, blk_page, blk_rows, blk_w, xs, n_blocks, w_gate, b_gate, w_up, b_up,
             w_down, b_down):
    rows_spec_shape = (ROW_BLOCK * ROW_TILE, LANES)

    def blk(i, be, nu):
        return jnp.minimum(i, nu[0] - 1)

    def row_map(i, be, nu, bp, br, bw):
        return (bp[blk(i, be, nu)], 0)

    def w_map(i, be, nu, bp, br, bw):
        return (bw[blk(i, be, nu)], 0, 0)

    def b_map(i, be, nu, bp, br, bw):
        return (be[blk(i, be, nu)], 0, 0)

    w_spec = pl.BlockSpec((1, D_MODEL, D_MODEL), w_map)
    b_spec = pl.BlockSpec((1, 1, D_MODEL), b_map)
    grid_spec = pltpu.PrefetchScalarGridSpec(
        num_scalar_prefetch=5,
        grid=(n_blocks,),
        in_specs=[pl.BlockSpec(rows_spec_shape, row_map),
                  w_spec, b_spec, w_spec, b_spec, w_spec, b_spec],
        out_specs=pl.BlockSpec(rows_spec_shape, lambda i, be, nu, bp, br, bw: (bp[i], 0)),
        scratch_shapes=[pltpu.VMEM((D_MODEL, D_MODEL), BF16)] * 3,
    )
    return pl.pallas_call(
        _expert_kernel,
        grid_spec=grid_spec,
        out_shape=jax.ShapeDtypeStruct((n_blocks * ROW_BLOCK * ROW_TILE, LANES), F32),
        compiler_params=pltpu.CompilerParams(
            dimension_semantics=("arbitrary",),
            vmem_limit_bytes=VMEM_LIMIT_BYTES),
        name="experts",
    )(blk_e, n_used, blk_page, blk_rows, blk_w, xs, w_gate, b_gate[:, None, :], w_up, b_up[:, None, :],
      w_down, b_down[:, None, :])


def _combine_kernel(final, dest_ref, dest_next_ref, ys_ref, x1_ref, route_ref, gf_ref, out_ref,
                    buf, sems):
    i = pl.program_id(0)
    n_steps = pl.num_programs(0)

    def gather_copy(idx_ref, s, k, r):
        return pltpu.make_async_copy(ys_ref.at[_tile_rows(idx_ref[k * COMBINE_TOKENS + r]), :],
                                     buf.at[s, k, _tile_rows(r), :], sems.at[s])

    def wait_tile(s):
        for k in range(TOP_K):
            pltpu.make_async_copy(ys_ref.at[pl.ds(0, COMBINE_TOKENS * ROW_TILE), :],
                                  buf.at[s, k], sems.at[s]).wait()

    @pl.when(i == 0)
    def _():
        def row_body(it, carry):
            for j in range(ROWS_PER_ITER):
                r = it * ROWS_PER_ITER + j
                for k in range(TOP_K):
                    gather_copy(dest_ref, 0, k, r).start(priority=k % 2)
            return carry

        lax.fori_loop(0, COMBINE_TOKENS // ROWS_PER_ITER, row_body, 0)

    def step_body(s):
        wait_tile(s)
        for r in range(COMBINE_TOKENS):
            for k in range(TOP_K):
                gather_copy(dest_next_ref, 1 - s, k, r).start(priority=k % 2)
        route = route_ref[...].T
        acc = x1_ref[...]
        for k in range(TOP_K):
            acc = acc + route[:, TOP_K + k:TOP_K + k + 1] * _from_row_tiles(
                buf.at[s, k], COMBINE_TOKENS)
        out_ref[...] = _rms(acc, gf_ref[...]) if final else acc

        @pl.when(i == n_steps - 1)
        def _():
            wait_tile(1 - s)

    for s in range(2):
        @pl.when(lax.rem(i, 2) == s)
        def _():
            step_body(s)


def _combine(dest, ys, x1, route, gf, final):
    n = x1.shape[0]
    n_steps = n // COMBINE_TOKENS
    idx_block = (COMBINE_TOKENS * TOP_K,)
    return pl.pallas_call(
        functools.partial(_combine_kernel, final),
        grid=(n_steps,),
        in_specs=[
            pl.BlockSpec(idx_block, lambda i: (i,), memory_space=pltpu.SMEM),
            pl.BlockSpec(idx_block, lambda i: (jnp.minimum(i + 1, n_steps - 1),),
                         memory_space=pltpu.SMEM),
            pl.BlockSpec(memory_space=pl.ANY),
            pl.BlockSpec((COMBINE_TOKENS, D_MODEL), lambda i: (i, 0)),
            pl.BlockSpec((ROUTE_ROWS, COMBINE_TOKENS), lambda i: (0, i)),
            pl.BlockSpec((1, D_MODEL), lambda i: (0, 0)),
        ],
        out_specs=pl.BlockSpec((COMBINE_TOKENS, D_MODEL), lambda i: (i, 0)),
        out_shape=jax.ShapeDtypeStruct((n, D_MODEL), F32),
        scratch_shapes=[
            pltpu.VMEM((2, TOP_K, COMBINE_TOKENS * ROW_TILE, LANES), F32),
            pltpu.SemaphoreType.DMA((2,)),
        ],
        compiler_params=pltpu.CompilerParams(
            dimension_semantics=("arbitrary",),
            vmem_limit_bytes=VMEM_LIMIT_BYTES),
        name="combine",
    )(dest, dest, ys, x1, route, gf)


def kernel(x, norm1_g, w_in, hgrn_lb, hgrn_norm_g, pool_w, pool_scale, w_out, norm2_g,
           router_w, router_b, w_gate, b_gate, w_up, b_up, w_down, b_down, norm_f_g):
    depth = w_in.shape[0]
    batch, seq, _ = x.shape
    n = batch * seq
    n_pages = n * TOP_K // ROW_BLOCK + N_EXPERTS
    lb_all = jnp.cumsum(jax.nn.softmax(hgrn_lb.astype(F32), axis=0), axis=0)

    xt = x.reshape(n, D_MODEL)
    for l in range(depth):
        wr = router_w[l].T
        wr_hi = wr.astype(BF16)
        wr_lo = (wr - wr_hi.astype(F32)).astype(BF16)
        br = router_b[l][:, None]
        x1, route, meta, xs = _mixer(
            xt, norm1_g[l][None, :], w_in[l].astype(BF16), lb_all[l][None, :],
            hgrn_norm_g[l][None, :], pool_w[l].astype(BF16), pool_scale[l][None, :],
            w_out[l].astype(BF16), norm2_g[l][None, :], wr_hi, wr_lo, br, seq, n_pages)

        page_expert = meta[0, :n_pages].astype(jnp.int32)
        n_used = meta[1, 0:1].astype(jnp.int32)
        page_ids = jnp.arange(n_pages, dtype=jnp.int32)
        key = jnp.where(page_expert < 0, N_EXPERTS, page_expert) * PAGE_PAD + page_ids
        rank = jnp.sum((key[None, :] < key[:, None]).astype(jnp.int32), axis=1)
        at_step = rank[None, :] == page_ids[:, None]
        blk_page = jnp.sum(jnp.where(at_step, page_ids[None, :], 0), axis=1)
        blk_e = jnp.clip(jnp.sum(jnp.where(at_step, page_expert[None, :], 0), axis=1),
                         0, N_EXPERTS - 1)
        tail_rows = meta[2, :n_pages].astype(jnp.int32)
        page_rows = jnp.where(tail_rows > 0, tail_rows, ROW_BLOCK)
        blk_rows = jnp.sum(jnp.where(at_step, page_rows[None, :], 0), axis=1)
        first_page = (page_ids == 0) | (blk_e != jnp.roll(blk_e, 1))
        later = (page_ids < n_used[0])[None, :] & (blk_e[None, :] > blk_e[:, None])
        next_e = jnp.min(jnp.where(later, blk_e[None, :], N_EXPERTS), axis=1)
        blk_w = jnp.where(first_page | (next_e == N_EXPERTS), blk_e, next_e)
        dest = route[0:TOP_K].astype(jnp.int32).reshape(TOP_K, n // COMBINE_TOKENS, COMBINE_TOKENS)
        dest = dest.transpose(1, 0, 2).reshape(n * TOP_K)

        ys = _experts(blk_e, n_used, blk_page, blk_rows, blk_w, xs, n_pages, w_gate[l], b_gate[l],
                      w_up[l], b_up[l], w_down[l], b_down[l])
        xt = _combine(dest, ys, x1, route, norm_f_g[None, :], l == depth - 1)
    return xt.reshape(batch, seq, D_MODEL)
```

```python
import functools
import math

import numpy as np
import jax
import jax.numpy as jnp
from jax import lax
from jax.experimental import pallas as pl
from jax.experimental.pallas import tpu as pltpu

F32 = jnp.float32
BF16 = jnp.bfloat16

D_MODEL = 1024
HGRN_WIDTH = 512
HEAD_DIM = 128
N_HEADS = HGRN_WIDTH // HEAD_DIM
POOL_WIDTH = 512
POOL_WINDOWS = (2, 4, 8, 16)
POOL_GW = POOL_WIDTH // len(POOL_WINDOWS)
IN_COLS = 4 * HGRN_WIDTH + POOL_WIDTH
N_EXPERTS = 32
TOP_K = 4
SWIGLU_LIMIT = 7.0
SWIGLU_ALPHA = 1.702
NORM_EPS = 1e-6

LANES = 128
ROW_TILE = D_MODEL // LANES
VMEM_LIMIT_BYTES = 56 * 1024 * 1024

MIX_TOKENS = 512
CHUNK = 64
N_LEVELS = int(math.log2(CHUNK))
HALO = max(POOL_WINDOWS)
FIRST_COPY_TOKENS = 256
ROW_BLOCK = 512
ZERO_ROWS = ROW_BLOCK // 2
PAGE_PAD = 384
COMBINE_TOKENS = 256
ROWS_PER_ITER = 32
ROUTE_ROWS = 16
LOG2_E = 1.4426950408889634


def _dot(a, b):
    return jnp.dot(a, b, preferred_element_type=F32)


def _dot_nt(a, b):
    return lax.dot_general(a, b, (((1,), (1,)), ((), ())), preferred_element_type=F32)


def _dot_tn(a, b):
    return lax.dot_general(a, b, (((0,), (0,)), ((), ())), preferred_element_type=F32)


def _sigmoid(x):
    return 1.0 / (1.0 + jnp.exp2(x * (-LOG2_E)))


def _rms(x, g):
    return x * lax.rsqrt(jnp.mean(x * x, axis=-1, keepdims=True) + NORM_EPS) * g


def _split_bf16(x):
    hi = x.astype(BF16)
    lo = (x - hi.astype(F32)).astype(BF16)
    return hi, lo


def _to_row_tiles(ref, value):
    t = value.shape[0]
    for j in range(ROW_TILE):
        ref[pl.ds(j, t, stride=ROW_TILE), :] = value[:, j * LANES:(j + 1) * LANES]


def _from_row_tiles(ref, t):
    return jnp.concatenate(
        [ref[pl.ds(j, t, stride=ROW_TILE), :] for j in range(ROW_TILE)], axis=1)


def _tile_rows(row):
    return pl.ds(pl.multiple_of(row * ROW_TILE, ROW_TILE), ROW_TILE)


def _decay_tables(ch):
    t = np.arange(ch)[:, None]
    u = np.arange(ch)[None, :]
    mats = [u <= t, u > t]
    masks = []
    m = 1
    while m < ch:
        right = (t // m) % 2 == 1
        mid = (t // (2 * m)) * (2 * m) + m
        mats.append(np.where(right, (u >= mid) & (u <= t), (u > t) & (u <= mid - 1)))
        masks.append(right & ((u // m) % 2 == 0) & ((u // (2 * m)) == (t // (2 * m))))
        m *= 2
    return (np.concatenate(mats, 0).astype(np.float32),
            np.stack(masks).astype(np.float32))


def _hgrn_chunk(qp, fp, v, gp, lb, hg, tabs, masks_ref, st_ref):
    heads = [slice(h * HEAD_DIM, (h + 1) * HEAD_DIM) for h in range(N_HEADS)]
    q = qp * _sigmoid(qp)
    f = lb + (1.0 - lb) * _sigmoid(fp)
    lf2 = jnp.log(f) * LOG2_E
    k = 1.0 - f
    x = jnp.exp2(_dot(tabs, jnp.concatenate(_split_bf16(lf2), axis=0)))
    eb = x[0:CHUNK]
    esuf = x[CHUNK:2 * CHUNK]
    vb = v.astype(BF16)
    q_in = (q * eb).astype(BF16)
    k_end = (k * esuf).astype(BF16)
    qk = q * k
    st = [st_ref[h] for h in range(N_HEADS)]
    o = [_dot_nt(q_in[:, hs], st[h].astype(BF16)) for h, hs in enumerate(heads)]
    upd = [_dot_tn(vb[:, hs], k_end[:, hs]) for hs in heads]
    s = [jnp.zeros((CHUNK, CHUNK), F32) for _ in heads]
    row = lax.broadcasted_iota(jnp.int32, (CHUNK, HGRN_WIDTH), 0)
    for lvl in range(N_LEVELS):
        m = 1 << lvl
        if m % 8 == 0:
            qk_rows = jnp.concatenate(
                [(q if (r0 // m) % 2 else k)[r0:r0 + m] for r0 in range(0, CHUNK, m)], axis=0)
        else:
            qk_rows = jnp.where((row & m) != 0, q, k)
        z = (qk_rows * x[(2 + lvl) * CHUNK:(3 + lvl) * CHUNK]).astype(BF16)
        mask = masks_ref[lvl] != 0.0
        for h, hs in enumerate(heads):
            s[h] = jnp.where(mask, _dot_nt(z[:, hs], z[:, hs]), s[h])
    outs = []
    for h, hs in enumerate(heads):
        oh = o[h] + jnp.sum(qk[:, hs], axis=-1, keepdims=True) * v[:, hs]
        oh = oh + _dot(s[h].astype(BF16), vb[:, hs])
        st_ref[h] = eb[CHUNK - 1:CHUNK, hs] * st[h] + upd[h]
        oh = oh * lax.rsqrt(jnp.mean(oh * oh, axis=-1, keepdims=True) + NORM_EPS) * hg
        outs.append(oh)
    return jnp.concatenate(outs, axis=1) * (gp * _sigmoid(gp))


def _row_copy(src, src_row, dst, dst_row, sem):
    return pltpu.make_async_copy(src.at[_tile_rows(src_row), :], dst.at[_tile_rows(dst_row), :], sem)


def _mixer_kernel(n_pages,
                  x_ref, g1_ref, win_ref, lb_ref, hg_ref, pw_ref, ps_ref, wout_ref, g2_ref,
                  wrh_ref, wrl_ref, br_ref, tabs_ref, masks_ref, tri_ref, trie_ref,
                  x1_ref, route_ref, meta_ref, xs_ref,
                  proj_scr, mix_scr, st_scr, ubuf_scr, cnt_scr, page_scr, nfree_scr, pexp_scr,
                  h2buf, dst_vmem, dst_smem, tail_vmem, tail_smem, zblk, row_sem, idx_sem, zsem):
    b_i = pl.program_id(0)
    s_i = pl.program_id(1)
    step = b_i * pl.num_programs(1) + s_i
    n_steps = pl.num_programs(0) * pl.num_programs(1)
    slot = lax.rem(step, 2)
    prev = 1 - slot

    @pl.when(s_i == 0)
    def _():
        st_scr[...] = jnp.zeros_like(st_scr)
        ubuf_scr[0:HALO, :] = jnp.zeros((HALO, POOL_WIDTH), F32)

    @pl.when(step == 0)
    def _():
        cnt_scr[...] = jnp.zeros_like(cnt_scr)
        page_scr[...] = jnp.zeros_like(page_scr)
        nfree_scr[...] = jnp.zeros_like(nfree_scr)
        pexp_scr[...] = jnp.full(pexp_scr.shape, -1.0, F32)
        zblk[...] = jnp.zeros_like(zblk)
        h2buf[1] = jnp.zeros(h2buf.shape[1:], F32)

        def spare_body(r, carry):
            for k in range(TOP_K):
                dst_smem[k, r] = n_pages * ROW_BLOCK + r * TOP_K + k
            return carry

        lax.fori_loop(0, MIX_TOKENS, spare_body, 0)

    @pl.when(step > 0)
    def _():
        pltpu.make_async_copy(dst_vmem, dst_smem, idx_sem).wait()
        for k in range(TOP_K):
            pltpu.make_async_copy(h2buf.at[slot], xs_ref.at[pl.ds(0, MIX_TOKENS * ROW_TILE), :],
                                  row_sem).wait()

    @pl.when(step >= 0)
    def _():
        for r in range(FIRST_COPY_TOKENS):
            for k in range(TOP_K):
                _row_copy(h2buf.at[prev], r, xs_ref, dst_smem[k, r], row_sem).start(priority=k % 2)
        h = _rms(x_ref[...], g1_ref[...]).astype(BF16)
        proj_scr[...] = _dot(h, win_ref[...])

    tabs = tabs_ref[...]

    for c in range(MIX_TOKENS // CHUNK):
        rows = slice(c * CHUNK, (c + 1) * CHUNK)
        out = _hgrn_chunk(
            proj_scr[rows, 0:HGRN_WIDTH],
            proj_scr[rows, HGRN_WIDTH:2 * HGRN_WIDTH],
            proj_scr[rows, 2 * HGRN_WIDTH:3 * HGRN_WIDTH],
            proj_scr[rows, 3 * HGRN_WIDTH:4 * HGRN_WIDTH],
            lb_ref[...], hg_ref[...], tabs, masks_ref, st_scr)
        mix_scr[rows, 0:HGRN_WIDTH] = out.astype(BF16)

    ubuf_scr[HALO:HALO + MIX_TOKENS, :] = proj_scr[:, 4 * HGRN_WIDTH:IN_COLS]
    pos = (s_i * MIX_TOKENS + 1
           + lax.broadcasted_iota(jnp.int32, (MIX_TOKENS, 1), 0)).astype(F32)
    for g, win in enumerate(POOL_WINDOWS):
        lo = g * POOL_GW
        ext = ubuf_scr[:, lo:lo + POOL_GW]
        acc = ext
        span = 1
        while span < win:
            acc = acc + pltpu.roll(acc, span, axis=0)
            span *= 2
        u = ext[HALO:]
        d = acc[HALO:] / jnp.minimum(pos, float(win)) - u
        y = _dot(d.astype(BF16), pw_ref[g]) * ps_ref[:, lo:lo + POOL_GW]
        mix_scr[:, HGRN_WIDTH + lo:HGRN_WIDTH + lo + POOL_GW] = y.astype(BF16)
    ubuf_scr[0:HALO, :] = ubuf_scr[MIX_TOKENS:MIX_TOKENS + HALO, :]

    @pl.when(step >= 0)
    def _():
        for r in range(FIRST_COPY_TOKENS, MIX_TOKENS):
            for k in range(TOP_K):
                _row_copy(h2buf.at[prev], r, xs_ref, dst_smem[k, r], row_sem).start(priority=k % 2)
        x1_ref[...] = x_ref[...] + _dot(mix_scr[...], wout_ref[...])

    x1 = x1_ref[...]
    h2 = _rms(x1, g2_ref[...])

    h_hi, h_lo = _split_bf16(h2)
    wrh = wrh_ref[...]
    logits = (_dot_nt(wrh, h_hi) + _dot_nt(wrh, h_lo) + _dot_nt(wrl_ref[...], h_hi)
              + br_ref[...])
    eidx = lax.broadcasted_iota(jnp.int32, (N_EXPERTS, MIX_TOKENS), 0)
    work = logits
    tops, sels = [], []
    for _ in range(TOP_K):
        m = jnp.max(work, axis=0, keepdims=True)
        idx = jnp.min(jnp.where(work == m, eidx, N_EXPERTS), axis=0, keepdims=True)
        sel = eidx == idx
        work = jnp.where(sel, -jnp.inf, work)
        tops.append(m)
        sels.append(sel)
    exps = [jnp.exp(m - tops[0]) for m in tops]
    denom = exps[0] + exps[1] + exps[2] + exps[3]
    gates = [e / denom for e in exps]

    sel_any = jnp.zeros((N_EXPERTS, MIX_TOKENS), F32)
    for sel in sels:
        sel_any = jnp.where(sel, 1.0, sel_any)
    cnt = cnt_scr[...]
    before = _dot(sel_any.astype(BF16), tri_ref[...]) + cnt
    cnt_new = cnt + jnp.sum(sel_any, axis=1, keepdims=True)
    cnt_scr[...] = cnt_new

    shift = int(math.log2(ROW_BLOCK))
    before_i = before.astype(jnp.int32)
    cnt_i = cnt.astype(jnp.int32)
    page_lo = cnt_i >> shift
    starts_page = (cnt_i & (ROW_BLOCK - 1)) == 0
    page_hi = (cnt_new.astype(jnp.int32) - 1) >> shift
    need = (cnt_new > cnt) & (starts_page | (page_hi > page_lo))
    need_f = jnp.where(need, 1.0, 0.0)
    earlier = _dot(trie_ref[...], jnp.broadcast_to(need_f, (N_EXPERTS, LANES)).astype(BF16))
    nfree = nfree_scr[...]
    new_id = nfree + earlier[:, 0:1]
    new_idx = jnp.where(starts_page, page_lo, page_lo + 1)
    cur_page = page_scr[...]
    page = jnp.where(need & ((before_i >> shift) == new_idx), new_id, cur_page)
    dest_all = page * float(ROW_BLOCK) + (before_i & (ROW_BLOCK - 1)).astype(F32)
    page_scr[...] = jnp.where(need, new_id, cur_page)
    nfree_new = nfree + jnp.sum(need_f, axis=0, keepdims=True)
    nfree_scr[...] = nfree_new
    page_lane = lax.broadcasted_iota(jnp.int32, (N_EXPERTS, PAGE_PAD), 1).astype(F32)
    owner = jnp.where(need & (page_lane == new_id), eidx[:, 0:1].astype(F32), -1.0)
    pexp = jnp.maximum(pexp_scr[...], jnp.max(owner, axis=0, keepdims=True))
    pexp_scr[...] = pexp

    dests = []
    for k in range(TOP_K):
        dest = jnp.sum(jnp.where(sels[k], dest_all, 0.0), axis=0, keepdims=True)
        dests.append(dest)
        route_ref[k:k + 1, :] = dest
        route_ref[TOP_K + k:TOP_K + k + 1, :] = gates[k]
    route_ref[2 * TOP_K:, :] = jnp.zeros((ROUTE_ROWS - 2 * TOP_K, MIX_TOKENS), F32)
    used = (cnt_new.astype(jnp.int32) - 1) & (ROW_BLOCK - 1)
    last_page = (cnt_new > 0.0) & (page_lane == page_scr[...])
    tail_rows = jnp.max(jnp.where(last_page, (used + 1).astype(F32), 0.0), axis=0, keepdims=True)
    meta_ref[0:1, :] = pexp
    meta_ref[1:2, :] = jnp.broadcast_to(nfree_new[0:1, :], (1, PAGE_PAD))
    meta_ref[2:3, :] = tail_rows
    meta_ref[3:, :] = jnp.zeros((5, PAGE_PAD), F32)

    _to_row_tiles(h2buf.at[slot], h2)
    dst_vmem[...] = jnp.concatenate(
        dests + [jnp.zeros((8 - TOP_K, MIX_TOKENS), F32)], axis=0).astype(jnp.int32)
    pltpu.make_async_copy(dst_vmem, dst_smem, idx_sem).start()

    @pl.when(step == n_steps - 1)
    def _():
        pltpu.make_async_copy(dst_vmem, dst_smem, idx_sem).wait()
        for k in range(TOP_K):
            pltpu.make_async_copy(h2buf.at[prev], xs_ref.at[pl.ds(0, MIX_TOKENS * ROW_TILE), :],
                                  row_sem).wait()

        def row_body(it, carry):
            for j in range(ROWS_PER_ITER):
                r = it * ROWS_PER_ITER + j
                for k in range(TOP_K):
                    _row_copy(h2buf.at[slot], r, xs_ref, dst_smem[k, r],
                              row_sem).start(priority=k % 2)
            return carry

        lax.fori_loop(0, MIX_TOKENS // ROWS_PER_ITER, row_body, 0)
        for k in range(TOP_K):
            pltpu.make_async_copy(h2buf.at[slot], xs_ref.at[pl.ds(0, MIX_TOKENS * ROW_TILE), :],
                                  row_sem).wait()

        tail_vmem[0] = jnp.broadcast_to(cnt_new, (N_EXPERTS, LANES)).astype(jnp.int32)
        tail_vmem[1] = jnp.broadcast_to(page_scr[...], (N_EXPERTS, LANES)).astype(jnp.int32)
        tail_vmem[2] = jnp.broadcast_to(nfree_new, (N_EXPERTS, LANES)).astype(jnp.int32)
        cp = pltpu.make_async_copy(tail_vmem, tail_smem, idx_sem)
        cp.start()
        cp.wait()

        def zero_copy(first_row, rows):
            return pltpu.make_async_copy(
                zblk.at[pl.ds(0, rows * ROW_TILE), :],
                xs_ref.at[pl.ds(pl.multiple_of(first_row * ROW_TILE, ROW_TILE), rows * ROW_TILE), :],
                zsem)

        def tail_pass(wait):
            def body(e, carry):
                used = tail_smem[0, e, 0] & (ROW_BLOCK - 1)
                first = tail_smem[1, e, 0] * ROW_BLOCK + used
                left = jnp.where(used == 0, 0, ROW_BLOCK - used)
                rows = ZERO_ROWS
                while rows >= 1:
                    @pl.when((left & rows) != 0)
                    def _():
                        cp = zero_copy(first, rows)
                        cp.wait() if wait else cp.start()
                    first = first + jnp.where((left & rows) != 0, rows, 0)
                    rows //= 2
                return carry

            lax.fori_loop(0, N_EXPERTS, body, 0)

        def page_pass(wait):
            def body(p, carry):
                for half in range(ROW_BLOCK // ZERO_ROWS):
                    cp = zero_copy(p * ROW_BLOCK + half * ZERO_ROWS, ZERO_ROWS)
                    cp.wait() if wait else cp.start()
                return carry

            lax.fori_loop(tail_smem[2, 0, 0], n_pages, body, 0)

        tail_pass(False)
        page_pass(False)
        tail_pass(True)
        page_pass(True)


def _mixer(x2d, g1, w_in, lb, hg, pool_w, pool_scale, w_out, g2, wr_hi, wr_lo, br, seq, n_pages):
    n = x2d.shape[0]
    steps_per_seq = seq // MIX_TOKENS
    assert MIX_TOKENS <= ROW_BLOCK and n_pages <= PAGE_PAD
    spare_pages = MIX_TOKENS * TOP_K // ROW_BLOCK
    tabs_np, masks_np = _decay_tables(CHUNK)
    tabs = jnp.asarray(np.concatenate([tabs_np, tabs_np], axis=1), BF16)
    masks = jnp.asarray(masks_np, F32)
    tri = jnp.asarray(np.triu(np.ones((MIX_TOKENS, MIX_TOKENS), np.float32), 1), BF16)
    tri_e = jnp.asarray(np.tril(np.ones((N_EXPERTS, N_EXPERTS), np.float32), -1), BF16)

    def tok(b, s):
        return (b * steps_per_seq + s, 0)

    def tok_col(b, s):
        return (0, b * steps_per_seq + s)

    def const2(b, s):
        return (0, 0)

    def const3(b, s):
        return (0, 0, 0)

    in_specs = [
        pl.BlockSpec((MIX_TOKENS, D_MODEL), tok),
        pl.BlockSpec((1, D_MODEL), const2),
        pl.BlockSpec((D_MODEL, IN_COLS), const2),
        pl.BlockSpec((1, HGRN_WIDTH), const2),
        pl.BlockSpec((1, HEAD_DIM), const2),
        pl.BlockSpec((len(POOL_WINDOWS), POOL_GW, POOL_GW), const3),
        pl.BlockSpec((1, POOL_WIDTH), const2),
        pl.BlockSpec((D_MODEL, D_MODEL), const2),
        pl.BlockSpec((1, D_MODEL), const2),
        pl.BlockSpec((N_EXPERTS, D_MODEL), const2),
        pl.BlockSpec((N_EXPERTS, D_MODEL), const2),
        pl.BlockSpec((N_EXPERTS, 1), const2),
        pl.BlockSpec(tabs.shape, const2),
        pl.BlockSpec(masks.shape, const3),
        pl.BlockSpec(tri.shape, const2),
        pl.BlockSpec(tri_e.shape, const2),
    ]
    out_specs = [
        pl.BlockSpec((MIX_TOKENS, D_MODEL), tok),
        pl.BlockSpec((ROUTE_ROWS, MIX_TOKENS), tok_col),
        pl.BlockSpec((8, PAGE_PAD), const2),
        pl.BlockSpec(memory_space=pl.ANY),
    ]
    out_shape = [
        jax.ShapeDtypeStruct((n, D_MODEL), F32),
        jax.ShapeDtypeStruct((ROUTE_ROWS, n), F32),
        jax.ShapeDtypeStruct((8, PAGE_PAD), F32),
        jax.ShapeDtypeStruct(((n_pages + spare_pages) * ROW_BLOCK * ROW_TILE, LANES), F32),
    ]
    scratch = [
        pltpu.VMEM((MIX_TOKENS, IN_COLS), F32),
        pltpu.VMEM((MIX_TOKENS, D_MODEL), BF16),
        pltpu.VMEM((N_HEADS, HEAD_DIM, HEAD_DIM), F32),
        pltpu.VMEM((HALO + MIX_TOKENS, POOL_WIDTH), F32),
        pltpu.VMEM((N_EXPERTS, 1), F32),
        pltpu.VMEM((N_EXPERTS, 1), F32),
        pltpu.VMEM((N_EXPERTS, 1), F32),
        pltpu.VMEM((1, PAGE_PAD), F32),
        pltpu.VMEM((2, MIX_TOKENS * ROW_TILE, LANES), F32),
        pltpu.VMEM((8, MIX_TOKENS), jnp.int32),
        pltpu.SMEM((8, MIX_TOKENS), jnp.int32),
        pltpu.VMEM((3, N_EXPERTS, LANES), jnp.int32),
        pltpu.SMEM((3, N_EXPERTS, LANES), jnp.int32),
        pltpu.VMEM((ZERO_ROWS * ROW_TILE, LANES), F32),
        pltpu.SemaphoreType.DMA,
        pltpu.SemaphoreType.DMA,
        pltpu.SemaphoreType.DMA,
    ]
    return pl.pallas_call(
        functools.partial(_mixer_kernel, n_pages),
        grid=(n // seq, steps_per_seq),
        in_specs=in_specs,
        out_specs=out_specs,
        out_shape=out_shape,
        scratch_shapes=scratch,
        compiler_params=pltpu.CompilerParams(
            dimension_semantics=("arbitrary", "arbitrary"),
            vmem_limit_bytes=VMEM_LIMIT_BYTES),
        name="mixer",
    )(x2d, g1, w_in, lb, hg, pool_w, pool_scale, w_out, g2, wr_hi, wr_lo, br, tabs, masks, tri,
      tri_e)


def _expert_kernel(blk_e_ref, nused_ref, blk_page_ref, blk_rows_ref, blk_w_ref, xs_ref, wg_ref,
                   bg_ref, wu_ref, bu_ref, wd_ref, bd_ref, ys_ref, wg_s, wu_s, wd_s):
    i = pl.program_id(0)
    active = i < nused_ref[0]
    half_page = blk_rows_ref[i] <= ROW_BLOCK // 2
    changed = (i == 0) | (blk_e_ref[i] != blk_e_ref[jnp.maximum(i - 1, 0)])

    @pl.when(active & changed)
    def _():
        wg_s[...] = wg_ref[0].astype(BF16)
        wu_s[...] = wu_ref[0].astype(BF16)
        wd_s[...] = wd_ref[0].astype(BF16)

    def ffn(rows):
        x_rows = xs_ref.at[pl.ds(0, rows * ROW_TILE), :]
        xb = _from_row_tiles(x_rows, rows).astype(BF16)
        gt = jnp.minimum(_dot(xb, wg_s[...]) + bg_ref[0], SWIGLU_LIMIT)
        up = jnp.clip(_dot(xb, wu_s[...]) + bu_ref[0], -SWIGLU_LIMIT, SWIGLU_LIMIT)
        act = (up + 1.0) * (gt * _sigmoid(SWIGLU_ALPHA * gt))
        _to_row_tiles(ys_ref.at[pl.ds(0, rows * ROW_TILE), :],
                      _dot(act.astype(BF16), wd_s[...]) + bd_ref[0])

    @pl.when(active & jnp.logical_not(half_page))
    def _():
        ffn(ROW_BLOCK)

    @pl.when(active & half_page)
    def _():
        ffn(ROW_BLOCK // 2)
        ys_ref[pl.ds(ROW_BLOCK // 2 * ROW_TILE, ROW_BLOCK // 2 * ROW_TILE), :] = jnp.zeros(
            (ROW_BLOCK // 2 * ROW_TILE, LANES), F32)

    @pl.when(jnp.logical_not(active))
    def _():
        ys_ref[...] = jnp.zeros_like(ys_ref)


def _experts(blk_e, n_used, blk_page, blk_rows, blk_w, xs, n_blocks, w_gate, b_gate, w_up, b_up,
             w_down, b_down):
    rows_spec_shape = (ROW_BLOCK * ROW_TILE, LANES)

    def blk(i, be, nu):
        return jnp.minimum(i, nu[0] - 1)

    def row_map(i, be, nu, bp, br, bw):
        return (bp[blk(i, be, nu)], 0)

    def w_map(i, be, nu, bp, br, bw):
        return (bw[blk(i, be, nu)], 0, 0)

    def b_map(i, be, nu, bp, br, bw):
        return (be[blk(i, be, nu)], 0, 0)

    w_spec = pl.BlockSpec((1, D_MODEL, D_MODEL), w_map)
    b_spec = pl.BlockSpec((1, 1, D_MODEL), b_map)
    grid_spec = pltpu.PrefetchScalarGridSpec(
        num_scalar_prefetch=5,
        grid=(n_blocks,),
        in_specs=[pl.BlockSpec(rows_spec_shape, row_map),
                  w_spec, b_spec, w_spec, b_spec, w_spec, b_spec],
        out_specs=pl.BlockSpec(rows_spec_shape, lambda i, be, nu, bp, br, bw: (bp[i], 0)),
        scratch_shapes=[pltpu.VMEM((D_MODEL, D_MODEL), BF16)] * 3,
    )
    return pl.pallas_call(
        _expert_kernel,
        grid_spec=grid_spec,
        out_shape=jax.ShapeDtypeStruct((n_blocks * ROW_BLOCK * ROW_TILE, LANES), F32),
        compiler_params=pltpu.CompilerParams(
            dimension_semantics=("arbitrary",),
            vmem_limit_bytes=VMEM_LIMIT_BYTES),
        name="experts",
    )(blk_e, n_used, blk_page, blk_rows, blk_w, xs, w_gate, b_gate[:, None, :], w_up, b_up[:, None, :],
      w_down, b_down[:, None, :])


def _combine_kernel(final, dest_ref, dest_next_ref, ys_ref, x1_ref, route_ref, gf_ref, out_ref,
                    buf, sems):
    i = pl.program_id(0)
    n_steps = pl.num_programs(0)

    def gather_copy(idx_ref, s, k, r):
        return pltpu.make_async_copy(ys_ref.at[_tile_rows(idx_ref[k * COMBINE_TOKENS + r]), :],
                                     buf.at[s, k, _tile_rows(r), :], sems.at[s])

    def wait_tile(s):
        for k in range(TOP_K):
            pltpu.make_async_copy(ys_ref.at[pl.ds(0, COMBINE_TOKENS * ROW_TILE), :],
                                  buf.at[s, k], sems.at[s]).wait()

    @pl.when(i == 0)
    def _():
        def row_body(it, carry):
            for j in range(ROWS_PER_ITER):
                r = it * ROWS_PER_ITER + j
                for k in range(TOP_K):
                    gather_copy(dest_ref, 0, k, r).start(priority=k % 2)
            return carry

        lax.fori_loop(0, COMBINE_TOKENS // ROWS_PER_ITER, row_body, 0)

    def step_body(s):
        wait_tile(s)
        for r in range(COMBINE_TOKENS):
            for k in range(TOP_K):
                gather_copy(dest_next_ref, 1 - s, k, r).start(priority=k % 2)
        route = route_ref[...].T
        acc = x1_ref[...]
        for k in range(TOP_K):
            acc = acc + route[:, TOP_K + k:TOP_K + k + 1] * _from_row_tiles(
                buf.at[s, k], COMBINE_TOKENS)
        out_ref[...] = _rms(acc, gf_ref[...]) if final else acc

        @pl.when(i == n_steps - 1)
        def _():
            wait_tile(1 - s)

    for s in range(2):
        @pl.when(lax.rem(i, 2) == s)
        def _():
            step_body(s)


def _combine(dest, ys, x1, route, gf, final):
    n = x1.shape[0]
    n_steps = n // COMBINE_TOKENS
    idx_block = (COMBINE_TOKENS * TOP_K,)
    return pl.pallas_call(
        functools.partial(_combine_kernel, final),
        grid=(n_steps,),
        in_specs=[
            pl.BlockSpec(idx_block, lambda i: (i,), memory_space=pltpu.SMEM),
            pl.BlockSpec(idx_block, lambda i: (jnp.minimum(i + 1, n_steps - 1),),
                         memory_space=pltpu.SMEM),
            pl.BlockSpec(memory_space=pl.ANY),
            pl.BlockSpec((COMBINE_TOKENS, D_MODEL), lambda i: (i, 0)),
            pl.BlockSpec((ROUTE_ROWS, COMBINE_TOKENS), lambda i: (0, i)),
            pl.BlockSpec((1, D_MODEL), lambda i: (0, 0)),
        ],
        out_specs=pl.BlockSpec((COMBINE_TOKENS, D_MODEL), lambda i: (i, 0)),
        out_shape=jax.ShapeDtypeStruct((n, D_MODEL), F32),
        scratch_shapes=[
            pltpu.VMEM((2, TOP_K, COMBINE_TOKENS * ROW_TILE, LANES), F32),
            pltpu.SemaphoreType.DMA((2,)),
        ],
        compiler_params=pltpu.CompilerParams(
            dimension_semantics=("arbitrary",),
            vmem_limit_bytes=VMEM_LIMIT_BYTES),
        name="combine",
    )(dest, dest, ys, x1, route, gf)


def kernel(x, norm1_g, w_in, hgrn_lb, hgrn_norm_g, pool_w, pool_scale, w_out, norm2_g,
           router_w, router_b, w_gate, b_gate, w_up, b_up, w_down, b_down, norm_f_g):
    depth = w_in.shape[0]
    batch, seq, _ = x.shape
    n = batch * seq
    n_pages = n * TOP_K // ROW_BLOCK + N_EXPERTS
    lb_all = jnp.cumsum(jax.nn.softmax(hgrn_lb.astype(F32), axis=0), axis=0)

    xt = x.reshape(n, D_MODEL)
    for l in range(depth):
        wr = router_w[l].T
        wr_hi = wr.astype(BF16)
        wr_lo = (wr - wr_hi.astype(F32)).astype(BF16)
        br = router_b[l][:, None]
        x1, route, meta, xs = _mixer(
            xt, norm1_g[l][None, :], w_in[l].astype(BF16), lb_all[l][None, :],
            hgrn_norm_g[l][None, :], pool_w[l].astype(BF16), pool_scale[l][None, :],
            w_out[l].astype(BF16), norm2_g[l][None, :], wr_hi, wr_lo, br, seq, n_pages)

        page_expert = meta[0, :n_pages].astype(jnp.int32)
        n_used = meta[1, 0:1].astype(jnp.int32)
        page_ids = jnp.arange(n_pages, dtype=jnp.int32)
        key = jnp.where(page_expert < 0, N_EXPERTS, page_expert) * PAGE_PAD + page_ids
        rank = jnp.sum((key[None, :] < key[:, None]).astype(jnp.int32), axis=1)
        at_step = rank[None, :] == page_ids[:, None]
        blk_page = jnp.sum(jnp.where(at_step, page_ids[None, :], 0), axis=1)
        blk_e = jnp.clip(jnp.sum(jnp.where(at_step, page_expert[None, :], 0), axis=1),
                         0, N_EXPERTS - 1)
        tail_rows = meta[2, :n_pages].astype(jnp.int32)
        page_rows = jnp.where(tail_rows > 0, tail_rows, ROW_BLOCK)
        blk_rows = jnp.sum(jnp.where(at_step, page_rows[None, :], 0), axis=1)
        first_page = (page_ids == 0) | (blk_e != jnp.roll(blk_e, 1))
        later = (page_ids < n_used[0])[None, :] & (blk_e[None, :] > blk_e[:, None])
        next_e = jnp.min(jnp.where(later, blk_e[None, :], N_EXPERTS), axis=1)
        blk_w = jnp.where(first_page | (next_e == N_EXPERTS), blk_e, next_e)
        dest = route[0:TOP_K].astype(jnp.int32).reshape(TOP_K, n // COMBINE_TOKENS, COMBINE_TOKENS)
        dest = dest.transpose(1, 0, 2).reshape(n * TOP_K)

        ys = _experts(blk_e, n_used, blk_page, blk_rows, blk_w, xs, n_pages, w_gate[l], b_gate[l],
                      w_up[l], b_up[l], w_down[l], b_down[l])
        xt = _combine(dest, ys, x1, route, norm_f_g[None, :], l == depth - 1)
    return xt.reshape(batch, seq, D_MODEL)
```

```python
import functools
import math

import numpy as np
import jax
import jax.numpy as jnp
from jax import lax
from jax.experimental import pallas as pl
from jax.experimental.pallas import tpu as pltpu

F32 = jnp.float32
BF16 = jnp.bfloat16

D_MODEL = 1024
HGRN_WIDTH = 512
HEAD_DIM = 128
N_HEADS = HGRN_WIDTH // HEAD_DIM
POOL_WIDTH = 512
POOL_WINDOWS = (2, 4, 8, 16)
POOL_GW = POOL_WIDTH // len(POOL_WINDOWS)
IN_COLS = 4 * HGRN_WIDTH + POOL_WIDTH
N_EXPERTS = 32
TOP_K = 4
SWIGLU_LIMIT = 7.0
SWIGLU_ALPHA = 1.702
NORM_EPS = 1e-6

LANES = 128
ROW_TILE = D_MODEL // LANES
VMEM_LIMIT_BYTES = 56 * 1024 * 1024

MIX_TOKENS = 512
CHUNK = 64
N_LEVELS = int(math.log2(CHUNK))
HALO = max(POOL_WINDOWS)
FIRST_COPY_TOKENS = 384
ROW_BLOCK = 512
ZERO_ROWS = ROW_BLOCK // 2
PAGE_PAD = 384
COMBINE_TOKENS = 256
ROWS_PER_ITER = 32
ROUTE_ROWS = 16
LOG2_E = 1.4426950408889634


def _dot(a, b):
    return jnp.dot(a, b, preferred_element_type=F32)


def _dot_nt(a, b):
    return lax.dot_general(a, b, (((1,), (1,)), ((), ())), preferred_element_type=F32)


def _dot_tn(a, b):
    return lax.dot_general(a, b, (((0,), (0,)), ((), ())), preferred_element_type=F32)


def _sigmoid(x):
    return 1.0 / (1.0 + jnp.exp2(x * (-LOG2_E)))


def _rms(x, g):
    return x * lax.rsqrt(jnp.mean(x * x, axis=-1, keepdims=True) + NORM_EPS) * g


def _split_bf16(x):
    hi = x.astype(BF16)
    lo = (x - hi.astype(F32)).astype(BF16)
    return hi, lo


def _to_row_tiles(ref, value):
    t = value.shape[0]
    for j in range(ROW_TILE):
        ref[pl.ds(j, t, stride=ROW_TILE), :] = value[:, j * LANES:(j + 1) * LANES]


def _from_row_tiles(ref, t):
    return jnp.concatenate(
        [ref[pl.ds(j, t, stride=ROW_TILE), :] for j in range(ROW_TILE)], axis=1)


def _tile_rows(row):
    return pl.ds(pl.multiple_of(row * ROW_TILE, ROW_TILE), ROW_TILE)


def _decay_tables(ch):
    t = np.arange(ch)[:, None]
    u = np.arange(ch)[None, :]
    mats = [u <= t, u > t]
    masks = []
    m = 1
    while m < ch:
        right = (t // m) % 2 == 1
        mid = (t // (2 * m)) * (2 * m) + m
        mats.append(np.where(right, (u >= mid) & (u <= t), (u > t) & (u <= mid - 1)))
        masks.append(right & ((u // m) % 2 == 0) & ((u // (2 * m)) == (t // (2 * m))))
        m *= 2
    return (np.concatenate(mats, 0).astype(np.float32),
            np.stack(masks).astype(np.float32))


def _hgrn_chunk(qp, fp, v, gp, lb, hg, tabs, masks_ref, st_ref):
    heads = [slice(h * HEAD_DIM, (h + 1) * HEAD_DIM) for h in range(N_HEADS)]
    q = qp * _sigmoid(qp)
    f = lb + (1.0 - lb) * _sigmoid(fp)
    lf2 = jnp.log(f) * LOG2_E
    k = 1.0 - f
    x = jnp.exp2(_dot(tabs, jnp.concatenate(_split_bf16(lf2), axis=0)))
    eb = x[0:CHUNK]
    esuf = x[CHUNK:2 * CHUNK]
    vb = v.astype(BF16)
    q_in = (q * eb).astype(BF16)
    k_end = (k * esuf).astype(BF16)
    qk = q * k
    st = [st_ref[h] for h in range(N_HEADS)]
    o = [_dot_nt(q_in[:, hs], st[h].astype(BF16)) for h, hs in enumerate(heads)]
    upd = [_dot_tn(vb[:, hs], k_end[:, hs]) for hs in heads]
    s = [jnp.zeros((CHUNK, CHUNK), F32) for _ in heads]
    row = lax.broadcasted_iota(jnp.int32, (CHUNK, HGRN_WIDTH), 0)
    for lvl in range(N_LEVELS):
        m = 1 << lvl
        if m % 8 == 0:
            qk_rows = jnp.concatenate(
                [(q if (r0 // m) % 2 else k)[r0:r0 + m] for r0 in range(0, CHUNK, m)], axis=0)
        else:
            qk_rows = jnp.where((row & m) != 0, q, k)
        z = (qk_rows * x[(2 + lvl) * CHUNK:(3 + lvl) * CHUNK]).astype(BF16)
        mask = masks_ref[lvl] != 0.0
        for h, hs in enumerate(heads):
            s[h] = jnp.where(mask, _dot_nt(z[:, hs], z[:, hs]), s[h])
    outs = []
    for h, hs in enumerate(heads):
        oh = o[h] + jnp.sum(qk[:, hs], axis=-1, keepdims=True) * v[:, hs]
        oh = oh + _dot(s[h].astype(BF16), vb[:, hs])
        st_ref[h] = eb[CHUNK - 1:CHUNK, hs] * st[h] + upd[h]
        oh = oh * lax.rsqrt(jnp.mean(oh * oh, axis=-1, keepdims=True) + NORM_EPS) * hg
        outs.append(oh)
    return jnp.concatenate(outs, axis=1) * (gp * _sigmoid(gp))


def _row_copy(src, src_row, dst, dst_row, sem):
    return pltpu.make_async_copy(src.at[_tile_rows(src_row), :], dst.at[_tile_rows(dst_row), :], sem)


def _mixer_kernel(n_pages,
                  x_ref, g1_ref, win_ref, lb_ref, hg_ref, pw_ref, ps_ref, wout_ref, g2_ref,
                  wrh_ref, wrl_ref, br_ref, tabs_ref, masks_ref, tri_ref, trie_ref,
                  x1_ref, route_ref, meta_ref, xs_ref,
                  proj_scr, mix_scr, st_scr, ubuf_scr, cnt_scr, page_scr, nfree_scr, pexp_scr,
                  h2buf, dst_vmem, dst_smem, tail_vmem, tail_smem, zblk, row_sem, idx_sem, zsem):
    b_i = pl.program_id(0)
    s_i = pl.program_id(1)
    step = b_i * pl.num_programs(1) + s_i
    n_steps = pl.num_programs(0) * pl.num_programs(1)
    slot = lax.rem(step, 2)
    prev = 1 - slot

    @pl.when(s_i == 0)
    def _():
        st_scr[...] = jnp.zeros_like(st_scr)
        ubuf_scr[0:HALO, :] = jnp.zeros((HALO, POOL_WIDTH), F32)

    @pl.when(step == 0)
    def _():
        cnt_scr[...] = jnp.zeros_like(cnt_scr)
        page_scr[...] = jnp.zeros_like(page_scr)
        nfree_scr[...] = jnp.zeros_like(nfree_scr)
        pexp_scr[...] = jnp.full(pexp_scr.shape, -1.0, F32)
        zblk[...] = jnp.zeros_like(zblk)
        h2buf[1] = jnp.zeros(h2buf.shape[1:], F32)

        def spare_body(r, carry):
            for k in range(TOP_K):
                dst_smem[k, r] = n_pages * ROW_BLOCK + r * TOP_K + k
            return carry

        lax.fori_loop(0, MIX_TOKENS, spare_body, 0)

    @pl.when(step > 0)
    def _():
        pltpu.make_async_copy(dst_vmem, dst_smem, idx_sem).wait()
        for k in range(TOP_K):
            pltpu.make_async_copy(h2buf.at[slot], xs_ref.at[pl.ds(0, MIX_TOKENS * ROW_TILE), :],
                                  row_sem).wait()

    @pl.when(step >= 0)
    def _():
        for r in range(FIRST_COPY_TOKENS):
            for k in range(TOP_K):
                _row_copy(h2buf.at[prev], r, xs_ref, dst_smem[k, r], row_sem).start(priority=k % 2)
        h = _rms(x_ref[...], g1_ref[...]).astype(BF16)
        proj_scr[...] = _dot(h, win_ref[...])

    tabs = tabs_ref[...]

    for c in range(MIX_TOKENS // CHUNK):
        rows = slice(c * CHUNK, (c + 1) * CHUNK)
        out = _hgrn_chunk(
            proj_scr[rows, 0:HGRN_WIDTH],
            proj_scr[rows, HGRN_WIDTH:2 * HGRN_WIDTH],
            proj_scr[rows, 2 * HGRN_WIDTH:3 * HGRN_WIDTH],
            proj_scr[rows, 3 * HGRN_WIDTH:4 * HGRN_WIDTH],
            lb_ref[...], hg_ref[...], tabs, masks_ref, st_scr)
        mix_scr[rows, 0:HGRN_WIDTH] = out.astype(BF16)

    ubuf_scr[HALO:HALO + MIX_TOKENS, :] = proj_scr[:, 4 * HGRN_WIDTH:IN_COLS]
    pos = (s_i * MIX_TOKENS + 1
           + lax.broadcasted_iota(jnp.int32, (MIX_TOKENS, 1), 0)).astype(F32)
    for g, win in enumerate(POOL_WINDOWS):
        lo = g * POOL_GW
        ext = ubuf_scr[:, lo:lo + POOL_GW]
        acc = ext
        span = 1
        while span < win:
            acc = acc + pltpu.roll(acc, span, axis=0)
            span *= 2
        u = ext[HALO:]
        d = acc[HALO:] / jnp.minimum(pos, float(win)) - u
        y = _dot(d.astype(BF16), pw_ref[g]) * ps_ref[:, lo:lo + POOL_GW]
        mix_scr[:, HGRN_WIDTH + lo:HGRN_WIDTH + lo + POOL_GW] = y.astype(BF16)
    ubuf_scr[0:HALO, :] = ubuf_scr[MIX_TOKENS:MIX_TOKENS + HALO, :]

    @pl.when(step >= 0)
    def _():
        for r in range(FIRST_COPY_TOKENS, MIX_TOKENS):
            for k in range(TOP_K):
                _row_copy(h2buf.at[prev], r, xs_ref, dst_smem[k, r], row_sem).start(priority=k % 2)
        x1_ref[...] = x_ref[...] + _dot(mix_scr[...], wout_ref[...])

    x1 = x1_ref[...]
    h2 = _rms(x1, g2_ref[...])

    h_hi, h_lo = _split_bf16(h2)
    wrh = wrh_ref[...]
    logits = (_dot_nt(wrh, h_hi) + _dot_nt(wrh, h_lo) + _dot_nt(wrl_ref[...], h_hi)
              + br_ref[...])
    eidx = lax.broadcasted_iota(jnp.int32, (N_EXPERTS, MIX_TOKENS), 0)
    work = logits
    tops, sels = [], []
    for _ in range(TOP_K):
        m = jnp.max(work, axis=0, keepdims=True)
        idx = jnp.min(jnp.where(work == m, eidx, N_EXPERTS), axis=0, keepdims=True)
        sel = eidx == idx
        work = jnp.where(sel, -jnp.inf, work)
        tops.append(m)
        sels.append(sel)
    exps = [jnp.exp(m - tops[0]) for m in tops]
    denom = exps[0] + exps[1] + exps[2] + exps[3]
    gates = [e / denom for e in exps]

    sel_any = jnp.zeros((N_EXPERTS, MIX_TOKENS), F32)
    for sel in sels:
        sel_any = jnp.where(sel, 1.0, sel_any)
    cnt = cnt_scr[...]
    before = _dot(sel_any.astype(BF16), tri_ref[...]) + cnt
    cnt_new = cnt + jnp.sum(sel_any, axis=1, keepdims=True)
    cnt_scr[...] = cnt_new

    shift = int(math.log2(ROW_BLOCK))
    before_i = before.astype(jnp.int32)
    cnt_i = cnt.astype(jnp.int32)
    page_lo = cnt_i >> shift
    starts_page = (cnt_i & (ROW_BLOCK - 1)) == 0
    page_hi = (cnt_new.astype(jnp.int32) - 1) >> shift
    need = (cnt_new > cnt) & (starts_page | (page_hi > page_lo))
    need_f = jnp.where(need, 1.0, 0.0)
    earlier = _dot(trie_ref[...], jnp.broadcast_to(need_f, (N_EXPERTS, LANES)).astype(BF16))
    nfree = nfree_scr[...]
    new_id = nfree + earlier[:, 0:1]
    new_idx = jnp.where(starts_page, page_lo, page_lo + 1)
    cur_page = page_scr[...]
    page = jnp.where(need & ((before_i >> shift) == new_idx), new_id, cur_page)
    dest_all = page * float(ROW_BLOCK) + (before_i & (ROW_BLOCK - 1)).astype(F32)
    page_scr[...] = jnp.where(need, new_id, cur_page)
    nfree_new = nfree + jnp.sum(need_f, axis=0, keepdims=True)
    nfree_scr[...] = nfree_new
    page_lane = lax.broadcasted_iota(jnp.int32, (N_EXPERTS, PAGE_PAD), 1).astype(F32)
    owner = jnp.where(need & (page_lane == new_id), eidx[:, 0:1].astype(F32), -1.0)
    pexp = jnp.maximum(pexp_scr[...], jnp.max(owner, axis=0, keepdims=True))
    pexp_scr[...] = pexp

    dests = []
    for k in range(TOP_K):
        dest = jnp.sum(jnp.where(sels[k], dest_all, 0.0), axis=0, keepdims=True)
        dests.append(dest)
        route_ref[k:k + 1, :] = dest
        route_ref[TOP_K + k:TOP_K + k + 1, :] = gates[k]
    route_ref[2 * TOP_K:, :] = jnp.zeros((ROUTE_ROWS - 2 * TOP_K, MIX_TOKENS), F32)
    used = (cnt_new.astype(jnp.int32) - 1) & (ROW_BLOCK - 1)
    last_page = (cnt_new > 0.0) & (page_lane == page_scr[...])
    tail_rows = jnp.max(jnp.where(last_page, (used + 1).astype(F32), 0.0), axis=0, keepdims=True)
    meta_ref[0:1, :] = pexp
    meta_ref[1:2, :] = jnp.broadcast_to(nfree_new[0:1, :], (1, PAGE_PAD))
    meta_ref[2:3, :] = tail_rows
    meta_ref[3:, :] = jnp.zeros((5, PAGE_PAD), F32)

    _to_row_tiles(h2buf.at[slot], h2)
    dst_vmem[...] = jnp.concatenate(
        dests + [jnp.zeros((8 - TOP_K, MIX_TOKENS), F32)], axis=0).astype(jnp.int32)
    pltpu.make_async_copy(dst_vmem, dst_smem, idx_sem).start()

    @pl.when(step == n_steps - 1)
    def _():
        pltpu.make_async_copy(dst_vmem, dst_smem, idx_sem).wait()
        for k in range(TOP_K):
            pltpu.make_async_copy(h2buf.at[prev], xs_ref.at[pl.ds(0, MIX_TOKENS * ROW_TILE), :],
                                  row_sem).wait()

        def row_body(it, carry):
            for j in range(ROWS_PER_ITER):
                r = it * ROWS_PER_ITER + j
                for k in range(TOP_K):
                    _row_copy(h2buf.at[slot], r, xs_ref, dst_smem[k, r],
                              row_sem).start(priority=k % 2)
            return carry

        lax.fori_loop(0, MIX_TOKENS // ROWS_PER_ITER, row_body, 0)
        for k in range(TOP_K):
            pltpu.make_async_copy(h2buf.at[slot], xs_ref.at[pl.ds(0, MIX_TOKENS * ROW_TILE), :],
                                  row_sem).wait()

        tail_vmem[0] = jnp.broadcast_to(cnt_new, (N_EXPERTS, LANES)).astype(jnp.int32)
        tail_vmem[1] = jnp.broadcast_to(page_scr[...], (N_EXPERTS, LANES)).astype(jnp.int32)
        tail_vmem[2] = jnp.broadcast_to(nfree_new, (N_EXPERTS, LANES)).astype(jnp.int32)
        cp = pltpu.make_async_copy(tail_vmem, tail_smem, idx_sem)
        cp.start()
        cp.wait()

        def zero_copy(first_row, rows):
            return pltpu.make_async_copy(
                zblk.at[pl.ds(0, rows * ROW_TILE), :],
                xs_ref.at[pl.ds(pl.multiple_of(first_row * ROW_TILE, ROW_TILE), rows * ROW_TILE), :],
                zsem)

        def tail_pass(wait):
            def body(e, carry):
                used = tail_smem[0, e, 0] & (ROW_BLOCK - 1)
                first = tail_smem[1, e, 0] * ROW_BLOCK + used
                left = jnp.where(used == 0, 0, ROW_BLOCK - used)
                rows = ZERO_ROWS
                while rows >= 1:
                    @pl.when((left & rows) != 0)
                    def _():
                        cp = zero_copy(first, rows)
                        cp.wait() if wait else cp.start()
                    first = first + jnp.where((left & rows) != 0, rows, 0)
                    rows //= 2
                return carry

            lax.fori_loop(0, N_EXPERTS, body, 0)

        def page_pass(wait):
            def body(p, carry):
                for half in range(ROW_BLOCK // ZERO_ROWS):
                    cp = zero_copy(p * ROW_BLOCK + half * ZERO_ROWS, ZERO_ROWS)
                    cp.wait() if wait else cp.start()
                return carry

            lax.fori_loop(tail_smem[2, 0, 0], n_pages, body, 0)

        tail_pass(False)
        page_pass(False)
        tail_pass(True)
        page_pass(True)


def _mixer(x2d, g1, w_in, lb, hg, pool_w, pool_scale, w_out, g2, wr_hi, wr_lo, br, seq, n_pages):
    n = x2d.shape[0]
    steps_per_seq = seq // MIX_TOKENS
    assert MIX_TOKENS <= ROW_BLOCK and n_pages <= PAGE_PAD
    spare_pages = MIX_TOKENS * TOP_K // ROW_BLOCK
    tabs_np, masks_np = _decay_tables(CHUNK)
    tabs = jnp.asarray(np.concatenate([tabs_np, tabs_np], axis=1), BF16)
    masks = jnp.asarray(masks_np, F32)
    tri = jnp.asarray(np.triu(np.ones((MIX_TOKENS, MIX_TOKENS), np.float32), 1), BF16)
    tri_e = jnp.asarray(np.tril(np.ones((N_EXPERTS, N_EXPERTS), np.float32), -1), BF16)

    def tok(b, s):
        return (b * steps_per_seq + s, 0)

    def tok_col(b, s):
        return (0, b * steps_per_seq + s)

    def const2(b, s):
        return (0, 0)

    def const3(b, s):
        return (0, 0, 0)

    in_specs = [
        pl.BlockSpec((MIX_TOKENS, D_MODEL), tok),
        pl.BlockSpec((1, D_MODEL), const2),
        pl.BlockSpec((D_MODEL, IN_COLS), const2),
        pl.BlockSpec((1, HGRN_WIDTH), const2),
        pl.BlockSpec((1, HEAD_DIM), const2),
        pl.BlockSpec((len(POOL_WINDOWS), POOL_GW, POOL_GW), const3),
        pl.BlockSpec((1, POOL_WIDTH), const2),
        pl.BlockSpec((D_MODEL, D_MODEL), const2),
        pl.BlockSpec((1, D_MODEL), const2),
        pl.BlockSpec((N_EXPERTS, D_MODEL), const2),
        pl.BlockSpec((N_EXPERTS, D_MODEL), const2),
        pl.BlockSpec((N_EXPERTS, 1), const2),
        pl.BlockSpec(tabs.shape, const2),
        pl.BlockSpec(masks.shape, const3),
        pl.BlockSpec(tri.shape, const2),
        pl.BlockSpec(tri_e.shape, const2),
    ]
    out_specs = [
        pl.BlockSpec((MIX_TOKENS, D_MODEL), tok),
        pl.BlockSpec((ROUTE_ROWS, MIX_TOKENS), tok_col),
        pl.BlockSpec((8, PAGE_PAD), const2),
        pl.BlockSpec(memory_space=pl.ANY),
    ]
    out_shape = [
        jax.ShapeDtypeStruct((n, D_MODEL), F32),
        jax.ShapeDtypeStruct((ROUTE_ROWS, n), F32),
        jax.ShapeDtypeStruct((8, PAGE_PAD), F32),
        jax.ShapeDtypeStruct(((n_pages + spare_pages) * ROW_BLOCK * ROW_TILE, LANES), F32),
    ]
    scratch = [
        pltpu.VMEM((MIX_TOKENS, IN_COLS), F32),
        pltpu.VMEM((MIX_TOKENS, D_MODEL), BF16),
        pltpu.VMEM((N_HEADS, HEAD_DIM, HEAD_DIM), F32),
        pltpu.VMEM((HALO + MIX_TOKENS, POOL_WIDTH), F32),
        pltpu.VMEM((N_EXPERTS, 1), F32),
        pltpu.VMEM((N_EXPERTS, 1), F32),
        pltpu.VMEM((N_EXPERTS, 1), F32),
        pltpu.VMEM((1, PAGE_PAD), F32),
        pltpu.VMEM((2, MIX_TOKENS * ROW_TILE, LANES), F32),
        pltpu.VMEM((8, MIX_TOKENS), jnp.int32),
        pltpu.SMEM((8, MIX_TOKENS), jnp.int32),
        pltpu.VMEM((3, N_EXPERTS, LANES), jnp.int32),
        pltpu.SMEM((3, N_EXPERTS, LANES), jnp.int32),
        pltpu.VMEM((ZERO_ROWS * ROW_TILE, LANES), F32),
        pltpu.SemaphoreType.DMA,
        pltpu.SemaphoreType.DMA,
        pltpu.SemaphoreType.DMA,
    ]
    return pl.pallas_call(
        functools.partial(_mixer_kernel, n_pages),
        grid=(n // seq, steps_per_seq),
        in_specs=in_specs,
        out_specs=out_specs,
        out_shape=out_shape,
        scratch_shapes=scratch,
        compiler_params=pltpu.CompilerParams(
            dimension_semantics=("arbitrary", "arbitrary"),
            vmem_limit_bytes=VMEM_LIMIT_BYTES),
        name="mixer",
    )(x2d, g1, w_in, lb, hg, pool_w, pool_scale, w_out, g2, wr_hi, wr_lo, br, tabs, masks, tri,
      tri_e)


def _expert_kernel(blk_e_ref, nused_ref, blk_page_ref, blk_rows_ref, blk_w_ref, xs_ref, wg_ref,
                   bg_ref, wu_ref, bu_ref, wd_ref, bd_ref, ys_ref, wg_s, wu_s, wd_s):
    i = pl.program_id(0)
    active = i < nused_ref[0]
    half_page = blk_rows_ref[i] <= ROW_BLOCK // 2
    changed = (i == 0) | (blk_e_ref[i] != blk_e_ref[jnp.maximum(i - 1, 0)])

    @pl.when(active & changed)
    def _():
        wg_s[...] = wg_ref[0].astype(BF16)
        wu_s[...] = wu_ref[0].astype(BF16)
        wd_s[...] = wd_ref[0].astype(BF16)

    def ffn(rows):
        x_rows = xs_ref.at[pl.ds(0, rows * ROW_TILE), :]
        xb = _from_row_tiles(x_rows, rows).astype(BF16)
        gt = jnp.minimum(_dot(xb, wg_s[...]) + bg_ref[0], SWIGLU_LIMIT)
        up = jnp.clip(_dot(xb, wu_s[...]) + bu_ref[0], -SWIGLU_LIMIT, SWIGLU_LIMIT)
        act = (up + 1.0) * (gt * _sigmoid(SWIGLU_ALPHA * gt))
        _to_row_tiles(ys_ref.at[pl.ds(0, rows * ROW_TILE), :],
                      _dot(act.astype(BF16), wd_s[...]) + bd_ref[0])

    @pl.when(active & jnp.logical_not(half_page))
    def _():
        ffn(ROW_BLOCK)

    @pl.when(active & half_page)
    def _():
        ffn(ROW_BLOCK // 2)
        ys_ref[pl.ds(ROW_BLOCK // 2 * ROW_TILE, ROW_BLOCK // 2 * ROW_TILE), :] = jnp.zeros(
            (ROW_BLOCK // 2 * ROW_TILE, LANES), F32)

    @pl.when(jnp.logical_not(active))
    def _():
        ys_ref[...] = jnp.zeros_like(ys_ref)


def _experts(blk_e, n_used, blk_page, blk_rows, blk_w, xs, n_blocks, w_gate, b_gate, w_up, b_up,
             w_down, b_down):
    rows_spec_shape = (ROW_BLOCK * ROW_TILE, LANES)

    def blk(i, be, nu):
        return jnp.minimum(i, nu[0] - 1)

    def row_map(i, be, nu, bp, br, bw):
        return (bp[blk(i, be, nu)], 0)

    def w_map(i, be, nu, bp, br, bw):
        return (bw[blk(i, be, nu)], 0, 0)

    def b_map(i, be, nu, bp, br, bw):
        return (be[blk(i, be, nu)], 0, 0)

    w_spec = pl.BlockSpec((1, D_MODEL, D_MODEL), w_map)
    b_spec = pl.BlockSpec((1, 1, D_MODEL), b_map)
    grid_spec = pltpu.PrefetchScalarGridSpec(
        num_scalar_prefetch=5,
        grid=(n_blocks,),
        in_specs=[pl.BlockSpec(rows_spec_shape, row_map),
                  w_spec, b_spec, w_spec, b_spec, w_spec, b_spec],
        out_specs=pl.BlockSpec(rows_spec_shape, lambda i, be, nu, bp, br, bw: (bp[i], 0)),
        scratch_shapes=[pltpu.VMEM((D_MODEL, D_MODEL), BF16)] * 3,
    )
    return pl.pallas_call(
        _expert_kernel,
        grid_spec=grid_spec,
        out_shape=jax.ShapeDtypeStruct((n_blocks * ROW_BLOCK * ROW_TILE, LANES), F32),
        compiler_params=pltpu.CompilerParams(
            dimension_semantics=("arbitrary",),
            vmem_limit_bytes=VMEM_LIMIT_BYTES),
        name="experts",
    )(blk_e, n_used, blk_page, blk_rows, blk_w, xs, w_gate, b_gate[:, None, :], w_up, b_up[:, None, :],
      w_down, b_down[:, None, :])


def _combine_kernel(final, dest_ref, dest_next_ref, ys_ref, x1_ref, route_ref, gf_ref, out_ref,
                    buf, sems):
    i = pl.program_id(0)
    n_steps = pl.num_programs(0)

    def gather_copy(idx_ref, s, k, r):
        return pltpu.make_async_copy(ys_ref.at[_tile_rows(idx_ref[k * COMBINE_TOKENS + r]), :],
                                     buf.at[s, k, _tile_rows(r), :], sems.at[s])

    def wait_tile(s):
        for k in range(TOP_K):
            pltpu.make_async_copy(ys_ref.at[pl.ds(0, COMBINE_TOKENS * ROW_TILE), :],
                                  buf.at[s, k], sems.at[s]).wait()

    @pl.when(i == 0)
    def _():
        def row_body(it, carry):
            for j in range(ROWS_PER_ITER):
                r = it * ROWS_PER_ITER + j
                for k in range(TOP_K):
                    gather_copy(dest_ref, 0, k, r).start(priority=k % 2)
            return carry

        lax.fori_loop(0, COMBINE_TOKENS // ROWS_PER_ITER, row_body, 0)

    def step_body(s):
        wait_tile(s)
        for r in range(COMBINE_TOKENS):
            for k in range(TOP_K):
                gather_copy(dest_next_ref, 1 - s, k, r).start(priority=k % 2)
        route = route_ref[...].T
        acc = x1_ref[...]
        for k in range(TOP_K):
            acc = acc + route[:, TOP_K + k:TOP_K + k + 1] * _from_row_tiles(
                buf.at[s, k], COMBINE_TOKENS)
        out_ref[...] = _rms(acc, gf_ref[...]) if final else acc

        @pl.when(i == n_steps - 1)
        def _():
            wait_tile(1 - s)

    for s in range(2):
        @pl.when(lax.rem(i, 2) == s)
        def _():
            step_body(s)


def _combine(dest, ys, x1, route, gf, final):
    n = x1.shape[0]
    n_steps = n // COMBINE_TOKENS
    idx_block = (COMBINE_TOKENS * TOP_K,)
    return pl.pallas_call(
        functools.partial(_combine_kernel, final),
        grid=(n_steps,),
        in_specs=[
            pl.BlockSpec(idx_block, lambda i: (i,), memory_space=pltpu.SMEM),
            pl.BlockSpec(idx_block, lambda i: (jnp.minimum(i + 1, n_steps - 1),),
                         memory_space=pltpu.SMEM),
            pl.BlockSpec(memory_space=pl.ANY),
            pl.BlockSpec((COMBINE_TOKENS, D_MODEL), lambda i: (i, 0)),
            pl.BlockSpec((ROUTE_ROWS, COMBINE_TOKENS), lambda i: (0, i)),
            pl.BlockSpec((1, D_MODEL), lambda i: (0, 0)),
        ],
        out_specs=pl.BlockSpec((COMBINE_TOKENS, D_MODEL), lambda i: (i, 0)),
        out_shape=jax.ShapeDtypeStruct((n, D_MODEL), F32),
        scratch_shapes=[
            pltpu.VMEM((2, TOP_K, COMBINE_TOKENS * ROW_TILE, LANES), F32),
            pltpu.SemaphoreType.DMA((2,)),
        ],
        compiler_params=pltpu.CompilerParams(
            dimension_semantics=("arbitrary",),
            vmem_limit_bytes=VMEM_LIMIT_BYTES),
        name="combine",
    )(dest, dest, ys, x1, route, gf)


def kernel(x, norm1_g, w_in, hgrn_lb, hgrn_norm_g, pool_w, pool_scale, w_out, norm2_g,
           router_w, router_b, w_gate, b_gate, w_up, b_up, w_down, b_down, norm_f_g):
    depth = w_in.shape[0]
    batch, seq, _ = x.shape
    n = batch * seq
    n_pages = n * TOP_K // ROW_BLOCK + N_EXPERTS
    lb_all = jnp.cumsum(jax.nn.softmax(hgrn_lb.astype(F32), axis=0), axis=0)

    xt = x.reshape(n, D_MODEL)
    for l in range(depth):
        wr = router_w[l].T
        wr_hi = wr.astype(BF16)
        wr_lo = (wr - wr_hi.astype(F32)).astype(BF16)
        br = router_b[l][:, None]
        x1, route, meta, xs = _mixer(
            xt, norm1_g[l][None, :], w_in[l].astype(BF16), lb_all[l][None, :],
            hgrn_norm_g[l][None, :], pool_w[l].astype(BF16), pool_scale[l][None, :],
            w_out[l].astype(BF16), norm2_g[l][None, :], wr_hi, wr_lo, br, seq, n_pages)

        page_expert = meta[0, :n_pages].astype(jnp.int32)
        n_used = meta[1, 0:1].astype(jnp.int32)
        page_ids = jnp.arange(n_pages, dtype=jnp.int32)
        key = jnp.where(page_expert < 0, N_EXPERTS, page_expert) * PAGE_PAD + page_ids
        rank = jnp.sum((key[None, :] < key[:, None]).astype(jnp.int32), axis=1)
        at_step = rank[None, :] == page_ids[:, None]
        blk_page = jnp.sum(jnp.where(at_step, page_ids[None, :], 0), axis=1)
        blk_e = jnp.clip(jnp.sum(jnp.where(at_step, page_expert[None, :], 0), axis=1),
                         0, N_EXPERTS - 1)
        tail_rows = meta[2, :n_pages].astype(jnp.int32)
        page_rows = jnp.where(tail_rows > 0, tail_rows, ROW_BLOCK)
        blk_rows = jnp.sum(jnp.where(at_step, page_rows[None, :], 0), axis=1)
        first_page = (page_ids == 0) | (blk_e != jnp.roll(blk_e, 1))
        later = (page_ids < n_used[0])[None, :] & (blk_e[None, :] > blk_e[:, None])
        next_e = jnp.min(jnp.where(later, blk_e[None, :], N_EXPERTS), axis=1)
        blk_w = jnp.where(first_page | (next_e == N_EXPERTS), blk_e, next_e)
        dest = route[0:TOP_K].astype(jnp.int32).reshape(TOP_K, n // COMBINE_TOKENS, COMBINE_TOKENS)
        dest = dest.transpose(1, 0, 2).reshape(n * TOP_K)

        ys = _experts(blk_e, n_used, blk_page, blk_rows, blk_w, xs, n_pages, w_gate[l], b_gate[l],
                      w_up[l], b_up[l], w_down[l], b_down[l])
        xt = _combine(dest, ys, x1, route, norm_f_g[None, :], l == depth - 1)
    return xt.reshape(batch, seq, D_MODEL)
```

```python
import functools
import math

import numpy as np
import jax
import jax.numpy as jnp
from jax import lax
from jax.experimental import pallas as pl
from jax.experimental.pallas import tpu as pltpu

F32 = jnp.float32
BF16 = jnp.bfloat16

D_MODEL = 1024
HGRN_WIDTH = 512
HEAD_DIM = 128
N_HEADS = HGRN_WIDTH // HEAD_DIM
POOL_WIDTH = 512
POOL_WINDOWS = (2, 4, 8, 16)
POOL_GW = POOL_WIDTH // len(POOL_WINDOWS)
IN_COLS = 4 * HGRN_WIDTH + POOL_WIDTH
N_EXPERTS = 32
TOP_K = 4
SWIGLU_LIMIT = 7.0
SWIGLU_ALPHA = 1.702
NORM_EPS = 1e-6

LANES = 128
ROW_TILE = D_MODEL // LANES
VMEM_LIMIT_BYTES = 56 * 1024 * 1024

MIX_TOKENS = 512
CHUNK = 64
N_LEVELS = int(math.log2(CHUNK))
HALO = max(POOL_WINDOWS)
FIRST_COPY_TOKENS = 320
FF_SLAB = 512
ROW_BLOCK = 512
ZERO_ROWS = ROW_BLOCK // 2
PAGE_PAD = 384
COMBINE_TOKENS = 256
ROWS_PER_ITER = 32
ROUTE_ROWS = 16
LOG2_E = 1.4426950408889634


def _dot(a, b):
    return jnp.dot(a, b, preferred_element_type=F32)


def _dot_nt(a, b):
    return lax.dot_general(a, b, (((1,), (1,)), ((), ())), preferred_element_type=F32)


def _dot_tn(a, b):
    return lax.dot_general(a, b, (((0,), (0,)), ((), ())), preferred_element_type=F32)


def _sigmoid(x):
    return 1.0 / (1.0 + jnp.exp2(x * (-LOG2_E)))


def _rms(x, g):
    return x * lax.rsqrt(jnp.mean(x * x, axis=-1, keepdims=True) + NORM_EPS) * g


def _split_bf16(x):
    hi = x.astype(BF16)
    lo = (x - hi.astype(F32)).astype(BF16)
    return hi, lo


def _to_row_tiles(ref, value):
    t = value.shape[0]
    for j in range(ROW_TILE):
        ref[pl.ds(j, t, stride=ROW_TILE), :] = value[:, j * LANES:(j + 1) * LANES]


def _from_row_tiles(ref, t):
    return jnp.concatenate(
        [ref[pl.ds(j, t, stride=ROW_TILE), :] for j in range(ROW_TILE)], axis=1)


def _tile_rows(row):
    return pl.ds(pl.multiple_of(row * ROW_TILE, ROW_TILE), ROW_TILE)


def _decay_tables(ch):
    t = np.arange(ch)[:, None]
    u = np.arange(ch)[None, :]
    mats = [u <= t, u > t]
    masks = []
    m = 1
    while m < ch:
        right = (t // m) % 2 == 1
        mid = (t // (2 * m)) * (2 * m) + m
        mats.append(np.where(right, (u >= mid) & (u <= t), (u > t) & (u <= mid - 1)))
        masks.append(right & ((u // m) % 2 == 0) & ((u // (2 * m)) == (t // (2 * m))))
        m *= 2
    return (np.concatenate(mats, 0).astype(np.float32),
            np.stack(masks).astype(np.float32))


def _hgrn_chunk(qp, fp, v, gp, lb, hg, tabs, masks_ref, st_ref):
    heads = [slice(h * HEAD_DIM, (h + 1) * HEAD_DIM) for h in range(N_HEADS)]
    q = qp * _sigmoid(qp)
    f = lb + (1.0 - lb) * _sigmoid(fp)
    lf2 = jnp.log(f) * LOG2_E
    k = 1.0 - f
    x = jnp.exp2(_dot(tabs, jnp.concatenate(_split_bf16(lf2), axis=0)))
    eb = x[0:CHUNK]
    esuf = x[CHUNK:2 * CHUNK]
    vb = v.astype(BF16)
    q_in = (q * eb).astype(BF16)
    k_end = (k * esuf).astype(BF16)
    qk = q * k
    st = [st_ref[h] for h in range(N_HEADS)]
    o = [_dot_nt(q_in[:, hs], st[h].astype(BF16)) for h, hs in enumerate(heads)]
    upd = [_dot_tn(vb[:, hs], k_end[:, hs]) for hs in heads]
    s = [jnp.zeros((CHUNK, CHUNK), F32) for _ in heads]
    row = lax.broadcasted_iota(jnp.int32, (CHUNK, HGRN_WIDTH), 0)
    for lvl in range(N_LEVELS):
        m = 1 << lvl
        if m % 8 == 0:
            qk_rows = jnp.concatenate(
                [(q if (r0 // m) % 2 else k)[r0:r0 + m] for r0 in range(0, CHUNK, m)], axis=0)
        else:
            qk_rows = jnp.where((row & m) != 0, q, k)
        z = (qk_rows * x[(2 + lvl) * CHUNK:(3 + lvl) * CHUNK]).astype(BF16)
        mask = masks_ref[lvl] != 0.0
        for h, hs in enumerate(heads):
            s[h] = jnp.where(mask, _dot_nt(z[:, hs], z[:, hs]), s[h])
    outs = []
    for h, hs in enumerate(heads):
        oh = o[h] + jnp.sum(qk[:, hs], axis=-1, keepdims=True) * v[:, hs]
        oh = oh + _dot(s[h].astype(BF16), vb[:, hs])
        st_ref[h] = eb[CHUNK - 1:CHUNK, hs] * st[h] + upd[h]
        oh = oh * lax.rsqrt(jnp.mean(oh * oh, axis=-1, keepdims=True) + NORM_EPS) * hg
        outs.append(oh)
    return jnp.concatenate(outs, axis=1) * (gp * _sigmoid(gp))


def _row_copy(src, src_row, dst, dst_row, sem):
    return pltpu.make_async_copy(src.at[_tile_rows(src_row), :], dst.at[_tile_rows(dst_row), :], sem)


def _mixer_kernel(n_pages,
                  x_ref, g1_ref, win_ref, lb_ref, hg_ref, pw_ref, ps_ref, wout_ref, g2_ref,
                  wrh_ref, wrl_ref, br_ref, tabs_ref, masks_ref, tri_ref, trie_ref,
                  x1_ref, route_ref, meta_ref, xs_ref,
                  proj_scr, mix_scr, st_scr, ubuf_scr, cnt_scr, page_scr, nfree_scr, pexp_scr,
                  h2buf, dst_vmem, dst_smem, tail_vmem, tail_smem, zblk, row_sem, idx_sem, zsem):
    b_i = pl.program_id(0)
    s_i = pl.program_id(1)
    step = b_i * pl.num_programs(1) + s_i
    n_steps = pl.num_programs(0) * pl.num_programs(1)
    slot = lax.rem(step, 2)
    prev = 1 - slot

    @pl.when(s_i == 0)
    def _():
        st_scr[...] = jnp.zeros_like(st_scr)
        ubuf_scr[0:HALO, :] = jnp.zeros((HALO, POOL_WIDTH), F32)

    @pl.when(step == 0)
    def _():
        cnt_scr[...] = jnp.zeros_like(cnt_scr)
        page_scr[...] = jnp.zeros_like(page_scr)
        nfree_scr[...] = jnp.zeros_like(nfree_scr)
        pexp_scr[...] = jnp.full(pexp_scr.shape, -1.0, F32)
        zblk[...] = jnp.zeros_like(zblk)
        h2buf[1] = jnp.zeros(h2buf.shape[1:], F32)

        def spare_body(r, carry):
            for k in range(TOP_K):
                dst_smem[k, r] = n_pages * ROW_BLOCK + r * TOP_K + k
            return carry

        lax.fori_loop(0, MIX_TOKENS, spare_body, 0)

    @pl.when(step > 0)
    def _():
        pltpu.make_async_copy(dst_vmem, dst_smem, idx_sem).wait()
        for k in range(TOP_K):
            pltpu.make_async_copy(h2buf.at[slot], xs_ref.at[pl.ds(0, MIX_TOKENS * ROW_TILE), :],
                                  row_sem).wait()

    @pl.when(step >= 0)
    def _():
        for r in range(FIRST_COPY_TOKENS):
            for k in range(TOP_K):
                _row_copy(h2buf.at[prev], r, xs_ref, dst_smem[k, r], row_sem).start(priority=k % 2)
        h = _rms(x_ref[...], g1_ref[...]).astype(BF16)
        proj_scr[...] = _dot(h, win_ref[...])

    tabs = tabs_ref[...]

    for c in range(MIX_TOKENS // CHUNK):
        rows = slice(c * CHUNK, (c + 1) * CHUNK)
        out = _hgrn_chunk(
            proj_scr[rows, 0:HGRN_WIDTH],
            proj_scr[rows, HGRN_WIDTH:2 * HGRN_WIDTH],
            proj_scr[rows, 2 * HGRN_WIDTH:3 * HGRN_WIDTH],
            proj_scr[rows, 3 * HGRN_WIDTH:4 * HGRN_WIDTH],
            lb_ref[...], hg_ref[...], tabs, masks_ref, st_scr)
        mix_scr[rows, 0:HGRN_WIDTH] = out.astype(BF16)

    ubuf_scr[HALO:HALO + MIX_TOKENS, :] = proj_scr[:, 4 * HGRN_WIDTH:IN_COLS]
    pos = (s_i * MIX_TOKENS + 1
           + lax.broadcasted_iota(jnp.int32, (MIX_TOKENS, 1), 0)).astype(F32)
    for g, win in enumerate(POOL_WINDOWS):
        lo = g * POOL_GW
        ext = ubuf_scr[:, lo:lo + POOL_GW]
        acc = ext
        span = 1
        while span < win:
            acc = acc + pltpu.roll(acc, span, axis=0)
            span *= 2
        u = ext[HALO:]
        d = acc[HALO:] / jnp.minimum(pos, float(win)) - u
        y = _dot(d.astype(BF16), pw_ref[g]) * ps_ref[:, lo:lo + POOL_GW]
        mix_scr[:, HGRN_WIDTH + lo:HGRN_WIDTH + lo + POOL_GW] = y.astype(BF16)
    ubuf_scr[0:HALO, :] = ubuf_scr[MIX_TOKENS:MIX_TOKENS + HALO, :]

    @pl.when(step >= 0)
    def _():
        for r in range(FIRST_COPY_TOKENS, MIX_TOKENS):
            for k in range(TOP_K):
                _row_copy(h2buf.at[prev], r, xs_ref, dst_smem[k, r], row_sem).start(priority=k % 2)
        x1_ref[...] = x_ref[...] + _dot(mix_scr[...], wout_ref[...])

    x1 = x1_ref[...]
    h2 = _rms(x1, g2_ref[...])

    h_hi, h_lo = _split_bf16(h2)
    wrh = wrh_ref[...]
    logits = (_dot_nt(wrh, h_hi) + _dot_nt(wrh, h_lo) + _dot_nt(wrl_ref[...], h_hi)
              + br_ref[...])
    eidx = lax.broadcasted_iota(jnp.int32, (N_EXPERTS, MIX_TOKENS), 0)
    work = logits
    tops, sels = [], []
    for _ in range(TOP_K):
        m = jnp.max(work, axis=0, keepdims=True)
        idx = jnp.min(jnp.where(work == m, eidx, N_EXPERTS), axis=0, keepdims=True)
        sel = eidx == idx
        work = jnp.where(sel, -jnp.inf, work)
        tops.append(m)
        sels.append(sel)
    exps = [jnp.exp(m - tops[0]) for m in tops]
    denom = exps[0] + exps[1] + exps[2] + exps[3]
    gates = [e / denom for e in exps]

    sel_any = jnp.zeros((N_EXPERTS, MIX_TOKENS), F32)
    for sel in sels:
        sel_any = jnp.where(sel, 1.0, sel_any)
    cnt = cnt_scr[...]
    before = _dot(sel_any.astype(BF16), tri_ref[...]) + cnt
    cnt_new = cnt + jnp.sum(sel_any, axis=1, keepdims=True)
    cnt_scr[...] = cnt_new

    shift = int(math.log2(ROW_BLOCK))
    before_i = before.astype(jnp.int32)
    cnt_i = cnt.astype(jnp.int32)
    page_lo = cnt_i >> shift
    starts_page = (cnt_i & (ROW_BLOCK - 1)) == 0
    page_hi = (cnt_new.astype(jnp.int32) - 1) >> shift
    need = (cnt_new > cnt) & (starts_page | (page_hi > page_lo))
    need_f = jnp.where(need, 1.0, 0.0)
    earlier = _dot(trie_ref[...], jnp.broadcast_to(need_f, (N_EXPERTS, LANES)).astype(BF16))
    nfree = nfree_scr[...]
    new_id = nfree + earlier[:, 0:1]
    new_idx = jnp.where(starts_page, page_lo, page_lo + 1)
    cur_page = page_scr[...]
    page = jnp.where(need & ((before_i >> shift) == new_idx), new_id, cur_page)
    dest_all = page * float(ROW_BLOCK) + (before_i & (ROW_BLOCK - 1)).astype(F32)
    page_scr[...] = jnp.where(need, new_id, cur_page)
    nfree_new = nfree + jnp.sum(need_f, axis=0, keepdims=True)
    nfree_scr[...] = nfree_new
    page_lane = lax.broadcasted_iota(jnp.int32, (N_EXPERTS, PAGE_PAD), 1).astype(F32)
    owner = jnp.where(need & (page_lane == new_id), eidx[:, 0:1].astype(F32), -1.0)
    pexp = jnp.maximum(pexp_scr[...], jnp.max(owner, axis=0, keepdims=True))
    pexp_scr[...] = pexp

    dests = []
    for k in range(TOP_K):
        dest = jnp.sum(jnp.where(sels[k], dest_all, 0.0), axis=0, keepdims=True)
        dests.append(dest)
        route_ref[k:k + 1, :] = dest
        route_ref[TOP_K + k:TOP_K + k + 1, :] = gates[k]
    route_ref[2 * TOP_K:, :] = jnp.zeros((ROUTE_ROWS - 2 * TOP_K, MIX_TOKENS), F32)
    used = (cnt_new.astype(jnp.int32) - 1) & (ROW_BLOCK - 1)
    last_page = (cnt_new > 0.0) & (page_lane == page_scr[...])
    tail_rows = jnp.max(jnp.where(last_page, (used + 1).astype(F32), 0.0), axis=0, keepdims=True)
    meta_ref[0:1, :] = pexp
    meta_ref[1:2, :] = jnp.broadcast_to(nfree_new[0:1, :], (1, PAGE_PAD))
    meta_ref[2:3, :] = tail_rows
    meta_ref[3:, :] = jnp.zeros((5, PAGE_PAD), F32)

    _to_row_tiles(h2buf.at[slot], h2)
    dst_vmem[...] = jnp.concatenate(
        dests + [jnp.zeros((8 - TOP_K, MIX_TOKENS), F32)], axis=0).astype(jnp.int32)
    pltpu.make_async_copy(dst_vmem, dst_smem, idx_sem).start()

    @pl.when(step == n_steps - 1)
    def _():
        pltpu.make_async_copy(dst_vmem, dst_smem, idx_sem).wait()
        for k in range(TOP_K):
            pltpu.make_async_copy(h2buf.at[prev], xs_ref.at[pl.ds(0, MIX_TOKENS * ROW_TILE), :],
                                  row_sem).wait()

        def row_body(it, carry):
            for j in range(ROWS_PER_ITER):
                r = it * ROWS_PER_ITER + j
                for k in range(TOP_K):
                    _row_copy(h2buf.at[slot], r, xs_ref, dst_smem[k, r],
                              row_sem).start(priority=k % 2)
            return carry

        lax.fori_loop(0, MIX_TOKENS // ROWS_PER_ITER, row_body, 0)
        for k in range(TOP_K):
            pltpu.make_async_copy(h2buf.at[slot], xs_ref.at[pl.ds(0, MIX_TOKENS * ROW_TILE), :],
                                  row_sem).wait()

        tail_vmem[0] = jnp.broadcast_to(cnt_new, (N_EXPERTS, LANES)).astype(jnp.int32)
        tail_vmem[1] = jnp.broadcast_to(page_scr[...], (N_EXPERTS, LANES)).astype(jnp.int32)
        tail_vmem[2] = jnp.broadcast_to(nfree_new, (N_EXPERTS, LANES)).astype(jnp.int32)
        cp = pltpu.make_async_copy(tail_vmem, tail_smem, idx_sem)
        cp.start()
        cp.wait()

        def zero_copy(first_row, rows):
            return pltpu.make_async_copy(
                zblk.at[pl.ds(0, rows * ROW_TILE), :],
                xs_ref.at[pl.ds(pl.multiple_of(first_row * ROW_TILE, ROW_TILE), rows * ROW_TILE), :],
                zsem)

        def tail_pass(wait):
            def body(e, carry):
                used = tail_smem[0, e, 0] & (ROW_BLOCK - 1)
                first = tail_smem[1, e, 0] * ROW_BLOCK + used
                left = jnp.where(used == 0, 0, ROW_BLOCK - used)
                rows = ZERO_ROWS
                while rows >= 1:
                    @pl.when((left & rows) != 0)
                    def _():
                        cp = zero_copy(first, rows)
                        cp.wait() if wait else cp.start()
                    first = first + jnp.where((left & rows) != 0, rows, 0)
                    rows //= 2
                return carry

            lax.fori_loop(0, N_EXPERTS, body, 0)

        def page_pass(wait):
            def body(p, carry):
                for half in range(ROW_BLOCK // ZERO_ROWS):
                    cp = zero_copy(p * ROW_BLOCK + half * ZERO_ROWS, ZERO_ROWS)
                    cp.wait() if wait else cp.start()
                return carry

            lax.fori_loop(tail_smem[2, 0, 0], n_pages, body, 0)

        tail_pass(False)
        page_pass(False)
        tail_pass(True)
        page_pass(True)


def _mixer(x2d, g1, w_in, lb, hg, pool_w, pool_scale, w_out, g2, wr_hi, wr_lo, br, seq, n_pages):
    n = x2d.shape[0]
    steps_per_seq = seq // MIX_TOKENS
    assert MIX_TOKENS <= ROW_BLOCK and n_pages <= PAGE_PAD
    spare_pages = MIX_TOKENS * TOP_K // ROW_BLOCK
    tabs_np, masks_np = _decay_tables(CHUNK)
    tabs = jnp.asarray(np.concatenate([tabs_np, tabs_np], axis=1), BF16)
    masks = jnp.asarray(masks_np, F32)
    tri = jnp.asarray(np.triu(np.ones((MIX_TOKENS, MIX_TOKENS), np.float32), 1), BF16)
    tri_e = jnp.asarray(np.tril(np.ones((N_EXPERTS, N_EXPERTS), np.float32), -1), BF16)

    def tok(b, s):
        return (b * steps_per_seq + s, 0)

    def tok_col(b, s):
        return (0, b * steps_per_seq + s)

    def const2(b, s):
        return (0, 0)

    def const3(b, s):
        return (0, 0, 0)

    in_specs = [
        pl.BlockSpec((MIX_TOKENS, D_MODEL), tok),
        pl.BlockSpec((1, D_MODEL), const2),
        pl.BlockSpec((D_MODEL, IN_COLS), const2),
        pl.BlockSpec((1, HGRN_WIDTH), const2),
        pl.BlockSpec((1, HEAD_DIM), const2),
        pl.BlockSpec((len(POOL_WINDOWS), POOL_GW, POOL_GW), const3),
        pl.BlockSpec((1, POOL_WIDTH), const2),
        pl.BlockSpec((D_MODEL, D_MODEL), const2),
        pl.BlockSpec((1, D_MODEL), const2),
        pl.BlockSpec((N_EXPERTS, D_MODEL), const2),
        pl.BlockSpec((N_EXPERTS, D_MODEL), const2),
        pl.BlockSpec((N_EXPERTS, 1), const2),
        pl.BlockSpec(tabs.shape, const2),
        pl.BlockSpec(masks.shape, const3),
        pl.BlockSpec(tri.shape, const2),
        pl.BlockSpec(tri_e.shape, const2),
    ]
    out_specs = [
        pl.BlockSpec((MIX_TOKENS, D_MODEL), tok),
        pl.BlockSpec((ROUTE_ROWS, MIX_TOKENS), tok_col),
        pl.BlockSpec((8, PAGE_PAD), const2),
        pl.BlockSpec(memory_space=pl.ANY),
    ]
    out_shape = [
        jax.ShapeDtypeStruct((n, D_MODEL), F32),
        jax.ShapeDtypeStruct((ROUTE_ROWS, n), F32),
        jax.ShapeDtypeStruct((8, PAGE_PAD), F32),
        jax.ShapeDtypeStruct(((n_pages + spare_pages) * ROW_BLOCK * ROW_TILE, LANES), F32),
    ]
    scratch = [
        pltpu.VMEM((MIX_TOKENS, IN_COLS), F32),
        pltpu.VMEM((MIX_TOKENS, D_MODEL), BF16),
        pltpu.VMEM((N_HEADS, HEAD_DIM, HEAD_DIM), F32),
        pltpu.VMEM((HALO + MIX_TOKENS, POOL_WIDTH), F32),
        pltpu.VMEM((N_EXPERTS, 1), F32),
        pltpu.VMEM((N_EXPERTS, 1), F32),
        pltpu.VMEM((N_EXPERTS, 1), F32),
        pltpu.VMEM((1, PAGE_PAD), F32),
        pltpu.VMEM((2, MIX_TOKENS * ROW_TILE, LANES), F32),
        pltpu.VMEM((8, MIX_TOKENS), jnp.int32),
        pltpu.SMEM((8, MIX_TOKENS), jnp.int32),
        pltpu.VMEM((3, N_EXPERTS, LANES), jnp.int32),
        pltpu.SMEM((3, N_EXPERTS, LANES), jnp.int32),
        pltpu.VMEM((ZERO_ROWS * ROW_TILE, LANES), F32),
        pltpu.SemaphoreType.DMA,
        pltpu.SemaphoreType.DMA,
        pltpu.SemaphoreType.DMA,
    ]
    return pl.pallas_call(
        functools.partial(_mixer_kernel, n_pages),
        grid=(n // seq, steps_per_seq),
        in_specs=in_specs,
        out_specs=out_specs,
        out_shape=out_shape,
        scratch_shapes=scratch,
        compiler_params=pltpu.CompilerParams(
            dimension_semantics=("arbitrary", "arbitrary"),
            vmem_limit_bytes=VMEM_LIMIT_BYTES),
        name="mixer",
    )(x2d, g1, w_in, lb, hg, pool_w, pool_scale, w_out, g2, wr_hi, wr_lo, br, tabs, masks, tri,
      tri_e)


def _expert_kernel(blk_e_ref, nused_ref, blk_page_ref, blk_rows_ref, blk_w_ref, xs_ref, wg_ref,
                   bg_ref, wu_ref, bu_ref, wd_ref, bd_ref, ys_ref, wg_s, wu_s, wd_s):
    i = pl.program_id(0)
    active = i < nused_ref[0]
    half_page = blk_rows_ref[i] <= ROW_BLOCK // 2
    changed = (i == 0) | (blk_e_ref[i] != blk_e_ref[jnp.maximum(i - 1, 0)])

    @pl.when(active & changed)
    def _():
        wg_s[...] = wg_ref[0].astype(BF16)
        wu_s[...] = wu_ref[0].astype(BF16)
        wd_s[...] = wd_ref[0].astype(BF16)

    def ffn(rows):
        x_rows = xs_ref.at[pl.ds(0, rows * ROW_TILE), :]
        xb = _from_row_tiles(x_rows, rows).astype(BF16)
        y = bd_ref[0]
        for c0 in range(0, D_MODEL, FF_SLAB):
            cs = slice(c0, c0 + FF_SLAB)
            gt = jnp.minimum(_dot(xb, wg_s[:, cs]) + bg_ref[0][:, cs], SWIGLU_LIMIT)
            up = jnp.clip(_dot(xb, wu_s[:, cs]) + bu_ref[0][:, cs], -SWIGLU_LIMIT, SWIGLU_LIMIT)
            act = (up + 1.0) * (gt * _sigmoid(SWIGLU_ALPHA * gt))
            y = y + _dot(act.astype(BF16), wd_s[cs, :])
        _to_row_tiles(ys_ref.at[pl.ds(0, rows * ROW_TILE), :], y)

    @pl.when(active & jnp.logical_not(half_page))
    def _():
        ffn(ROW_BLOCK)

    @pl.when(active & half_page)
    def _():
        ffn(ROW_BLOCK // 2)
        ys_ref[pl.ds(ROW_BLOCK // 2 * ROW_TILE, ROW_BLOCK // 2 * ROW_TILE), :] = jnp.zeros(
            (ROW_BLOCK // 2 * ROW_TILE, LANES), F32)

    @pl.when(jnp.logical_not(active))
    def _():
        ys_ref[...] = jnp.zeros_like(ys_ref)


def _experts(blk_e, n_used, blk_page, blk_rows, blk_w, xs, n_blocks, w_gate, b_gate, w_up, b_up,
             w_down, b_down):
    rows_spec_shape = (ROW_BLOCK * ROW_TILE, LANES)

    def blk(i, be, nu):
        return jnp.minimum(i, nu[0] - 1)

    def row_map(i, be, nu, bp, br, bw):
        return (bp[blk(i, be, nu)], 0)

    def w_map(i, be, nu, bp, br, bw):
        return (bw[blk(i, be, nu)], 0, 0)

    def b_map(i, be, nu, bp, br, bw):
        return (be[blk(i, be, nu)], 0, 0)

    w_spec = pl.BlockSpec((1, D_MODEL, D_MODEL), w_map)
    b_spec = pl.BlockSpec((1, 1, D_MODEL), b_map)
    grid_spec = pltpu.PrefetchScalarGridSpec(
        num_scalar_prefetch=5,
        grid=(n_blocks,),
        in_specs=[pl.BlockSpec(rows_spec_shape, row_map),
                  w_spec, b_spec, w_spec, b_spec, w_spec, b_spec],
        out_specs=pl.BlockSpec(rows_spec_shape, lambda i, be, nu, bp, br, bw: (bp[i], 0)),
        scratch_shapes=[pltpu.VMEM((D_MODEL, D_MODEL), BF16)] * 3,
    )
    return pl.pallas_call(
        _expert_kernel,
        grid_spec=grid_spec,
        out_shape=jax.ShapeDtypeStruct((n_blocks * ROW_BLOCK * ROW_TILE, LANES), F32),
        compiler_params=pltpu.CompilerParams(
            dimension_semantics=("arbitrary",),
            vmem_limit_bytes=VMEM_LIMIT_BYTES),
        name="experts",
    )(blk_e, n_used, blk_page, blk_rows, blk_w, xs, w_gate, b_gate[:, None, :], w_up, b_up[:, None, :],
      w_down, b_down[:, None, :])


def _combine_kernel(final, dest_ref, dest_next_ref, ys_ref, x1_ref, route_ref, gf_ref, out_ref,
                    buf, sems):
    i = pl.program_id(0)
    n_steps = pl.num_programs(0)

    def gather_copy(idx_ref, s, k, r):
        return pltpu.make_async_copy(ys_ref.at[_tile_rows(idx_ref[k * COMBINE_TOKENS + r]), :],
                                     buf.at[s, k, _tile_rows(r), :], sems.at[s])

    def wait_tile(s):
        for k in range(TOP_K):
            pltpu.make_async_copy(ys_ref.at[pl.ds(0, COMBINE_TOKENS * ROW_TILE), :],
                                  buf.at[s, k], sems.at[s]).wait()

    @pl.when(i == 0)
    def _():
        def row_body(it, carry):
            for j in range(ROWS_PER_ITER):
                r = it * ROWS_PER_ITER + j
                for k in range(TOP_K):
                    gather_copy(dest_ref, 0, k, r).start(priority=k % 2)
            return carry

        lax.fori_loop(0, COMBINE_TOKENS // ROWS_PER_ITER, row_body, 0)

    def step_body(s):
        wait_tile(s)
        for r in range(COMBINE_TOKENS):
            for k in range(TOP_K):
                gather_copy(dest_next_ref, 1 - s, k, r).start(priority=k % 2)
        route = route_ref[...].T
        acc = x1_ref[...]
        for k in range(TOP_K):
            acc = acc + route[:, TOP_K + k:TOP_K + k + 1] * _from_row_tiles(
                buf.at[s, k], COMBINE_TOKENS)
        out_ref[...] = _rms(acc, gf_ref[...]) if final else acc

        @pl.when(i == n_steps - 1)
        def _():
            wait_tile(1 - s)

    for s in range(2):
        @pl.when(lax.rem(i, 2) == s)
        def _():
            step_body(s)


def _combine(dest, ys, x1, route, gf, final):
    n = x1.shape[0]
    n_steps = n // COMBINE_TOKENS
    idx_block = (COMBINE_TOKENS * TOP_K,)
    return pl.pallas_call(
        functools.partial(_combine_kernel, final),
        grid=(n_steps,),
        in_specs=[
            pl.BlockSpec(idx_block, lambda i: (i,), memory_space=pltpu.SMEM),
            pl.BlockSpec(idx_block, lambda i: (jnp.minimum(i + 1, n_steps - 1),),
                         memory_space=pltpu.SMEM),
            pl.BlockSpec(memory_space=pl.ANY),
            pl.BlockSpec((COMBINE_TOKENS, D_MODEL), lambda i: (i, 0)),
            pl.BlockSpec((ROUTE_ROWS, COMBINE_TOKENS), lambda i: (0, i)),
            pl.BlockSpec((1, D_MODEL), lambda i: (0, 0)),
        ],
        out_specs=pl.BlockSpec((COMBINE_TOKENS, D_MODEL), lambda i: (i, 0)),
        out_shape=jax.ShapeDtypeStruct((n, D_MODEL), F32),
        scratch_shapes=[
            pltpu.VMEM((2, TOP_K, COMBINE_TOKENS * ROW_TILE, LANES), F32),
            pltpu.SemaphoreType.DMA((2,)),
        ],
        compiler_params=pltpu.CompilerParams(
            dimension_semantics=("arbitrary",),
            vmem_limit_bytes=VMEM_LIMIT_BYTES),
        name="combine",
    )(dest, dest, ys, x1, route, gf)


def kernel(x, norm1_g, w_in, hgrn_lb, hgrn_norm_g, pool_w, pool_scale, w_out, norm2_g,
           router_w, router_b, w_gate, b_gate, w_up, b_up, w_down, b_down, norm_f_g):
    depth = w_in.shape[0]
    batch, seq, _ = x.shape
    n = batch * seq
    n_pages = n * TOP_K // ROW_BLOCK + N_EXPERTS
    lb_all = jnp.cumsum(jax.nn.softmax(hgrn_lb.astype(F32), axis=0), axis=0)

    xt = x.reshape(n, D_MODEL)
    for l in range(depth):
        wr = router_w[l].T
        wr_hi = wr.astype(BF16)
        wr_lo = (wr - wr_hi.astype(F32)).astype(BF16)
        br = router_b[l][:, None]
        x1, route, meta, xs = _mixer(
            xt, norm1_g[l][None, :], w_in[l].astype(BF16), lb_all[l][None, :],
            hgrn_norm_g[l][None, :], pool_w[l].astype(BF16), pool_scale[l][None, :],
            w_out[l].astype(BF16), norm2_g[l][None, :], wr_hi, wr_lo, br, seq, n_pages)

        page_expert = meta[0, :n_pages].astype(jnp.int32)
        n_used = meta[1, 0:1].astype(jnp.int32)
        page_ids = jnp.arange(n_pages, dtype=jnp.int32)
        key = jnp.where(page_expert < 0, N_EXPERTS, page_expert) * PAGE_PAD + page_ids
        rank = jnp.sum((key[None, :] < key[:, None]).astype(jnp.int32), axis=1)
        at_step = rank[None, :] == page_ids[:, None]
        blk_page = jnp.sum(jnp.where(at_step, page_ids[None, :], 0), axis=1)
        blk_e = jnp.clip(jnp.sum(jnp.where(at_step, page_expert[None, :], 0), axis=1),
                         0, N_EXPERTS - 1)
        tail_rows = meta[2, :n_pages].astype(jnp.int32)
        page_rows = jnp.where(tail_rows > 0, tail_rows, ROW_BLOCK)
        blk_rows = jnp.sum(jnp.where(at_step, page_rows[None, :], 0), axis=1)
        first_page = (page_ids == 0) | (blk_e != jnp.roll(blk_e, 1))
        later = (page_ids < n_used[0])[None, :] & (blk_e[None, :] > blk_e[:, None])
        next_e = jnp.min(jnp.where(later, blk_e[None, :], N_EXPERTS), axis=1)
        blk_w = jnp.where(first_page | (next_e == N_EXPERTS), blk_e, next_e)
        dest = route[0:TOP_K].astype(jnp.int32).reshape(TOP_K, n // COMBINE_TOKENS, COMBINE_TOKENS)
        dest = dest.transpose(1, 0, 2).reshape(n * TOP_K)

        ys = _experts(blk_e, n_used, blk_page, blk_rows, blk_w, xs, n_pages, w_gate[l], b_gate[l],
                      w_up[l], b_up[l], w_down[l], b_down[l])
        xt = _combine(dest, ys, x1, route, norm_f_g[None, :], l == depth - 1)
    return xt.reshape(batch, seq, D_MODEL)
```
